```python
import math
import jax, jax.numpy as jnp
from jax import lax
import numpy as np

D_MODEL = 1024
BATCH = 8
SEQ = 2048
DEPTH = 1

GLA_HEADS = 4
GLA_DK = D_MODEL // 2
GLA_DV = D_MODEL
GLA_HEAD_DK = GLA_DK // GLA_HEADS
GLA_HEAD_DV = GLA_DV // GLA_HEADS
GLA_GATE_RANK = 16
GLA_GATE_TAU = 16.0
GLA_CHUNK = 64
POOL_WIDTH = D_MODEL
POOL_WINDOWS = (2, 4, 8, 16)
POOL_GROUPS = len(POOL_WINDOWS)
POOL_GROUP_DIM = POOL_WIDTH // POOL_GROUPS
NORM_EPS = 1e-5
DEEPNORM_ALPHA = (2.0 * DEPTH) ** 0.25
DEEPNORM_BETA = (8.0 * DEPTH) ** -0.25

SPLIT_SIZES = (GLA_DK, GLA_DK, GLA_DV, GLA_DV, GLA_GATE_RANK,
               POOL_WIDTH, POOL_WIDTH, D_MODEL, D_MODEL)
D_IN = int(sum(SPLIT_SIZES))
SPLIT_POINTS = tuple(int(p) for p in np.cumsum(SPLIT_SIZES)[:-1])

kernel_name = "hybrid_gla_pool_gated_deepnorm"


def gla_chunked(q, k, v, log_a):
    B, S, H, dk = q.shape
    dv = v.shape[-1]
    n = S // GLA_CHUNK

    def to_chunks(t):
        return t.reshape(B, n, GLA_CHUNK, H, t.shape[-1]).transpose(1, 0, 3, 2, 4)

    qc, kc, vc, ac = to_chunks(q), to_chunks(k), to_chunks(v), to_chunks(log_a)
    causal = jnp.tril(jnp.ones((GLA_CHUNK, GLA_CHUNK), dtype=bool))[None, None, :, :, None]

    def step(state, inp):
        qi, ki, vi, ai = inp
        qi = qi.astype(jnp.float32)
        ki = ki.astype(jnp.float32)
        vi = vi.astype(jnp.float32)
        b = jnp.cumsum(ai.astype(jnp.float32), axis=2)
        o_inter = jnp.einsum('bhck,bhkv->bhcv', qi * jnp.exp(b), state)
        diff = b[:, :, :, None, :] - b[:, :, None, :, :]
        decay = jnp.exp(jnp.where(causal, diff, -jnp.inf))
        scores = jnp.einsum('bhik,bhjk,bhijk->bhij', qi, ki, decay)
        o_intra = jnp.einsum('bhij,bhjv->bhiv', scores, vi)
        b_last = b[:, :, -1:, :]
        k_dec = ki * jnp.exp(b_last - b)
        new_state = (jnp.exp(b_last[:, :, 0, :])[..., None] * state
                     + jnp.einsum('bhck,bhcv->bhkv', k_dec, vi))
        return new_state, o_inter + o_intra

    state0 = jnp.zeros((B, H, dk, dv), jnp.float32)
    _, o = lax.scan(step, state0, (qc, kc, vc, ac))
    return o.transpose(1, 0, 3, 2, 4).reshape(B, S, H, dv)


def causal_multiscale_pool(u):
    B, S, G, Cg = u.shape
    u32 = u.astype(jnp.float32)
    cs = jnp.cumsum(u32, axis=1)
    cs = jnp.concatenate([jnp.zeros_like(cs[:, :1]), cs], axis=1)
    t = jnp.arange(S)
    win = jnp.array(POOL_WINDOWS, dtype=jnp.int32)
    start = jnp.maximum(t[:, None] + 1 - win[None, :], 0)
    lower = cs[:, start, jnp.arange(G)[None, :]]
    count = jnp.minimum(t[:, None] + 1, win[None, :]).astype(jnp.float32)
    pooled = (cs[:, 1:] - lower) / count[None, :, :, None]
    return pooled - u32


def rms_norm_heads(o, w):
    o32 = o.astype(jnp.float32)
    o32 = o32 * lax.rsqrt(jnp.mean(jnp.square(o32), axis=-1, keepdims=True) + NORM_EPS)
    B, S = o.shape[:2]
    return o32.reshape(B, S, -1) * w


def layer_norm(r, w, b):
    r32 = r.astype(jnp.float32)
    mu = jnp.mean(r32, axis=-1, keepdims=True)
    var = jnp.mean(jnp.square(r32 - mu), axis=-1, keepdims=True)
    return (r32 - mu) * lax.rsqrt(var + NORM_EPS) * w + b


def hybrid_layer(x, w_in, w_gate_up, b_gate, gn_w, pool_w, pool_b, pool_scale,
                 w_a, w_b, w_o, ln_w, ln_b):
    B, S, _ = x.shape
    h = jnp.einsum('bsd,de->bse', x, w_in)
    q, k, v, g, a_low, u, z, gate_a, gate_b = jnp.split(h, SPLIT_POINTS, axis=-1)

    log_a = jax.nn.log_sigmoid(
        (jnp.einsum('bsr,rk->bsk', a_low, w_gate_up) + b_gate).astype(jnp.float32)) / GLA_GATE_TAU
    q = q.reshape(B, S, GLA_HEADS, GLA_HEAD_DK) * (GLA_HEAD_DK ** -0.5)
    k = k.reshape(B, S, GLA_HEADS, GLA_HEAD_DK)
    v = v.reshape(B, S, GLA_HEADS, GLA_HEAD_DV)
    log_a = log_a.reshape(B, S, GLA_HEADS, GLA_HEAD_DK)
    o = gla_chunked(q, k, v, log_a)
    y_a = rms_norm_heads(o, gn_w) * jax.nn.silu(g.astype(jnp.float32))

    p = causal_multiscale_pool(u.reshape(B, S, POOL_GROUPS, POOL_GROUP_DIM))
    p = jnp.einsum('bsgc,gcd->bsgd', p, pool_w).reshape(B, S, POOL_WIDTH) + pool_b
    y_b = p * pool_scale * jax.nn.silu(z.astype(jnp.float32))

    merged = (jax.nn.sigmoid(gate_a.astype(jnp.float32)) * jnp.einsum('bse,ed->bsd', y_a, w_a)
              + jax.nn.sigmoid(gate_b.astype(jnp.float32)) * jnp.einsum('bse,ed->bsd', y_b, w_b))
    out = jnp.einsum('bsd,de->bse', merged, w_o)

    return layer_norm(DEEPNORM_ALPHA * x.astype(jnp.float32) + out, ln_w, ln_b).astype(x.dtype)


def setup_inputs(seed: int = 0) -> dict:
    key = jax.random.key(seed)
    ks = jax.random.split(key, 14)
    L = DEPTH
    f32 = jnp.float32
    x = jax.random.normal(ks[0], (BATCH, SEQ, D_MODEL), f32)
    w_in = jax.random.normal(ks[1], (L, D_MODEL, D_IN), f32) * D_MODEL ** -0.5
    w_gate_up = jax.random.normal(ks[2], (L, GLA_GATE_RANK, GLA_DK), f32) * GLA_GATE_RANK ** -0.5
    b_gate = 0.1 * jax.random.normal(ks[3], (L, GLA_DK), f32)
    gn_w = 1.0 + 0.02 * jax.random.normal(ks[4], (L, GLA_DV), f32)
    pool_w = jax.random.normal(ks[5], (L, POOL_GROUPS, POOL_GROUP_DIM, POOL_GROUP_DIM), f32) * POOL_GROUP_DIM ** -0.5
    pool_b = 0.02 * jax.random.normal(ks[6], (L, POOL_WIDTH), f32)
    pool_scale = 1.0 + 0.02 * jax.random.normal(ks[7], (L, POOL_WIDTH), f32)
    w_a = jax.random.normal(ks[8], (L, GLA_DV, D_MODEL), f32) * (GLA_DV ** -0.5) * DEEPNORM_BETA
    w_b = jax.random.normal(ks[9], (L, POOL_WIDTH, D_MODEL), f32) * (POOL_WIDTH ** -0.5) * DEEPNORM_BETA
    w_o = jax.random.normal(ks[10], (L, D_MODEL, D_MODEL), f32) * (D_MODEL ** -0.5) * DEEPNORM_BETA
    ln_w = 1.0 + 0.02 * jax.random.normal(ks[11], (L, D_MODEL), f32)
    ln_b = 0.02 * jax.random.normal(ks[12], (L, D_MODEL), f32)
    return {"x": x, "w_in": w_in, "w_gate_up": w_gate_up, "b_gate": b_gate, "gn_w": gn_w,
            "pool_w": pool_w, "pool_b": pool_b, "pool_scale": pool_scale,
            "w_a": w_a, "w_b": w_b, "w_o": w_o, "ln_w": ln_w, "ln_b": ln_b}


def reference(x, w_in, w_gate_up, b_gate, gn_w, pool_w, pool_b, pool_scale,
              w_a, w_b, w_o, ln_w, ln_b):
    for l in range(DEPTH):
        x = hybrid_layer(x, w_in[l], w_gate_up[l], b_gate[l], gn_w[l], pool_w[l], pool_b[l],
                         pool_scale[l], w_a[l], w_b[l], w_o[l], ln_w[l], ln_b[l])
    return x
```

```python
import functools

import numpy as np
import jax
import jax.numpy as jnp
from jax import lax
from jax.experimental import pallas as pl
from jax.experimental.pallas import tpu as pltpu

GLA_HEADS = 4
GLA_GATE_RANK = 16
GLA_GATE_TAU = 16.0
POOL_WINDOWS = (2, 4, 8, 16)
POOL_LOOKBACK = 16
NORM_EPS = 1e-5

SEQ_TILE = 256
GLA_CHUNK = 64
VMEM_LIMIT_BYTES = 56 * 1024 * 1024

F32 = jnp.float32
BF16 = jnp.bfloat16


def _dot(a, b):
    return jnp.dot(a, b, preferred_element_type=F32)


def _sigmoid(x):
    return 1.0 / (1.0 + jnp.exp(-x))


def _log_sigmoid(x):
    return jnp.minimum(x, 0.0) - jnp.log1p(jnp.exp(-jnp.abs(x)))


def _block_kernel(x_ref, wbig_ref, wal_ref, wgu_ref, bg_ref, gnw_ref, poolw_ref, poolb_ref,
                  pools_ref, wa_ref, wb_ref, wo_ref, lnw_ref, lnb_ref, tri_ref, band_ref,
                  out_ref,
                  qs_ref, k_ref, b_ref, v_ref, o_ref, state_ref, ubuf_ref,
                  *, d_model, dk, dv, alpha):
    ts = SEQ_TILE
    hdk = dk // GLA_HEADS
    hdv = dv // GLA_HEADS
    t = pl.program_id(1)

    @pl.when(t == 0)
    def _():
        state_ref[...] = jnp.zeros_like(state_ref)
        ubuf_ref[0:POOL_LOOKBACK, :] = jnp.zeros((POOL_LOOKBACK, ubuf_ref.shape[1]), BF16)

    xf = x_ref[...]
    xb = xf.astype(BF16)

    def proj(n):
        return _dot(xb, wbig_ref[:, n * d_model:(n + 1) * d_model])

    qk = proj(0)
    qs_ref[...] = qk[:, :dk] * (hdk ** -0.5)
    k_ref[...] = qk[:, dk:]
    v_ref[...] = proj(1).astype(BF16)

    a_low = _dot(xb, wal_ref[...])
    gate_pre = _dot(a_low.astype(BF16), wgu_ref[...].astype(BF16)) + bg_ref[...]
    log_a = _log_sigmoid(gate_pre) * (1.0 / GLA_GATE_TAU)
    la_hi = log_a.astype(BF16)
    la_lo = (log_a - la_hi.astype(F32)).astype(BF16)
    b_ref[...] = _dot(tri_ref[...], la_hi) + _dot(tri_ref[...], la_lo)

    row_c = lax.broadcasted_iota(jnp.int32, (GLA_CHUNK, 1), 0)
    lane_c = lax.broadcasted_iota(jnp.int32, (GLA_CHUNK, GLA_CHUNK), 1)

    for c in range(ts // GLA_CHUNK):
        r0 = c * GLA_CHUNK
        rows = pl.ds(r0, GLA_CHUNK)
        qs_c = qs_ref[rows, :]
        b_c = b_ref[rows, :]

        def col_body(j, a_heads):
            kj = k_ref[pl.ds(r0 + j, 1), :]
            bj = b_ref[pl.ds(r0 + j, 1), :]
            p = qs_c * kj * jnp.exp(jnp.minimum(b_c - bj, 0.0))
            p = jnp.where(row_c >= j, p, 0.0)
            new = []
            for h in range(GLA_HEADS):
                s = jnp.sum(p[:, h * hdk:(h + 1) * hdk], axis=-1, keepdims=True)
                new.append(jnp.where(lane_c == j, s, a_heads[h]))
            return tuple(new)

        a_heads = lax.fori_loop(
            0, GLA_CHUNK, col_body,
            tuple(jnp.zeros((GLA_CHUNK, GLA_CHUNK), F32) for _ in range(GLA_HEADS)))

        b_last = b_ref[pl.ds(r0 + GLA_CHUNK - 1, 1), :]
        qd = (qs_c * jnp.exp(b_c)).astype(BF16)
        kd = k_ref[rows, :] * jnp.exp(b_last - b_c)
        e_last = jnp.exp(b_last)
        for h in range(GLA_HEADS):
            ks = slice(h * hdk, (h + 1) * hdk)
            vs = slice(h * hdv, (h + 1) * hdv)
            v_ch = v_ref[rows, vs]
            st = state_ref[h]
            o_ref[rows, vs] = (_dot(qd[:, ks], st.astype(BF16))
                               + _dot(a_heads[h].astype(BF16), v_ch))
            kd_t = jnp.transpose(kd[:, ks]).astype(BF16)
            e_col = jnp.transpose(jnp.broadcast_to(e_last[:, ks], (8, hdk)))[:, 0:1]
            state_ref[h] = st * e_col + _dot(kd_t, v_ch)

    g = proj(2)
    o = o_ref[...]
    y_a_parts = []
    for h in range(GLA_HEADS):
        vs = slice(h * hdv, (h + 1) * hdv)
        oh = o[:, vs]
        y_a_parts.append(oh * lax.rsqrt(jnp.mean(oh * oh, axis=-1, keepdims=True) + NORM_EPS))
    y_a = jnp.concatenate(y_a_parts, axis=-1) * gnw_ref[...] * (g * _sigmoid(g))

    u = proj(3)
    ubuf_ref[POOL_LOOKBACK:POOL_LOOKBACK + ts, :] = u.astype(BF16)
    t_abs = t * ts + lax.broadcasted_iota(jnp.int32, (ts, 1), 0)
    gdim = d_model // len(POOL_WINDOWS)
    p_parts = []
    for gi, w in enumerate(POOL_WINDOWS):
        gs = slice(gi * gdim, (gi + 1) * gdim)
        win_sum = _dot(band_ref[gi], ubuf_ref[:, gs])
        count = jnp.minimum(t_abs + 1, w).astype(F32)
        pg = win_sum / count - u[:, gs]
        p_parts.append(_dot(pg.astype(BF16), poolw_ref[gi]))
    ubuf_ref[0:POOL_LOOKBACK, :] = ubuf_ref[ts:ts + POOL_LOOKBACK, :]
    z = proj(4)
    y_b = (jnp.concatenate(p_parts, axis=-1) + poolb_ref[...]) * pools_ref[...] * (z * _sigmoid(z))

    merged = (_sigmoid(proj(5)) * _dot(y_a.astype(BF16), wa_ref[...])
              + _sigmoid(proj(6)) * _dot(y_b.astype(BF16), wb_ref[...]))
    r = alpha * xf + _dot(merged.astype(BF16), wo_ref[...])
    mu = jnp.mean(r, axis=-1, keepdims=True)
    rc = r - mu
    var = jnp.mean(rc * rc, axis=-1, keepdims=True)
    out_ref[...] = (rc * lax.rsqrt(var + NORM_EPS) * lnw_ref[...] + lnb_ref[...]).astype(out_ref.dtype)


def _const_spec(shape):
    nd = len(shape)
    return pl.BlockSpec(shape, lambda b, t: (0,) * nd, pipeline_mode=pl.Buffered(1))


def _tri_matrix():
    r = np.arange(SEQ_TILE)
    same_chunk = (r[:, None] // GLA_CHUNK) == (r[None, :] // GLA_CHUNK)
    return jnp.asarray((same_chunk & (r[None, :] <= r[:, None])).astype(np.float32), dtype=BF16)


def _band_matrices():
    r = np.arange(SEQ_TILE)[:, None] + POOL_LOOKBACK
    c = np.arange(SEQ_TILE + POOL_LOOKBACK)[None, :]
    bands = [((r - c >= 0) & (r - c < w)).astype(np.float32) for w in POOL_WINDOWS]
    return jnp.asarray(np.stack(bands), dtype=BF16)


def _layer(x, w_in, w_gate_up, b_gate, gn_w, pool_w, pool_b, pool_scale, w_a, w_b, w_o, ln_w, ln_b,
           *, alpha):
    bsz, seq, d_model = x.shape
    rank, dk = w_gate_up.shape
    dv = gn_w.shape[0]
    assert seq % SEQ_TILE == 0 and SEQ_TILE % GLA_CHUNK == 0
    assert dv == d_model and 2 * dk == d_model and rank == GLA_GATE_RANK
    assert pool_w.shape[0] == len(POOL_WINDOWS)
    n_qkvg = 2 * dk + 2 * dv
    w_big = jnp.concatenate([w_in[:, :n_qkvg], w_in[:, n_qkvg + rank:]], axis=1).astype(BF16)
    w_al = w_in[:, n_qkvg:n_qkvg + rank].astype(BF16)
    row = lambda a: a.reshape(1, -1)
    operands = (x, w_big, w_al, w_gate_up, row(b_gate), row(gn_w), pool_w.astype(BF16), row(pool_b),
                row(pool_scale), w_a.astype(BF16), w_b.astype(BF16), w_o.astype(BF16), row(ln_w),
                row(ln_b), _tri_matrix(), _band_matrices())
    in_specs = [pl.BlockSpec((None, SEQ_TILE, d_model), lambda b, t: (b, t, 0))]
    in_specs += [_const_spec(a.shape) for a in operands[1:]]
    kernel = functools.partial(_block_kernel, d_model=d_model, dk=dk, dv=dv, alpha=alpha)
    return pl.pallas_call(
        kernel,
        grid=(bsz, seq // SEQ_TILE),
        in_specs=in_specs,
        out_specs=pl.BlockSpec((None, SEQ_TILE, d_model), lambda b, t: (b, t, 0)),
        out_shape=jax.ShapeDtypeStruct(x.shape, x.dtype),
        scratch_shapes=[
            pltpu.VMEM((SEQ_TILE, dk), F32),
            pltpu.VMEM((SEQ_TILE, dk), F32),
            pltpu.VMEM((SEQ_TILE, dk), F32),
            pltpu.VMEM((SEQ_TILE, dv), BF16),
            pltpu.VMEM((SEQ_TILE, dv), F32),
            pltpu.VMEM((GLA_HEADS, dk // GLA_HEADS, dv // GLA_HEADS), F32),
            pltpu.VMEM((SEQ_TILE + POOL_LOOKBACK, d_model), BF16),
        ],
        compiler_params=pltpu.CompilerParams(
            dimension_semantics=("arbitrary", "arbitrary"),
            vmem_limit_bytes=VMEM_LIMIT_BYTES),
        name="hybrid_gla_pool_layer",
    )(*operands)


def kernel(x, w_in, w_gate_up, b_gate, gn_w, pool_w, pool_b, pool_scale, w_a, w_b, w_o, ln_w, ln_b):
    depth = w_in.shape[0]
    alpha = (2.0 * depth) ** 0.25
    for l in range(depth):
        x = _layer(x, w_in[l], w_gate_up[l], b_gate[l], gn_w[l], pool_w[l], pool_b[l], pool_scale[l],
                   w_a[l], w_b[l], w_o[l], ln_w[l], ln_b[l], alpha=alpha)
    return x
```

```python
import functools

import numpy as np
import jax
import jax.numpy as jnp
from jax import lax
from jax.experimental import pallas as pl
from jax.experimental.pallas import tpu as pltpu

GLA_HEADS = 4
GLA_GATE_RANK = 16
GLA_GATE_TAU = 16.0
POOL_WINDOWS = (2, 4, 8, 16)
POOL_LOOKBACK = 16
NORM_EPS = 1e-5

SEQ_TILE = 256
GLA_CHUNK = 64
VMEM_LIMIT_BYTES = 56 * 1024 * 1024
GLA_FACTOR_MAX_DECAY = 50.0

F32 = jnp.float32
BF16 = jnp.bfloat16


def _dot(a, b):
    return jnp.dot(a, b, preferred_element_type=F32)


def _sigmoid(x):
    return 1.0 / (1.0 + jnp.exp(-x))


def _log_sigmoid(x):
    return jnp.minimum(x, 0.0) - jnp.log1p(jnp.exp(-jnp.abs(x)))


def _row_to_col(row):
    return jnp.transpose(jnp.broadcast_to(row, (8, row.shape[1])))[:, 0:1]


def _gla_tile_factorised(qs_ref, k_ref, b_ref, v_ref, o_ref, state_ref, *, hdk, hdv):
    ts = SEQ_TILE
    b = b_ref[...]
    e_pos = jnp.exp(b)
    e_neg = jnp.exp(-b)
    e_last = e_pos[ts - 1:ts, :]
    k_neg = k_ref[...] * e_neg
    qd = (qs_ref[...] * e_pos).astype(BF16)
    kn = k_neg.astype(BF16)
    kd = k_neg * e_last
    causal = (lax.broadcasted_iota(jnp.int32, (ts, ts), 0)
              >= lax.broadcasted_iota(jnp.int32, (ts, ts), 1))
    for h in range(GLA_HEADS):
        ks = slice(h * hdk, (h + 1) * hdk)
        vs = slice(h * hdv, (h + 1) * hdv)
        v_h = v_ref[:, vs]
        st = state_ref[h]
        scores = lax.dot_general(qd[:, ks], kn[:, ks], (((1,), (1,)), ((), ())),
                                 preferred_element_type=F32)
        a = jnp.where(causal, scores, 0.0).astype(BF16)
        o_ref[:, vs] = _dot(qd[:, ks], st.astype(BF16)) + _dot(a, v_h)
        kd_t = jnp.transpose(kd[:, ks]).astype(BF16)
        state_ref[h] = st * _row_to_col(e_last[:, ks]) + _dot(kd_t, v_h)


def _gla_tile_direct(qs_ref, k_ref, b_ref, v_ref, o_ref, state_ref, *, hdk, hdv):
    row_c = lax.broadcasted_iota(jnp.int32, (GLA_CHUNK, 1), 0)
    lane_c = lax.broadcasted_iota(jnp.int32, (GLA_CHUNK, GLA_CHUNK), 1)

    for c in range(SEQ_TILE // GLA_CHUNK):
        r0 = c * GLA_CHUNK
        rows = pl.ds(r0, GLA_CHUNK)
        qs_c = qs_ref[rows, :]
        b_c = b_ref[rows, :]

        def col_body(j, a_heads):
            kj = k_ref[pl.ds(r0 + j, 1), :]
            bj = b_ref[pl.ds(r0 + j, 1), :]
            p = qs_c * kj * jnp.exp(jnp.minimum(b_c - bj, 0.0))
            p = jnp.where(row_c >= j, p, 0.0)
            new = []
            for h in range(GLA_HEADS):
                s = jnp.sum(p[:, h * hdk:(h + 1) * hdk], axis=-1, keepdims=True)
                new.append(jnp.where(lane_c == j, s, a_heads[h]))
            return tuple(new)

        a_heads = lax.fori_loop(
            0, GLA_CHUNK, col_body,
            tuple(jnp.zeros((GLA_CHUNK, GLA_CHUNK), F32) for _ in range(GLA_HEADS)))

        b_last = b_ref[pl.ds(r0 + GLA_CHUNK - 1, 1), :]
        qd = (qs_c * jnp.exp(b_c)).astype(BF16)
        kd = k_ref[rows, :] * jnp.exp(b_last - b_c)
        e_last = jnp.exp(b_last)
        for h in range(GLA_HEADS):
            ks = slice(h * hdk, (h + 1) * hdk)
            vs = slice(h * hdv, (h + 1) * hdv)
            v_ch = v_ref[rows, vs]
            st = state_ref[h]
            o_ref[rows, vs] = (_dot(qd[:, ks], st.astype(BF16))
                               + _dot(a_heads[h].astype(BF16), v_ch))
            kd_t = jnp.transpose(kd[:, ks]).astype(BF16)
            state_ref[h] = st * _row_to_col(e_last[:, ks]) + _dot(kd_t, v_ch)


def _block_kernel(x_ref, wbig_ref, wal_ref, wgu_ref, bg_ref, gnw_ref, poolw_ref, poolb_ref,
                  pools_ref, wa_ref, wb_ref, wo_ref, lnw_ref, lnb_ref, tri_ref, band_ref,
                  out_ref,
                  qs_ref, k_ref, b_ref, v_ref, o_ref, state_ref, ubuf_ref,
                  *, d_model, dk, dv, alpha):
    ts = SEQ_TILE
    hdk = dk // GLA_HEADS
    hdv = dv // GLA_HEADS
    t = pl.program_id(1)

    @pl.when(t == 0)
    def _():
        state_ref[...] = jnp.zeros_like(state_ref)
        ubuf_ref[0:POOL_LOOKBACK, :] = jnp.zeros((POOL_LOOKBACK, ubuf_ref.shape[1]), BF16)

    xf = x_ref[...]
    xb = xf.astype(BF16)

    def proj(n):
        return _dot(xb, wbig_ref[:, n * d_model:(n + 1) * d_model])

    qk = proj(0)
    qs_ref[...] = qk[:, :dk] * (hdk ** -0.5)
    k_ref[...] = qk[:, dk:]
    v_ref[...] = proj(1).astype(BF16)

    a_low = _dot(xb, wal_ref[...])
    gate_pre = _dot(a_low.astype(BF16), wgu_ref[...].astype(BF16)) + bg_ref[...]
    log_a = _log_sigmoid(gate_pre) * (1.0 / GLA_GATE_TAU)
    la_hi = log_a.astype(BF16)
    la_lo = (log_a - la_hi.astype(F32)).astype(BF16)
    b_ref[...] = _dot(tri_ref[...], la_hi) + _dot(tri_ref[...], la_lo)
    factorisable = jnp.min(b_ref[ts - 1:ts, :]) >= -GLA_FACTOR_MAX_DECAY

    @pl.when(factorisable)
    def _():
        _gla_tile_factorised(qs_ref, k_ref, b_ref, v_ref, o_ref, state_ref, hdk=hdk, hdv=hdv)

    @pl.when(jnp.logical_not(factorisable))
    def _():
        row = lax.broadcasted_iota(jnp.int32, (ts, ts), 0)
        col = lax.broadcasted_iota(jnp.int32, (ts, ts), 1)
        tri_chunk = jnp.where(row // GLA_CHUNK == col // GLA_CHUNK, tri_ref[...], jnp.zeros((), BF16))
        b_ref[...] = _dot(tri_chunk, la_hi) + _dot(tri_chunk, la_lo)
        _gla_tile_direct(qs_ref, k_ref, b_ref, v_ref, o_ref, state_ref, hdk=hdk, hdv=hdv)

    g = proj(2)
    o = o_ref[...]
    y_a_parts = []
    for h in range(GLA_HEADS):
        vs = slice(h * hdv, (h + 1) * hdv)
        oh = o[:, vs]
        y_a_parts.append(oh * lax.rsqrt(jnp.mean(oh * oh, axis=-1, keepdims=True) + NORM_EPS))
    y_a = jnp.concatenate(y_a_parts, axis=-1) * gnw_ref[...] * (g * _sigmoid(g))

    u = proj(3)
    ubuf_ref[POOL_LOOKBACK:POOL_LOOKBACK + ts, :] = u.astype(BF16)
    t_abs = t * ts + lax.broadcasted_iota(jnp.int32, (ts, 1), 0)
    gdim = d_model // len(POOL_WINDOWS)
    p_parts = []
    for gi, w in enumerate(POOL_WINDOWS):
        gs = slice(gi * gdim, (gi + 1) * gdim)
        win_sum = _dot(band_ref[gi], ubuf_ref[:, gs])
        count = jnp.minimum(t_abs + 1, w).astype(F32)
        pg = win_sum / count - u[:, gs]
        p_parts.append(_dot(pg.astype(BF16), poolw_ref[gi]))
    ubuf_ref[0:POOL_LOOKBACK, :] = ubuf_ref[ts:ts + POOL_LOOKBACK, :]
    z = proj(4)
    y_b = (jnp.concatenate(p_parts, axis=-1) + poolb_ref[...]) * pools_ref[...] * (z * _sigmoid(z))

    merged = (_sigmoid(proj(5)) * _dot(y_a.astype(BF16), wa_ref[...])
              + _sigmoid(proj(6)) * _dot(y_b.astype(BF16), wb_ref[...]))
    r = alpha * xf + _dot(merged.astype(BF16), wo_ref[...])
    mu = jnp.mean(r, axis=-1, keepdims=True)
    rc = r - mu
    var = jnp.mean(rc * rc, axis=-1, keepdims=True)
    out_ref[...] = (rc * lax.rsqrt(var + NORM_EPS) * lnw_ref[...] + lnb_ref[...]).astype(out_ref.dtype)


def _const_spec(shape):
    nd = len(shape)
    return pl.BlockSpec(shape, lambda b, t: (0,) * nd, pipeline_mode=pl.Buffered(1))


def _tri_matrix():
    r = np.arange(SEQ_TILE)
    return jnp.asarray((r[None, :] <= r[:, None]).astype(np.float32), dtype=BF16)


def _band_matrices():
    r = np.arange(SEQ_TILE)[:, None] + POOL_LOOKBACK
    c = np.arange(SEQ_TILE + POOL_LOOKBACK)[None, :]
    bands = [((r - c >= 0) & (r - c < w)).astype(np.float32) for w in POOL_WINDOWS]
    return jnp.asarray(np.stack(bands), dtype=BF16)


def _layer(x, w_in, w_gate_up, b_gate, gn_w, pool_w, pool_b, pool_scale, w_a, w_b, w_o, ln_w, ln_b,
           *, alpha):
    bsz, seq, d_model = x.shape
    rank, dk = w_gate_up.shape
    dv = gn_w.shape[0]
    assert seq % SEQ_TILE == 0 and SEQ_TILE % GLA_CHUNK == 0
    assert dv == d_model and 2 * dk == d_model and rank == GLA_GATE_RANK
    assert pool_w.shape[0] == len(POOL_WINDOWS)
    n_qkvg = 2 * dk + 2 * dv
    w_big = jnp.concatenate([w_in[:, :n_qkvg], w_in[:, n_qkvg + rank:]], axis=1).astype(BF16)
    w_al = w_in[:, n_qkvg:n_qkvg + rank].astype(BF16)
    row = lambda a: a.reshape(1, -1)
    operands = (x, w_big, w_al, w_gate_up, row(b_gate), row(gn_w), pool_w.astype(BF16), row(pool_b),
                row(pool_scale), w_a.astype(BF16), w_b.astype(BF16), w_o.astype(BF16), row(ln_w),
                row(ln_b), _tri_matrix(), _band_matrices())
    in_specs = [pl.BlockSpec((None, SEQ_TILE, d_model), lambda b, t: (b, t, 0))]
    in_specs += [_const_spec(a.shape) for a in operands[1:]]
    kernel = functools.partial(_block_kernel, d_model=d_model, dk=dk, dv=dv, alpha=alpha)
    return pl.pallas_call(
        kernel,
        grid=(bsz, seq // SEQ_TILE),
        in_specs=in_specs,
        out_specs=pl.BlockSpec((None, SEQ_TILE, d_model), lambda b, t: (b, t, 0)),
        out_shape=jax.ShapeDtypeStruct(x.shape, x.dtype),
        scratch_shapes=[
            pltpu.VMEM((SEQ_TILE, dk), F32),
            pltpu.VMEM((SEQ_TILE, dk), F32),
            pltpu.VMEM((SEQ_TILE, dk), F32),
            pltpu.VMEM((SEQ_TILE, dv), BF16),
            pltpu.VMEM((SEQ_TILE, dv), F32),
            pltpu.VMEM((GLA_HEADS, dk // GLA_HEADS, dv // GLA_HEADS), F32),
            pltpu.VMEM((SEQ_TILE + POOL_LOOKBACK, d_model), BF16),
        ],
        compiler_params=pltpu.CompilerParams(
            dimension_semantics=("arbitrary", "arbitrary"),
            vmem_limit_bytes=VMEM_LIMIT_BYTES),
        name="hybrid_gla_pool_layer",
    )(*operands)


def kernel(x, w_in, w_gate_up, b_gate, gn_w, pool_w, pool_b, pool_scale, w_a, w_b, w_o, ln_w, ln_b):
    depth = w_in.shape[0]
    alpha = (2.0 * depth) ** 0.25
    for l in range(depth):
        x = _layer(x, w_in[l], w_gate_up[l], b_gate[l], gn_w[l], pool_w[l], pool_b[l], pool_scale[l],
                   w_a[l], w_b[l], w_o[l], ln_w[l], ln_b[l], alpha=alpha)
    return x
```

```python
import functools

import numpy as np
import jax
import jax.numpy as jnp
from jax import lax
from jax.experimental import pallas as pl
from jax.experimental.pallas import tpu as pltpu

GLA_HEADS = 4
GLA_GATE_RANK = 16
GLA_GATE_TAU = 16.0
POOL_WINDOWS = (2, 4, 8, 16)
POOL_LOOKBACK = 16
NORM_EPS = 1e-5

SEQ_TILE = 256
GLA_CHUNK = 64
VMEM_LIMIT_BYTES = 56 * 1024 * 1024
GLA_FACTOR_MAX_DECAY = 50.0

F32 = jnp.float32
BF16 = jnp.bfloat16


def _dot(a, b):
    return jnp.dot(a, b, preferred_element_type=F32)


def _sigmoid(x):
    return 1.0 / (1.0 + jnp.exp(-x))


def _log_sigmoid(x):
    return jnp.minimum(x, 0.0) - jnp.log1p(jnp.exp(-jnp.abs(x)))


def _row_to_col(row):
    return jnp.transpose(jnp.broadcast_to(row, (8, row.shape[1])))[:, 0:1]


def _gla_tile_factorised(qs_ref, k_ref, b_ref, v_ref, o_ref, state_ref, new_state_ref, *, hdk, hdv):
    ts = SEQ_TILE
    b = b_ref[...]
    e_pos = jnp.exp(b)
    e_neg = jnp.exp(-b)
    e_last = e_pos[ts - 1:ts, :]
    k_neg = k_ref[...] * e_neg
    qd = (qs_ref[...] * e_pos).astype(BF16)
    kn = k_neg.astype(BF16)
    kd = k_neg * e_last
    causal = (lax.broadcasted_iota(jnp.int32, (ts, ts), 0)
              >= lax.broadcasted_iota(jnp.int32, (ts, ts), 1))
    for h in range(GLA_HEADS):
        ks = slice(h * hdk, (h + 1) * hdk)
        vs = slice(h * hdv, (h + 1) * hdv)
        v_h = v_ref[:, vs]
        st = state_ref[h]
        scores = lax.dot_general(qd[:, ks], kn[:, ks], (((1,), (1,)), ((), ())),
                                 preferred_element_type=F32)
        a = jnp.where(causal, scores, 0.0).astype(BF16)
        o_ref[:, vs] = _dot(qd[:, ks], st.astype(BF16)) + _dot(a, v_h)
        kd_t = jnp.transpose(kd[:, ks]).astype(BF16)
        new_state_ref[h] = st * _row_to_col(e_last[:, ks]) + _dot(kd_t, v_h)


def _gla_tile_direct(qs_ref, k_ref, b_ref, v_ref, o_ref, state_ref, new_state_ref, *, hdk, hdv):
    new_state_ref[...] = state_ref[...]
    row_c = lax.broadcasted_iota(jnp.int32, (GLA_CHUNK, 1), 0)
    lane_c = lax.broadcasted_iota(jnp.int32, (GLA_CHUNK, GLA_CHUNK), 1)

    for c in range(SEQ_TILE // GLA_CHUNK):
        r0 = c * GLA_CHUNK
        rows = pl.ds(r0, GLA_CHUNK)
        qs_c = qs_ref[rows, :]
        b_c = b_ref[rows, :]

        def col_body(j, a_heads):
            kj = k_ref[pl.ds(r0 + j, 1), :]
            bj = b_ref[pl.ds(r0 + j, 1), :]
            p = qs_c * kj * jnp.exp(jnp.minimum(b_c - bj, 0.0))
            p = jnp.where(row_c >= j, p, 0.0)
            new = []
            for h in range(GLA_HEADS):
                s = jnp.sum(p[:, h * hdk:(h + 1) * hdk], axis=-1, keepdims=True)
                new.append(jnp.where(lane_c == j, s, a_heads[h]))
            return tuple(new)

        a_heads = lax.fori_loop(
            0, GLA_CHUNK, col_body,
            tuple(jnp.zeros((GLA_CHUNK, GLA_CHUNK), F32) for _ in range(GLA_HEADS)))

        b_last = b_ref[pl.ds(r0 + GLA_CHUNK - 1, 1), :]
        qd = (qs_c * jnp.exp(b_c)).astype(BF16)
        kd = k_ref[rows, :] * jnp.exp(b_last - b_c)
        e_last = jnp.exp(b_last)
        for h in range(GLA_HEADS):
            ks = slice(h * hdk, (h + 1) * hdk)
            vs = slice(h * hdv, (h + 1) * hdv)
            v_ch = v_ref[rows, vs]
            st = new_state_ref[h]
            o_ref[rows, vs] = (_dot(qd[:, ks], st.astype(BF16))
                               + _dot(a_heads[h].astype(BF16), v_ch))
            kd_t = jnp.transpose(kd[:, ks]).astype(BF16)
            new_state_ref[h] = st * _row_to_col(e_last[:, ks]) + _dot(kd_t, v_ch)


def _block_kernel(x_ref, whead_ref, wrest_ref, wgu_ref, bg_ref, gnw_ref, poolw_ref, poolb_ref,
                  pools_ref, wa_ref, wb_ref, wo_ref, lnw_ref, lnb_ref, tri_ref, band_ref,
                  out_ref,
                  qs_ref, k_ref, b_ref, v_ref, o_ref, state_ref, new_state_ref, ubuf_ref,
                  ya_scale_ref, gate_a_ref, merged_b_ref,
                  *, d_model, dk, dv, alpha):
    ts = SEQ_TILE
    hdk = dk // GLA_HEADS
    hdv = dv // GLA_HEADS
    t = pl.program_id(1)

    @pl.when(t == 0)
    def _():
        state_ref[...] = jnp.zeros_like(state_ref)
        ubuf_ref[0:POOL_LOOKBACK, :] = jnp.zeros((POOL_LOOKBACK, ubuf_ref.shape[1]), BF16)

    xb = x_ref[...].astype(BF16)

    def proj(w_ref, n):
        return _dot(xb, w_ref[:, n * d_model:(n + 1) * d_model])

    a_low = _dot(xb, whead_ref[:, 3 * d_model:3 * d_model + GLA_GATE_RANK])
    gate_pre = _dot(a_low.astype(BF16), wgu_ref[...].astype(BF16)) + bg_ref[...]
    log_a = _log_sigmoid(gate_pre) * (1.0 / GLA_GATE_TAU)
    la_hi = log_a.astype(BF16)
    la_lo = (log_a - la_hi.astype(F32)).astype(BF16)
    b_ref[...] = _dot(tri_ref[...], la_hi) + _dot(tri_ref[...], la_lo)
    factorisable = jnp.min(b_ref[ts - 1:ts, :]) >= -GLA_FACTOR_MAX_DECAY

    qk = proj(whead_ref, 0)
    qs_ref[...] = qk[:, :dk] * (hdk ** -0.5)
    k_ref[...] = qk[:, dk:]
    v_ref[...] = proj(whead_ref, 1).astype(BF16)
    _gla_tile_factorised(qs_ref, k_ref, b_ref, v_ref, o_ref, state_ref, new_state_ref, hdk=hdk, hdv=hdv)

    g = proj(whead_ref, 2)
    ya_scale_ref[...] = gnw_ref[...] * (g * _sigmoid(g))
    gate_a_ref[...] = _sigmoid(proj(wrest_ref, 2))

    u = proj(wrest_ref, 0)
    ubuf_ref[POOL_LOOKBACK:POOL_LOOKBACK + ts, :] = u.astype(BF16)
    t_abs = t * ts + lax.broadcasted_iota(jnp.int32, (ts, 1), 0)
    gdim = d_model // len(POOL_WINDOWS)
    p_parts = []
    for gi, w in enumerate(POOL_WINDOWS):
        gs = slice(gi * gdim, (gi + 1) * gdim)
        win_sum = _dot(band_ref[gi], ubuf_ref[:, gs])
        count = jnp.minimum(t_abs + 1, w).astype(F32)
        pg = win_sum / count - u[:, gs]
        p_parts.append(_dot(pg.astype(BF16), poolw_ref[gi]))
    ubuf_ref[0:POOL_LOOKBACK, :] = ubuf_ref[ts:ts + POOL_LOOKBACK, :]
    z = proj(wrest_ref, 1)
    y_b = (jnp.concatenate(p_parts, axis=-1) + poolb_ref[...]) * pools_ref[...] * (z * _sigmoid(z))
    merged_b_ref[...] = _sigmoid(proj(wrest_ref, 3)) * _dot(y_b.astype(BF16), wb_ref[...])

    @pl.when(jnp.logical_not(factorisable))
    def _():
        row = lax.broadcasted_iota(jnp.int32, (ts, ts), 0)
        col = lax.broadcasted_iota(jnp.int32, (ts, ts), 1)
        tri_chunk = jnp.where(row // GLA_CHUNK == col // GLA_CHUNK, tri_ref[...], jnp.zeros((), BF16))
        b_ref[...] = _dot(tri_chunk, la_hi) + _dot(tri_chunk, la_lo)
        _gla_tile_direct(qs_ref, k_ref, b_ref, v_ref, o_ref, state_ref, new_state_ref, hdk=hdk, hdv=hdv)

    state_ref[...] = new_state_ref[...]

    o = o_ref[...]
    y_a_parts = []
    for h in range(GLA_HEADS):
        vs = slice(h * hdv, (h + 1) * hdv)
        oh = o[:, vs]
        y_a_parts.append(oh * lax.rsqrt(jnp.mean(oh * oh, axis=-1, keepdims=True) + NORM_EPS))
    y_a = jnp.concatenate(y_a_parts, axis=-1) * ya_scale_ref[...]
    merged = gate_a_ref[...] * _dot(y_a.astype(BF16), wa_ref[...]) + merged_b_ref[...]
    r = alpha * x_ref[...] + _dot(merged.astype(BF16), wo_ref[...])
    mu = jnp.mean(r, axis=-1, keepdims=True)
    rc = r - mu
    var = jnp.mean(rc * rc, axis=-1, keepdims=True)
    out_ref[...] = (rc * lax.rsqrt(var + NORM_EPS) * lnw_ref[...] + lnb_ref[...]).astype(out_ref.dtype)


def _const_spec(shape):
    nd = len(shape)
    return pl.BlockSpec(shape, lambda b, t: (0,) * nd, pipeline_mode=pl.Buffered(1))


def _tri_matrix():
    r = np.arange(SEQ_TILE)
    return jnp.asarray((r[None, :] <= r[:, None]).astype(np.float32), dtype=BF16)


def _band_matrices():
    r = np.arange(SEQ_TILE)[:, None] + POOL_LOOKBACK
    c = np.arange(SEQ_TILE + POOL_LOOKBACK)[None, :]
    bands = [((r - c >= 0) & (r - c < w)).astype(np.float32) for w in POOL_WINDOWS]
    return jnp.asarray(np.stack(bands), dtype=BF16)


def _layer(x, w_in, w_gate_up, b_gate, gn_w, pool_w, pool_b, pool_scale, w_a, w_b, w_o, ln_w, ln_b,
           *, alpha):
    bsz, seq, d_model = x.shape
    rank, dk = w_gate_up.shape
    dv = gn_w.shape[0]
    assert seq % SEQ_TILE == 0 and SEQ_TILE % GLA_CHUNK == 0
    assert dv == d_model and 2 * dk == d_model and rank == GLA_GATE_RANK
    assert pool_w.shape[0] == len(POOL_WINDOWS)
    n_head = 2 * dk + 2 * dv + rank
    w_head = w_in[:, :n_head].astype(BF16)
    w_rest = w_in[:, n_head:].astype(BF16)
    row = lambda a: a.reshape(1, -1)
    operands = (x, w_head, w_rest, w_gate_up, row(b_gate), row(gn_w), pool_w.astype(BF16), row(pool_b),
                row(pool_scale), w_a.astype(BF16), w_b.astype(BF16), w_o.astype(BF16), row(ln_w),
                row(ln_b), _tri_matrix(), _band_matrices())
    in_specs = [pl.BlockSpec((None, SEQ_TILE, d_model), lambda b, t: (b, t, 0))]
    in_specs += [_const_spec(a.shape) for a in operands[1:]]
    kernel = functools.partial(_block_kernel, d_model=d_model, dk=dk, dv=dv, alpha=alpha)
    state_shape = (GLA_HEADS, dk // GLA_HEADS, dv // GLA_HEADS)
    return pl.pallas_call(
        kernel,
        grid=(bsz, seq // SEQ_TILE),
        in_specs=in_specs,
        out_specs=pl.BlockSpec((None, SEQ_TILE, d_model), lambda b, t: (b, t, 0)),
        out_shape=jax.ShapeDtypeStruct(x.shape, x.dtype),
        scratch_shapes=[
            pltpu.VMEM((SEQ_TILE, dk), F32),
            pltpu.VMEM((SEQ_TILE, dk), F32),
            pltpu.VMEM((SEQ_TILE, dk), F32),
            pltpu.VMEM((SEQ_TILE, dv), BF16),
            pltpu.VMEM((SEQ_TILE, dv), F32),
            pltpu.VMEM(state_shape, F32),
            pltpu.VMEM(state_shape, F32),
            pltpu.VMEM((SEQ_TILE + POOL_LOOKBACK, d_model), BF16),
            pltpu.VMEM((SEQ_TILE, dv), F32),
            pltpu.VMEM((SEQ_TILE, d_model), F32),
            pltpu.VMEM((SEQ_TILE, d_model), F32),
        ],
        compiler_params=pltpu.CompilerParams(
            dimension_semantics=("arbitrary", "arbitrary"),
            vmem_limit_bytes=VMEM_LIMIT_BYTES),
        name="hybrid_gla_pool_layer",
    )(*operands)


def kernel(x, w_in, w_gate_up, b_gate, gn_w, pool_w, pool_b, pool_scale, w_a, w_b, w_o, ln_w, ln_b):
    depth = w_in.shape[0]
    alpha = (2.0 * depth) ** 0.25
    for l in range(depth):
        x = _layer(x, w_in[l], w_gate_up[l], b_gate[l], gn_w[l], pool_w[l], pool_b[l], pool_scale[l],
                   w_a[l], w_b[l], w_o[l], ln_w[l], ln_b[l], alpha=alpha)
    return x
```

```python
import functools

import numpy as np
import jax
import jax.numpy as jnp
from jax import lax
from jax.experimental import pallas as pl
from jax.experimental.pallas import tpu as pltpu

GLA_HEADS = 4
GLA_GATE_RANK = 16
GLA_GATE_TAU = 16.0
POOL_WINDOWS = (2, 4, 8, 16)
POOL_LOOKBACK = 16
NORM_EPS = 1e-5

SEQ_TILE = 256
GLA_CHUNK = 64
VMEM_LIMIT_BYTES = 56 * 1024 * 1024
GLA_FACTOR_MAX_DECAY = 50.0

F32 = jnp.float32
BF16 = jnp.bfloat16


def _dot(a, b):
    return jnp.dot(a, b, preferred_element_type=F32)


def _sigmoid(x):
    return 1.0 / (1.0 + jnp.exp(-x))


def _log_sigmoid(x):
    return jnp.minimum(x, 0.0) - jnp.log1p(jnp.exp(-jnp.abs(x)))


def _row_to_col(row):
    return jnp.transpose(jnp.broadcast_to(row, (8, row.shape[1])))[:, 0:1]


def _gla_factor_operands(qs_ref, k_ref, b_ref):
    b = b_ref[...]
    e_pos = jnp.exp(b)
    e_last = e_pos[SEQ_TILE - 1:SEQ_TILE, :]
    k_neg = k_ref[...] * jnp.exp(-b)
    qd = (qs_ref[...] * e_pos).astype(BF16)
    kn = k_neg.astype(BF16)
    kd = k_neg * e_last
    return qd, kn, kd, e_last


def _gla_tile_factorised(operands, v_ref, o_ref, state_ref, new_state_ref, *, hdk, hdv):
    ts = SEQ_TILE
    qd, kn, kd, e_last = operands
    causal = (lax.broadcasted_iota(jnp.int32, (ts, ts), 0)
              >= lax.broadcasted_iota(jnp.int32, (ts, ts), 1))
    for h in range(GLA_HEADS):
        ks = slice(h * hdk, (h + 1) * hdk)
        vs = slice(h * hdv, (h + 1) * hdv)
        v_h = v_ref[:, vs]
        st = state_ref[h]
        scores = lax.dot_general(qd[:, ks], kn[:, ks], (((1,), (1,)), ((), ())),
                                 preferred_element_type=F32)
        a = jnp.where(causal, scores, 0.0).astype(BF16)
        o_ref[:, vs] = _dot(qd[:, ks], st.astype(BF16)) + _dot(a, v_h)
        kd_t = jnp.transpose(kd[:, ks]).astype(BF16)
        new_state_ref[h] = st * _row_to_col(e_last[:, ks]) + _dot(kd_t, v_h)


def _gla_tile_direct(qs_ref, k_ref, b_ref, v_ref, o_ref, state_ref, new_state_ref, *, hdk, hdv):
    new_state_ref[...] = state_ref[...]
    row_c = lax.broadcasted_iota(jnp.int32, (GLA_CHUNK, 1), 0)
    lane_c = lax.broadcasted_iota(jnp.int32, (GLA_CHUNK, GLA_CHUNK), 1)

    for c in range(SEQ_TILE // GLA_CHUNK):
        r0 = c * GLA_CHUNK
        rows = pl.ds(r0, GLA_CHUNK)
        qs_c = qs_ref[rows, :]
        b_c = b_ref[rows, :]

        def col_body(j, a_heads):
            kj = k_ref[pl.ds(r0 + j, 1), :]
            bj = b_ref[pl.ds(r0 + j, 1), :]
            p = qs_c * kj * jnp.exp(jnp.minimum(b_c - bj, 0.0))
            p = jnp.where(row_c >= j, p, 0.0)
            new = []
            for h in range(GLA_HEADS):
                s = jnp.sum(p[:, h * hdk:(h + 1) * hdk], axis=-1, keepdims=True)
                new.append(jnp.where(lane_c == j, s, a_heads[h]))
            return tuple(new)

        a_heads = lax.fori_loop(
            0, GLA_CHUNK, col_body,
            tuple(jnp.zeros((GLA_CHUNK, GLA_CHUNK), F32) for _ in range(GLA_HEADS)))

        b_last = b_ref[pl.ds(r0 + GLA_CHUNK - 1, 1), :]
        qd = (qs_c * jnp.exp(b_c)).astype(BF16)
        kd = k_ref[rows, :] * jnp.exp(b_last - b_c)
        e_last = jnp.exp(b_last)
        for h in range(GLA_HEADS):
            ks = slice(h * hdk, (h + 1) * hdk)
            vs = slice(h * hdv, (h + 1) * hdv)
            v_ch = v_ref[rows, vs]
            st = new_state_ref[h]
            o_ref[rows, vs] = (_dot(qd[:, ks], st.astype(BF16))
                               + _dot(a_heads[h].astype(BF16), v_ch))
            kd_t = jnp.transpose(kd[:, ks]).astype(BF16)
            new_state_ref[h] = st * _row_to_col(e_last[:, ks]) + _dot(kd_t, v_ch)


def _block_kernel(x_ref, whead_ref, wrest_ref, wgu_ref, bg_ref, gnw_ref, poolw_ref, poolb_ref,
                  pools_ref, wa_ref, wb_ref, wo_ref, lnw_ref, lnb_ref, tri_ref, band_ref,
                  out_ref,
                  qs_ref, k_ref, b_ref, v_ref, o_ref, state_ref, new_state_ref, ubuf_ref,
                  ya_scale_ref, gate_a_ref, merged_b_ref, resid_ref,
                  *, d_model, dk, dv, alpha, tiles_per_seq, n_tiles):
    ts = SEQ_TILE
    hdk = dk // GLA_HEADS
    hdv = dv // GLA_HEADS
    s = pl.program_id(0)
    t = lax.rem(jnp.minimum(s, n_tiles - 1), tiles_per_seq)

    @pl.when(s == 0)
    def _():
        for ref in (o_ref, ya_scale_ref, gate_a_ref, merged_b_ref, resid_ref):
            ref[...] = jnp.zeros_like(ref)

    @pl.when(t == 0)
    def _():
        state_ref[...] = jnp.zeros_like(state_ref)
        ubuf_ref[0:POOL_LOOKBACK, :] = jnp.zeros((POOL_LOOKBACK, ubuf_ref.shape[1]), BF16)

    xf = x_ref[...]
    xb = xf.astype(BF16)

    def proj(w_ref, n):
        return _dot(xb, w_ref[:, n * d_model:(n + 1) * d_model])

    o = o_ref[...]
    y_a_parts = []
    for h in range(GLA_HEADS):
        oh = o[:, h * hdv:(h + 1) * hdv]
        y_a_parts.append(oh * lax.rsqrt(jnp.mean(oh * oh, axis=-1, keepdims=True) + NORM_EPS))
    y_a = (jnp.concatenate(y_a_parts, axis=-1) * ya_scale_ref[...]).astype(BF16)

    a_low = _dot(xb, whead_ref[:, 3 * d_model:3 * d_model + GLA_GATE_RANK])
    qk = proj(whead_ref, 0)
    qs_ref[...] = qk[:, :dk] * (hdk ** -0.5)
    k_ref[...] = qk[:, dk:]

    gate_pre = _dot(a_low.astype(BF16), wgu_ref[...].astype(BF16)) + bg_ref[...]
    log_a = _log_sigmoid(gate_pre) * (1.0 / GLA_GATE_TAU)
    la_hi = log_a.astype(BF16)
    la_lo = (log_a - la_hi.astype(F32)).astype(BF16)

    merged = (gate_a_ref[...] * _dot(y_a, wa_ref[...]) + merged_b_ref[...]).astype(BF16)

    b_ref[...] = _dot(tri_ref[...], la_hi) + _dot(tri_ref[...], la_lo)
    factorisable = jnp.min(b_ref[ts - 1:ts, :]) >= -GLA_FACTOR_MAX_DECAY
    v_ref[...] = proj(whead_ref, 1).astype(BF16)

    gla_operands = _gla_factor_operands(qs_ref, k_ref, b_ref)

    r = resid_ref[...] + _dot(merged, wo_ref[...])

    _gla_tile_factorised(gla_operands, v_ref, o_ref, state_ref, new_state_ref, hdk=hdk, hdv=hdv)

    mu = jnp.mean(r, axis=-1, keepdims=True)
    rc = r - mu
    var = jnp.mean(rc * rc, axis=-1, keepdims=True)
    out_ref[...] = (rc * lax.rsqrt(var + NORM_EPS) * lnw_ref[...] + lnb_ref[...]).astype(out_ref.dtype)

    g = proj(whead_ref, 2)
    ya_scale_ref[...] = gnw_ref[...] * (g * _sigmoid(g))
    gate_a_ref[...] = _sigmoid(proj(wrest_ref, 2))

    u = proj(wrest_ref, 0)
    ubuf_ref[POOL_LOOKBACK:POOL_LOOKBACK + ts, :] = u.astype(BF16)
    t_abs = t * ts + lax.broadcasted_iota(jnp.int32, (ts, 1), 0)
    gdim = d_model // len(POOL_WINDOWS)
    p_parts = []
    for gi, w in enumerate(POOL_WINDOWS):
        gs = slice(gi * gdim, (gi + 1) * gdim)
        win_sum = _dot(band_ref[gi], ubuf_ref[:, gs])
        count = jnp.minimum(t_abs + 1, w).astype(F32)
        pg = win_sum / count - u[:, gs]
        p_parts.append(_dot(pg.astype(BF16), poolw_ref[gi]))
    ubuf_ref[0:POOL_LOOKBACK, :] = ubuf_ref[ts:ts + POOL_LOOKBACK, :]
    z = proj(wrest_ref, 1)
    y_b = (jnp.concatenate(p_parts, axis=-1) + poolb_ref[...]) * pools_ref[...] * (z * _sigmoid(z))
    merged_b_ref[...] = _sigmoid(proj(wrest_ref, 3)) * _dot(y_b.astype(BF16), wb_ref[...])
    resid_ref[...] = alpha * xf

    @pl.when(jnp.logical_not(factorisable))
    def _():
        row = lax.broadcasted_iota(jnp.int32, (ts, ts), 0)
        col = lax.broadcasted_iota(jnp.int32, (ts, ts), 1)
        tri_chunk = jnp.where(row // GLA_CHUNK == col // GLA_CHUNK, tri_ref[...], jnp.zeros((), BF16))
        b_ref[...] = _dot(tri_chunk, la_hi) + _dot(tri_chunk, la_lo)
        _gla_tile_direct(qs_ref, k_ref, b_ref, v_ref, o_ref, state_ref, new_state_ref, hdk=hdk, hdv=hdv)

    state_ref[...] = new_state_ref[...]


def _const_spec(shape):
    nd = len(shape)
    return pl.BlockSpec(shape, lambda s: (0,) * nd, pipeline_mode=pl.Buffered(1))


def _tri_matrix():
    r = np.arange(SEQ_TILE)
    return jnp.asarray((r[None, :] <= r[:, None]).astype(np.float32), dtype=BF16)


def _band_matrices():
    r = np.arange(SEQ_TILE)[:, None] + POOL_LOOKBACK
    c = np.arange(SEQ_TILE + POOL_LOOKBACK)[None, :]
    bands = [((r - c >= 0) & (r - c < w)).astype(np.float32) for w in POOL_WINDOWS]
    return jnp.asarray(np.stack(bands), dtype=BF16)


def _layer(x, w_in, w_gate_up, b_gate, gn_w, pool_w, pool_b, pool_scale, w_a, w_b, w_o, ln_w, ln_b,
           *, alpha):
    bsz, seq, d_model = x.shape
    rank, dk = w_gate_up.shape
    dv = gn_w.shape[0]
    assert seq % SEQ_TILE == 0 and SEQ_TILE % GLA_CHUNK == 0
    assert dv == d_model and 2 * dk == d_model and rank == GLA_GATE_RANK
    assert pool_w.shape[0] == len(POOL_WINDOWS)
    n_head = 2 * dk + 2 * dv + rank
    w_head = w_in[:, :n_head].astype(BF16)
    w_rest = w_in[:, n_head:].astype(BF16)
    row = lambda a: a.reshape(1, -1)
    operands = (x, w_head, w_rest, w_gate_up, row(b_gate), row(gn_w), pool_w.astype(BF16), row(pool_b),
                row(pool_scale), w_a.astype(BF16), w_b.astype(BF16), w_o.astype(BF16), row(ln_w),
                row(ln_b), _tri_matrix(), _band_matrices())
    tiles_per_seq = seq // SEQ_TILE
    n_tiles = bsz * tiles_per_seq

    def tile_block(tile):
        return (tile // tiles_per_seq, tile % tiles_per_seq, 0)

    in_specs = [pl.BlockSpec((None, SEQ_TILE, d_model), lambda s: tile_block(jnp.minimum(s, n_tiles - 1)))]
    in_specs += [_const_spec(a.shape) for a in operands[1:]]
    kernel = functools.partial(_block_kernel, d_model=d_model, dk=dk, dv=dv, alpha=alpha,
                               tiles_per_seq=tiles_per_seq, n_tiles=n_tiles)
    state_shape = (GLA_HEADS, dk // GLA_HEADS, dv // GLA_HEADS)
    return pl.pallas_call(
        kernel,
        grid=(n_tiles + 1,),
        in_specs=in_specs,
        out_specs=pl.BlockSpec((None, SEQ_TILE, d_model), lambda s: tile_block(jnp.maximum(s - 1, 0))),
        out_shape=jax.ShapeDtypeStruct(x.shape, x.dtype),
        scratch_shapes=[
            pltpu.VMEM((SEQ_TILE, dk), F32),
            pltpu.VMEM((SEQ_TILE, dk), F32),
            pltpu.VMEM((SEQ_TILE, dk), F32),
            pltpu.VMEM((SEQ_TILE, dv), BF16),
            pltpu.VMEM((SEQ_TILE, dv), F32),
            pltpu.VMEM(state_shape, F32),
            pltpu.VMEM(state_shape, F32),
            pltpu.VMEM((SEQ_TILE + POOL_LOOKBACK, d_model), BF16),
            pltpu.VMEM((SEQ_TILE, dv), F32),
            pltpu.VMEM((SEQ_TILE, d_model), F32),
            pltpu.VMEM((SEQ_TILE, d_model), F32),
            pltpu.VMEM((SEQ_TILE, d_model), F32),
        ],
        compiler_params=pltpu.CompilerParams(
            dimension_semantics=("arbitrary",),
            vmem_limit_bytes=VMEM_LIMIT_BYTES),
        name="hybrid_gla_pool_layer",
    )(*operands)


def kernel(x, w_in, w_gate_up, b_gate, gn_w, pool_w, pool_b, pool_scale, w_a, w_b, w_o, ln_w, ln_b):
    depth = w_in.shape[0]
    alpha = (2.0 * depth) ** 0.25
    for l in range(depth):
        x = _layer(x, w_in[l], w_gate_up[l], b_gate[l], gn_w[l], pool_w[l], pool_b[l], pool_scale[l],
                   w_a[l], w_b[l], w_o[l], ln_w[l], ln_b[l], alpha=alpha)
    return x
```

```python
import functools

import numpy as np
import jax
import jax.numpy as jnp
from jax import lax
from jax.experimental import pallas as pl
from jax.experimental.pallas import tpu as pltpu

GLA_HEADS = 4
GLA_GATE_RANK = 16
GLA_GATE_TAU = 16.0
POOL_WINDOWS = (2, 4, 8, 16)
POOL_LOOKBACK = 16
NORM_EPS = 1e-5

SEQ_TILE = 256
GLA_CHUNK = 64
VMEM_LIMIT_BYTES = 56 * 1024 * 1024
GLA_FACTOR_MAX_DECAY = 50.0

F32 = jnp.float32
BF16 = jnp.bfloat16


def _dot(a, b):
    return jnp.dot(a, b, preferred_element_type=F32)


def _sigmoid(x):
    return 1.0 / (1.0 + jnp.exp(-x))


def _log_sigmoid(x):
    return jnp.minimum(x, 0.0) - jnp.log1p(jnp.exp(-jnp.abs(x)))


def _row_to_col(row):
    return jnp.transpose(jnp.broadcast_to(row, (8, row.shape[1])))[:, 0:1]


def _gla_factor_operands(qs_ref, k_ref, b_ref):
    b = b_ref[...]
    e_pos = jnp.exp(b)
    e_last = e_pos[SEQ_TILE - 1:SEQ_TILE, :]
    k_neg = k_ref[...] * jnp.exp(-b)
    qd = (qs_ref[...] * e_pos).astype(BF16)
    kn = k_neg.astype(BF16)
    kd = k_neg * e_last
    return qd, kn, kd, e_last


def _gla_tile_factorised(operands, v_ref, o_ref, state_ref, new_state_ref, *, hdk, hdv, fillers):
    ts = SEQ_TILE
    qd, kn, kd, e_last = operands
    heads = range(GLA_HEADS)
    ks = [slice(h * hdk, (h + 1) * hdk) for h in heads]
    vs = [slice(h * hdv, (h + 1) * hdv) for h in heads]
    causal = (lax.broadcasted_iota(jnp.int32, (ts, ts), 0)
              >= lax.broadcasted_iota(jnp.int32, (ts, ts), 1))
    masked = []
    for h in heads:
        scores = lax.dot_general(qd[:, ks[h]], kn[:, ks[h]], (((1,), (1,)), ((), ())),
                                 preferred_element_type=F32)
        fillers[h]()
        masked.append(jnp.where(causal, scores, 0.0).astype(BF16))
    for h in heads:
        kd_t = jnp.transpose(kd[:, ks[h]]).astype(BF16)
        new_state_ref[h] = state_ref[h] * _row_to_col(e_last[:, ks[h]]) + _dot(kd_t, v_ref[:, vs[h]])
    for h in heads:
        o_ref[:, vs[h]] = _dot(qd[:, ks[h]], state_ref[h].astype(BF16)) + _dot(masked[h], v_ref[:, vs[h]])


def _gla_tile_direct(qs_ref, k_ref, b_ref, v_ref, o_ref, state_ref, new_state_ref, *, hdk, hdv):
    new_state_ref[...] = state_ref[...]
    row_c = lax.broadcasted_iota(jnp.int32, (GLA_CHUNK, 1), 0)
    lane_c = lax.broadcasted_iota(jnp.int32, (GLA_CHUNK, GLA_CHUNK), 1)

    for c in range(SEQ_TILE // GLA_CHUNK):
        r0 = c * GLA_CHUNK
        rows = pl.ds(r0, GLA_CHUNK)
        qs_c = qs_ref[rows, :]
        b_c = b_ref[rows, :]

        def col_body(j, a_heads):
            kj = k_ref[pl.ds(r0 + j, 1), :]
            bj = b_ref[pl.ds(r0 + j, 1), :]
            p = qs_c * kj * jnp.exp(jnp.minimum(b_c - bj, 0.0))
            p = jnp.where(row_c >= j, p, 0.0)
            new = []
            for h in range(GLA_HEADS):
                s = jnp.sum(p[:, h * hdk:(h + 1) * hdk], axis=-1, keepdims=True)
                new.append(jnp.where(lane_c == j, s, a_heads[h]))
            return tuple(new)

        a_heads = lax.fori_loop(
            0, GLA_CHUNK, col_body,
            tuple(jnp.zeros((GLA_CHUNK, GLA_CHUNK), F32) for _ in range(GLA_HEADS)))

        b_last = b_ref[pl.ds(r0 + GLA_CHUNK - 1, 1), :]
        qd = (qs_c * jnp.exp(b_c)).astype(BF16)
        kd = k_ref[rows, :] * jnp.exp(b_last - b_c)
        e_last = jnp.exp(b_last)
        for h in range(GLA_HEADS):
            ks = slice(h * hdk, (h + 1) * hdk)
            vs = slice(h * hdv, (h + 1) * hdv)
            v_ch = v_ref[rows, vs]
            st = new_state_ref[h]
            o_ref[rows, vs] = (_dot(qd[:, ks], st.astype(BF16))
                               + _dot(a_heads[h].astype(BF16), v_ch))
            kd_t = jnp.transpose(kd[:, ks]).astype(BF16)
            new_state_ref[h] = st * _row_to_col(e_last[:, ks]) + _dot(kd_t, v_ch)


def _block_kernel(x_ref, whead_ref, wrest_ref, wgu_ref, bg_ref, gnw_ref, poolw_ref, poolb_ref,
                  pools_ref, wa_ref, wb_ref, wo_ref, lnw_ref, lnb_ref, tri_ref, band_ref,
                  out_ref,
                  qs_ref, k_ref, b_ref, v_ref, o_ref, state_ref, new_state_ref, ubuf_ref,
                  ya_scale_ref, gate_a_ref, merged_b_ref, resid_ref,
                  *, d_model, dk, dv, alpha, tiles_per_seq, n_tiles):
    ts = SEQ_TILE
    hdk = dk // GLA_HEADS
    hdv = dv // GLA_HEADS
    s = pl.program_id(0)
    t = lax.rem(jnp.minimum(s, n_tiles - 1), tiles_per_seq)

    @pl.when(s == 0)
    def _():
        for ref in (o_ref, ya_scale_ref, gate_a_ref, merged_b_ref, resid_ref):
            ref[...] = jnp.zeros_like(ref)

    @pl.when(t == 0)
    def _():
        state_ref[...] = jnp.zeros_like(state_ref)
        ubuf_ref[0:POOL_LOOKBACK, :] = jnp.zeros((POOL_LOOKBACK, ubuf_ref.shape[1]), BF16)

    xf = x_ref[...]
    xb = xf.astype(BF16)

    def proj(w_ref, n):
        return _dot(xb, w_ref[:, n * d_model:(n + 1) * d_model])

    o = o_ref[...]
    y_a_parts = []
    for h in range(GLA_HEADS):
        oh = o[:, h * hdv:(h + 1) * hdv]
        y_a_parts.append(oh * lax.rsqrt(jnp.mean(oh * oh, axis=-1, keepdims=True) + NORM_EPS))
    y_a = (jnp.concatenate(y_a_parts, axis=-1) * ya_scale_ref[...]).astype(BF16)

    a_low = _dot(xb, whead_ref[:, 3 * d_model:3 * d_model + GLA_GATE_RANK])
    qk = proj(whead_ref, 0)
    qs_ref[...] = qk[:, :dk] * (hdk ** -0.5)
    k_ref[...] = qk[:, dk:]

    gate_pre = _dot(a_low.astype(BF16), wgu_ref[...].astype(BF16)) + bg_ref[...]
    log_a = _log_sigmoid(gate_pre) * (1.0 / GLA_GATE_TAU)
    la_hi = log_a.astype(BF16)
    la_lo = (log_a - la_hi.astype(F32)).astype(BF16)

    merged = (gate_a_ref[...] * _dot(y_a, wa_ref[...]) + merged_b_ref[...]).astype(BF16)

    b_ref[...] = _dot(tri_ref[...], la_hi) + _dot(tri_ref[...], la_lo)
    factorisable = jnp.min(b_ref[ts - 1:ts, :]) >= -GLA_FACTOR_MAX_DECAY
    v_ref[...] = proj(whead_ref, 1).astype(BF16)

    gla_operands = _gla_factor_operands(qs_ref, k_ref, b_ref)

    r = resid_ref[...] + _dot(merged, wo_ref[...])

    def gate_factor_columns(h):
        def issue():
            cols = slice(h * hdv, (h + 1) * hdv)
            g = _dot(xb, whead_ref[:, 2 * d_model + h * hdv:2 * d_model + (h + 1) * hdv])
            ya_scale_ref[:, cols] = gnw_ref[:, cols] * (g * _sigmoid(g))
        return issue

    _gla_tile_factorised(gla_operands, v_ref, o_ref, state_ref, new_state_ref, hdk=hdk, hdv=hdv,
                         fillers=[gate_factor_columns(h) for h in range(GLA_HEADS)])

    mu = jnp.mean(r, axis=-1, keepdims=True)
    rc = r - mu
    var = jnp.mean(rc * rc, axis=-1, keepdims=True)
    out_ref[...] = (rc * lax.rsqrt(var + NORM_EPS) * lnw_ref[...] + lnb_ref[...]).astype(out_ref.dtype)

    gate_a_ref[...] = _sigmoid(proj(wrest_ref, 2))

    u = proj(wrest_ref, 0)
    ubuf_ref[POOL_LOOKBACK:POOL_LOOKBACK + ts, :] = u.astype(BF16)
    t_abs = t * ts + lax.broadcasted_iota(jnp.int32, (ts, 1), 0)
    gdim = d_model // len(POOL_WINDOWS)
    gslices = [slice(gi * gdim, (gi + 1) * gdim) for gi in range(len(POOL_WINDOWS))]
    win_sums = [_dot(band_ref[gi], ubuf_ref[:, gs]) for gi, gs in enumerate(gslices)]
    ubuf_ref[0:POOL_LOOKBACK, :] = ubuf_ref[ts:ts + POOL_LOOKBACK, :]
    z = proj(wrest_ref, 1)
    p_parts = []
    for gi, w in enumerate(POOL_WINDOWS):
        count = jnp.minimum(t_abs + 1, w).astype(F32)
        pg = win_sums[gi] / count - u[:, gslices[gi]]
        p_parts.append(_dot(pg.astype(BF16), poolw_ref[gi]))
    gate_b = _sigmoid(proj(wrest_ref, 3))
    y_b = (jnp.concatenate(p_parts, axis=-1) + poolb_ref[...]) * pools_ref[...] * (z * _sigmoid(z))
    resid_ref[...] = alpha * xf
    merged_b_ref[...] = gate_b * _dot(y_b.astype(BF16), wb_ref[...])

    @pl.when(jnp.logical_not(factorisable))
    def _():
        row = lax.broadcasted_iota(jnp.int32, (ts, ts), 0)
        col = lax.broadcasted_iota(jnp.int32, (ts, ts), 1)
        tri_chunk = jnp.where(row // GLA_CHUNK == col // GLA_CHUNK, tri_ref[...], jnp.zeros((), BF16))
        b_ref[...] = _dot(tri_chunk, la_hi) + _dot(tri_chunk, la_lo)
        _gla_tile_direct(qs_ref, k_ref, b_ref, v_ref, o_ref, state_ref, new_state_ref, hdk=hdk, hdv=hdv)

    state_ref[...] = new_state_ref[...]


def _const_spec(shape):
    nd = len(shape)
    return pl.BlockSpec(shape, lambda s: (0,) * nd, pipeline_mode=pl.Buffered(1))


def _tri_matrix():
    r = np.arange(SEQ_TILE)
    return jnp.asarray((r[None, :] <= r[:, None]).astype(np.float32), dtype=BF16)


def _band_matrices():
    r = np.arange(SEQ_TILE)[:, None] + POOL_LOOKBACK
    c = np.arange(SEQ_TILE + POOL_LOOKBACK)[None, :]
    bands = [((r - c >= 0) & (r - c < w)).astype(np.float32) for w in POOL_WINDOWS]
    return jnp.asarray(np.stack(bands), dtype=BF16)


def _layer(x, w_in, w_gate_up, b_gate, gn_w, pool_w, pool_b, pool_scale, w_a, w_b, w_o, ln_w, ln_b,
           *, alpha):
    bsz, seq, d_model = x.shape
    rank, dk = w_gate_up.shape
    dv = gn_w.shape[0]
    assert seq % SEQ_TILE == 0 and SEQ_TILE % GLA_CHUNK == 0
    assert dv == d_model and 2 * dk == d_model and rank == GLA_GATE_RANK
    assert pool_w.shape[0] == len(POOL_WINDOWS)
    n_head = 2 * dk + 2 * dv + rank
    w_head = w_in[:, :n_head].astype(BF16)
    w_rest = w_in[:, n_head:].astype(BF16)
    row = lambda a: a.reshape(1, -1)
    operands = (x, w_head, w_rest, w_gate_up, row(b_gate), row(gn_w), pool_w.astype(BF16), row(pool_b),
                row(pool_scale), w_a.astype(BF16), w_b.astype(BF16), w_o.astype(BF16), row(ln_w),
                row(ln_b), _tri_matrix(), _band_matrices())
    tiles_per_seq = seq // SEQ_TILE
    n_tiles = bsz * tiles_per_seq

    def tile_block(tile):
        return (tile // tiles_per_seq, tile % tiles_per_seq, 0)

    in_specs = [pl.BlockSpec((None, SEQ_TILE, d_model), lambda s: tile_block(jnp.minimum(s, n_tiles - 1)))]
    in_specs += [_const_spec(a.shape) for a in operands[1:]]
    kernel = functools.partial(_block_kernel, d_model=d_model, dk=dk, dv=dv, alpha=alpha,
                               tiles_per_seq=tiles_per_seq, n_tiles=n_tiles)
    state_shape = (GLA_HEADS, dk // GLA_HEADS, dv // GLA_HEADS)
    return pl.pallas_call(
        kernel,
        grid=(n_tiles + 1,),
        in_specs=in_specs,
        out_specs=pl.BlockSpec((None, SEQ_TILE, d_model), lambda s: tile_block(jnp.maximum(s - 1, 0))),
        out_shape=jax.ShapeDtypeStruct(x.shape, x.dtype),
        scratch_shapes=[
            pltpu.VMEM((SEQ_TILE, dk), F32),
            pltpu.VMEM((SEQ_TILE, dk), F32),
            pltpu.VMEM((SEQ_TILE, dk), F32),
            pltpu.VMEM((SEQ_TILE, dv), BF16),
            pltpu.VMEM((SEQ_TILE, dv), F32),
            pltpu.VMEM(state_shape, F32),
            pltpu.VMEM(state_shape, F32),
            pltpu.VMEM((SEQ_TILE + POOL_LOOKBACK, d_model), BF16),
            pltpu.VMEM((SEQ_TILE, dv), F32),
            pltpu.VMEM((SEQ_TILE, d_model), F32),
            pltpu.VMEM((SEQ_TILE, d_model), F32),
            pltpu.VMEM((SEQ_TILE, d_model), F32),
        ],
        compiler_params=pltpu.CompilerParams(
            dimension_semantics=("arbitrary",),
            vmem_limit_bytes=VMEM_LIMIT_BYTES),
        name="hybrid_gla_pool_layer",
    )(*operands)


def kernel(x, w_in, w_gate_up, b_gate, gn_w, pool_w, pool_b, pool_scale, w_a, w_b, w_o, ln_w, ln_b):
    depth = w_in.shape[0]
    alpha = (2.0 * depth) ** 0.25
    for l in range(depth):
        x = _layer(x, w_in[l], w_gate_up[l], b_gate[l], gn_w[l], pool_w[l], pool_b[l], pool_scale[l],
                   w_a[l], w_b[l], w_o[l], ln_w[l], ln_b[l], alpha=alpha)
    return x
```

```python
import functools

import numpy as np
import jax
import jax.numpy as jnp
from jax import lax
from jax.experimental import pallas as pl
from jax.experimental.pallas import tpu as pltpu

GLA_HEADS = 4
GLA_GATE_RANK = 16
GLA_GATE_TAU = 16.0
POOL_WINDOWS = (2, 4, 8, 16)
POOL_LOOKBACK = 16
NORM_EPS = 1e-5

SEQ_TILE = 256
GLA_CHUNK = 64
WEIGHT_PREP_ROWS = 128
WEIGHT_PREP_COLS = 512

_QK, _V, _G, _U, _Z, _GATE_A, _GATE_B = range(7)
VMEM_LIMIT_BYTES = 56 * 1024 * 1024
GLA_FACTOR_MAX_DECAY = 50.0

F32 = jnp.float32
BF16 = jnp.bfloat16


def _dot(a, b):
    return jnp.dot(a, b, preferred_element_type=F32)


def _sigmoid(x):
    return 1.0 / (1.0 + jnp.exp(-x))


def _log_sigmoid(x):
    return jnp.minimum(x, 0.0) - jnp.log1p(jnp.exp(-jnp.abs(x)))


def _row_to_col(row):
    return jnp.transpose(jnp.broadcast_to(row, (8, row.shape[1])))[:, 0:1]


def _gla_factor_operands(qs_ref, k_ref, b_ref):
    b = b_ref[...]
    e_pos = jnp.exp(b)
    e_last = e_pos[SEQ_TILE - 1:SEQ_TILE, :]
    k_neg = k_ref[...] * jnp.exp(-b)
    qd = (qs_ref[...] * e_pos).astype(BF16)
    kn = k_neg.astype(BF16)
    kd = k_neg * e_last
    return qd, kn, kd, e_last


def _gla_tile_factorised(operands, v_ref, o_ref, state_ref, new_state_ref, *, hdk, hdv, fillers):
    ts = SEQ_TILE
    qd, kn, kd, e_last = operands
    heads = range(GLA_HEADS)
    ks = [slice(h * hdk, (h + 1) * hdk) for h in heads]
    vs = [slice(h * hdv, (h + 1) * hdv) for h in heads]
    causal = (lax.broadcasted_iota(jnp.int32, (ts, ts), 0)
              >= lax.broadcasted_iota(jnp.int32, (ts, ts), 1))
    masked = []
    for h in heads:
        scores = lax.dot_general(qd[:, ks[h]], kn[:, ks[h]], (((1,), (1,)), ((), ())),
                                 preferred_element_type=F32)
        fillers[h]()
        masked.append(jnp.where(causal, scores, 0.0).astype(BF16))
    for h in heads:
        kd_t = jnp.transpose(kd[:, ks[h]]).astype(BF16)
        new_state_ref[h] = state_ref[h] * _row_to_col(e_last[:, ks[h]]) + _dot(kd_t, v_ref[:, vs[h]])
    for h in heads:
        o_ref[:, vs[h]] = _dot(qd[:, ks[h]], state_ref[h].astype(BF16)) + _dot(masked[h], v_ref[:, vs[h]])


def _gla_tile_direct(qs_ref, k_ref, b_ref, v_ref, o_ref, state_ref, new_state_ref, *, hdk, hdv):
    new_state_ref[...] = state_ref[...]
    row_c = lax.broadcasted_iota(jnp.int32, (GLA_CHUNK, 1), 0)
    lane_c = lax.broadcasted_iota(jnp.int32, (GLA_CHUNK, GLA_CHUNK), 1)

    for c in range(SEQ_TILE // GLA_CHUNK):
        r0 = c * GLA_CHUNK
        rows = pl.ds(r0, GLA_CHUNK)
        qs_c = qs_ref[rows, :]
        b_c = b_ref[rows, :]

        def col_body(j, a_heads):
            kj = k_ref[pl.ds(r0 + j, 1), :]
            bj = b_ref[pl.ds(r0 + j, 1), :]
            p = qs_c * kj * jnp.exp(jnp.minimum(b_c - bj, 0.0))
            p = jnp.where(row_c >= j, p, 0.0)
            new = []
            for h in range(GLA_HEADS):
                s = jnp.sum(p[:, h * hdk:(h + 1) * hdk], axis=-1, keepdims=True)
                new.append(jnp.where(lane_c == j, s, a_heads[h]))
            return tuple(new)

        a_heads = lax.fori_loop(
            0, GLA_CHUNK, col_body,
            tuple(jnp.zeros((GLA_CHUNK, GLA_CHUNK), F32) for _ in range(GLA_HEADS)))

        b_last = b_ref[pl.ds(r0 + GLA_CHUNK - 1, 1), :]
        qd = (qs_c * jnp.exp(b_c)).astype(BF16)
        kd = k_ref[rows, :] * jnp.exp(b_last - b_c)
        e_last = jnp.exp(b_last)
        for h in range(GLA_HEADS):
            ks = slice(h * hdk, (h + 1) * hdk)
            vs = slice(h * hdv, (h + 1) * hdv)
            v_ch = v_ref[rows, vs]
            st = new_state_ref[h]
            o_ref[rows, vs] = (_dot(qd[:, ks], st.astype(BF16))
                               + _dot(a_heads[h].astype(BF16), v_ch))
            kd_t = jnp.transpose(kd[:, ks]).astype(BF16)
            new_state_ref[h] = st * _row_to_col(e_last[:, ks]) + _dot(kd_t, v_ch)


def _block_kernel(x_ref, wbig_ref, wal_ref, wgu_ref, bg_ref, gnw_ref, poolw_ref, poolb_ref,
                  pools_ref, wa_ref, wb_ref, wo_ref, lnw_ref, lnb_ref, tri_ref, band_ref,
                  out_ref,
                  qs_ref, k_ref, b_ref, v_ref, o_ref, state_ref, new_state_ref, ubuf_ref,
                  ya_scale_ref, gate_a_ref, merged_b_ref, resid_ref,
                  *, d_model, dk, dv, alpha, tiles_per_seq, n_tiles):
    ts = SEQ_TILE
    hdk = dk // GLA_HEADS
    hdv = dv // GLA_HEADS
    s = pl.program_id(0)
    t = lax.rem(jnp.minimum(s, n_tiles - 1), tiles_per_seq)

    @pl.when(s == 0)
    def _():
        for ref in (o_ref, ya_scale_ref, gate_a_ref, merged_b_ref, resid_ref):
            ref[...] = jnp.zeros_like(ref)

    @pl.when(t == 0)
    def _():
        state_ref[...] = jnp.zeros_like(state_ref)
        ubuf_ref[0:POOL_LOOKBACK, :] = jnp.zeros((POOL_LOOKBACK, ubuf_ref.shape[1]), BF16)

    xf = x_ref[...]
    xb = xf.astype(BF16)

    def proj(n):
        return _dot(xb, wbig_ref[:, n * d_model:(n + 1) * d_model])

    o = o_ref[...]
    y_a_parts = []
    for h in range(GLA_HEADS):
        oh = o[:, h * hdv:(h + 1) * hdv]
        y_a_parts.append(oh * lax.rsqrt(jnp.mean(oh * oh, axis=-1, keepdims=True) + NORM_EPS))
    y_a = (jnp.concatenate(y_a_parts, axis=-1) * ya_scale_ref[...]).astype(BF16)

    a_low = _dot(xb, wal_ref[...])
    qk = proj(_QK)
    qs_ref[...] = qk[:, :dk] * (hdk ** -0.5)
    k_ref[...] = qk[:, dk:]

    gate_pre = _dot(a_low.astype(BF16), wgu_ref[...].astype(BF16)) + bg_ref[...]
    log_a = _log_sigmoid(gate_pre) * (1.0 / GLA_GATE_TAU)
    la_hi = log_a.astype(BF16)
    la_lo = (log_a - la_hi.astype(F32)).astype(BF16)

    merged = (gate_a_ref[...] * _dot(y_a, wa_ref[...]) + merged_b_ref[...]).astype(BF16)

    b_ref[...] = _dot(tri_ref[...], la_hi) + _dot(tri_ref[...], la_lo)
    factorisable = jnp.min(b_ref[ts - 1:ts, :]) >= -GLA_FACTOR_MAX_DECAY
    v_ref[...] = proj(_V).astype(BF16)

    gla_operands = _gla_factor_operands(qs_ref, k_ref, b_ref)

    r = resid_ref[...] + _dot(merged, wo_ref[...])

    def gate_factor_columns(h):
        def issue():
            cols = slice(h * hdv, (h + 1) * hdv)
            g = _dot(xb, wbig_ref[:, _G * d_model + h * hdv:_G * d_model + (h + 1) * hdv])
            ya_scale_ref[:, cols] = gnw_ref[:, cols] * (g * _sigmoid(g))
        return issue

    _gla_tile_factorised(gla_operands, v_ref, o_ref, state_ref, new_state_ref, hdk=hdk, hdv=hdv,
                         fillers=[gate_factor_columns(h) for h in range(GLA_HEADS)])

    mu = jnp.mean(r, axis=-1, keepdims=True)
    rc = r - mu
    var = jnp.mean(rc * rc, axis=-1, keepdims=True)
    out_ref[...] = (rc * lax.rsqrt(var + NORM_EPS) * lnw_ref[...] + lnb_ref[...]).astype(out_ref.dtype)

    gate_a_ref[...] = _sigmoid(proj(_GATE_A))

    u = proj(_U)
    ubuf_ref[POOL_LOOKBACK:POOL_LOOKBACK + ts, :] = u.astype(BF16)
    t_abs = t * ts + lax.broadcasted_iota(jnp.int32, (ts, 1), 0)
    gdim = d_model // len(POOL_WINDOWS)
    gslices = [slice(gi * gdim, (gi + 1) * gdim) for gi in range(len(POOL_WINDOWS))]
    win_sums = [_dot(band_ref[gi], ubuf_ref[:, gs]) for gi, gs in enumerate(gslices)]
    ubuf_ref[0:POOL_LOOKBACK, :] = ubuf_ref[ts:ts + POOL_LOOKBACK, :]
    z = proj(_Z)
    p_parts = []
    for gi, w in enumerate(POOL_WINDOWS):
        count = jnp.minimum(t_abs + 1, w).astype(F32)
        pg = win_sums[gi] / count - u[:, gslices[gi]]
        p_parts.append(_dot(pg.astype(BF16), poolw_ref[gi]))
    gate_b = _sigmoid(proj(_GATE_B))
    y_b = (jnp.concatenate(p_parts, axis=-1) + poolb_ref[...]) * pools_ref[...] * (z * _sigmoid(z))
    resid_ref[...] = alpha * xf
    merged_b_ref[...] = gate_b * _dot(y_b.astype(BF16), wb_ref[...])

    @pl.when(jnp.logical_not(factorisable))
    def _():
        row = lax.broadcasted_iota(jnp.int32, (ts, ts), 0)
        col = lax.broadcasted_iota(jnp.int32, (ts, ts), 1)
        tri_chunk = jnp.where(row // GLA_CHUNK == col // GLA_CHUNK, tri_ref[...], jnp.zeros((), BF16))
        b_ref[...] = _dot(tri_chunk, la_hi) + _dot(tri_chunk, la_lo)
        _gla_tile_direct(qs_ref, k_ref, b_ref, v_ref, o_ref, state_ref, new_state_ref, hdk=hdk, hdv=hdv)

    state_ref[...] = new_state_ref[...]


def _const_spec(shape):
    nd = len(shape)
    return pl.BlockSpec(shape, lambda s: (0,) * nd, pipeline_mode=pl.Buffered(1))


def _tri_matrix():
    r = np.arange(SEQ_TILE)
    return jnp.asarray((r[None, :] <= r[:, None]).astype(np.float32), dtype=BF16)


def _band_matrices():
    r = np.arange(SEQ_TILE)[:, None] + POOL_LOOKBACK
    c = np.arange(SEQ_TILE + POOL_LOOKBACK)[None, :]
    bands = [((r - c >= 0) & (r - c < w)).astype(np.float32) for w in POOL_WINDOWS]
    return jnp.asarray(np.stack(bands), dtype=BF16)


def _weight_prep_kernel(wint_ref, walt_ref, wa_ref, wb_ref, wo_ref, poolw_ref,
                        wbig_out, wal_out, wa_out, wb_out, wo_out, poolw_out, *, n_row_steps):
    j = pl.program_id(0)
    wbig_out[...] = jnp.transpose(wint_ref[...]).astype(BF16)

    @pl.when(j == 0)
    def _():
        wal_out[...] = jnp.transpose(walt_ref[...]).astype(BF16)

    @pl.when(j < n_row_steps)
    def _():
        wa_out[...] = wa_ref[...].astype(BF16)
        wb_out[...] = wb_ref[...].astype(BF16)
        wo_out[...] = wo_ref[...].astype(BF16)
        poolw_out[...] = poolw_ref[...].astype(BF16)


def _prepare_weights(w_in_t, w_a, w_b, w_o, pool_w, *, n_before_gate, rank):
    d_in, d_model = w_in_t.shape
    groups, gdim, _ = pool_w.shape
    cb = WEIGHT_PREP_COLS
    assert groups * gdim == d_model and d_model % WEIGHT_PREP_ROWS == 0
    assert n_before_gate % cb == 0 and (d_in - rank) % cb == 0 and n_before_gate % rank == 0
    n_col_steps = (d_in - rank) // cb
    n_row_steps = d_model // WEIGHT_PREP_ROWS
    assert n_col_steps >= n_row_steps
    gate_steps = n_before_gate // cb

    def w_in_rows(j):
        return (pl.multiple_of(jnp.where(j < gate_steps, j * cb, j * cb + rank), rank), 0)

    row_block = lambda width: pl.BlockSpec((WEIGHT_PREP_ROWS, width),
                                           lambda j: (jnp.minimum(j, n_row_steps - 1), 0))
    mats = (w_a, w_b, w_o, pool_w.reshape(d_model, gdim))
    outs = pl.pallas_call(
        functools.partial(_weight_prep_kernel, n_row_steps=n_row_steps),
        grid=(n_col_steps,),
        in_specs=[pl.BlockSpec((pl.Element(cb), pl.Element(d_model)), w_in_rows),
                  pl.BlockSpec((rank, d_model), lambda j: (n_before_gate // rank, 0))]
                 + [row_block(m.shape[1]) for m in mats],
        out_specs=[pl.BlockSpec((d_model, cb), lambda j: (0, j)),
                   pl.BlockSpec((d_model, rank), lambda j: (0, 0))]
                  + [row_block(m.shape[1]) for m in mats],
        out_shape=[jax.ShapeDtypeStruct((d_model, d_in - rank), BF16),
                   jax.ShapeDtypeStruct((d_model, rank), BF16)]
                  + [jax.ShapeDtypeStruct(m.shape, BF16) for m in mats],
        compiler_params=pltpu.CompilerParams(dimension_semantics=("arbitrary",)),
        name="weight_prep",
    )(w_in_t, w_in_t, *mats)
    return (*outs[:5], outs[5].reshape(pool_w.shape))


def _layer(x, w_in_t, w_gate_up, b_gate, gn_w, pool_w, pool_b, pool_scale, w_a, w_b, w_o, ln_w, ln_b,
           *, alpha):
    bsz, seq, d_model = x.shape
    rank, dk = w_gate_up.shape
    dv = gn_w.shape[0]
    assert seq % SEQ_TILE == 0 and SEQ_TILE % GLA_CHUNK == 0
    assert dv == d_model and 2 * dk == d_model and rank == GLA_GATE_RANK
    assert pool_w.shape[0] == len(POOL_WINDOWS)
    w_big, w_al, w_a16, w_b16, w_o16, pool_w16 = _prepare_weights(
        w_in_t, w_a, w_b, w_o, pool_w, n_before_gate=2 * dk + 2 * dv, rank=rank)
    row = lambda a: a.reshape(1, -1)
    operands = (x, w_big, w_al, w_gate_up, row(b_gate), row(gn_w), pool_w16, row(pool_b),
                row(pool_scale), w_a16, w_b16, w_o16, row(ln_w),
                row(ln_b), _tri_matrix(), _band_matrices())
    tiles_per_seq = seq // SEQ_TILE
    n_tiles = bsz * tiles_per_seq

    def tile_block(tile):
        return (tile // tiles_per_seq, tile % tiles_per_seq, 0)

    in_specs = [pl.BlockSpec((None, SEQ_TILE, d_model), lambda s: tile_block(jnp.minimum(s, n_tiles - 1)))]
    in_specs += [_const_spec(a.shape) for a in operands[1:]]
    kernel = functools.partial(_block_kernel, d_model=d_model, dk=dk, dv=dv, alpha=alpha,
                               tiles_per_seq=tiles_per_seq, n_tiles=n_tiles)
    state_shape = (GLA_HEADS, dk // GLA_HEADS, dv // GLA_HEADS)
    return pl.pallas_call(
        kernel,
        grid=(n_tiles + 1,),
        in_specs=in_specs,
        out_specs=pl.BlockSpec((None, SEQ_TILE, d_model), lambda s: tile_block(jnp.maximum(s - 1, 0))),
        out_shape=jax.ShapeDtypeStruct(x.shape, x.dtype),
        scratch_shapes=[
            pltpu.VMEM((SEQ_TILE, dk), F32),
            pltpu.VMEM((SEQ_TILE, dk), F32),
            pltpu.VMEM((SEQ_TILE, dk), F32),
            pltpu.VMEM((SEQ_TILE, dv), BF16),
            pltpu.VMEM((SEQ_TILE, dv), F32),
            pltpu.VMEM(state_shape, F32),
            pltpu.VMEM(state_shape, F32),
            pltpu.VMEM((SEQ_TILE + POOL_LOOKBACK, d_model), BF16),
            pltpu.VMEM((SEQ_TILE, dv), F32),
            pltpu.VMEM((SEQ_TILE, d_model), F32),
            pltpu.VMEM((SEQ_TILE, d_model), F32),
            pltpu.VMEM((SEQ_TILE, d_model), F32),
        ],
        compiler_params=pltpu.CompilerParams(
            dimension_semantics=("arbitrary",),
            vmem_limit_bytes=VMEM_LIMIT_BYTES),
        name="hybrid_gla_pool_layer",
    )(*operands)


def kernel(x, w_in, w_gate_up, b_gate, gn_w, pool_w, pool_b, pool_scale, w_a, w_b, w_o, ln_w, ln_b):
    depth = w_in.shape[0]
    alpha = (2.0 * depth) ** 0.25
    w_in_t = jnp.swapaxes(w_in, 1, 2)
    for l in range(depth):
        x = _layer(x, w_in_t[l], w_gate_up[l], b_gate[l], gn_w[l], pool_w[l], pool_b[l], pool_scale[l],
                   w_a[l], w_b[l], w_o[l], ln_w[l], ln_b[l], alpha=alpha)
    return x
```

```python
import functools

import numpy as np
import jax
import jax.numpy as jnp
from jax import lax
from jax.experimental import pallas as pl
from jax.experimental.pallas import tpu as pltpu

GLA_HEADS = 4
GLA_GATE_RANK = 16
GLA_GATE_TAU = 16.0
POOL_WINDOWS = (2, 4, 8, 16)
POOL_LOOKBACK = 16
NORM_EPS = 1e-5

SEQ_TILE = 256
GLA_CHUNK = 64
WEIGHT_PREP_ROWS = 128
WEIGHT_PREP_COLS = 512

_QK, _V, _G, _U, _Z, _GATE_A, _GATE_B = range(7)
VMEM_LIMIT_BYTES = 56 * 1024 * 1024
GLA_FACTOR_MAX_DECAY = 50.0

F32 = jnp.float32
BF16 = jnp.bfloat16


def _dot(a, b):
    return jnp.dot(a, b, preferred_element_type=F32)


def _sigmoid(x):
    return 1.0 / (1.0 + jnp.exp(-x))


def _log_sigmoid(x):
    return jnp.minimum(x, 0.0) - jnp.log(1.0 + jnp.exp(-jnp.abs(x)))


def _row_to_col(row):
    return jnp.transpose(jnp.broadcast_to(row, (8, row.shape[1])))[:, 0:1]


def _gla_factor_operands(qs_ref, k_ref, b_ref):
    b = b_ref[...]
    e_pos = jnp.exp(b)
    e_last = e_pos[SEQ_TILE - 1:SEQ_TILE, :]
    k_neg = k_ref[...] * jnp.exp(-b)
    qd = (qs_ref[...] * e_pos).astype(BF16)
    kn = k_neg.astype(BF16)
    kd = k_neg * e_last
    return qd, kn, kd, e_last


def _gla_tile_factorised(operands, v_ref, o_ref, state_ref, new_state_ref, *, hdk, hdv, fillers):
    ts = SEQ_TILE
    qd, kn, kd, e_last = operands
    heads = range(GLA_HEADS)
    ks = [slice(h * hdk, (h + 1) * hdk) for h in heads]
    vs = [slice(h * hdv, (h + 1) * hdv) for h in heads]
    causal = (lax.broadcasted_iota(jnp.int32, (ts, ts), 0)
              >= lax.broadcasted_iota(jnp.int32, (ts, ts), 1))
    masked = []
    for h in heads:
        scores = lax.dot_general(qd[:, ks[h]], kn[:, ks[h]], (((1,), (1,)), ((), ())),
                                 preferred_element_type=F32)
        fillers[h]()
        masked.append(jnp.where(causal, scores, 0.0).astype(BF16))
    for h in heads:
        kd_t = jnp.transpose(kd[:, ks[h]]).astype(BF16)
        new_state_ref[h] = state_ref[h] * _row_to_col(e_last[:, ks[h]]) + _dot(kd_t, v_ref[:, vs[h]])
    for h in heads:
        o_ref[:, vs[h]] = _dot(qd[:, ks[h]], state_ref[h].astype(BF16)) + _dot(masked[h], v_ref[:, vs[h]])


def _gla_tile_direct(qs_ref, k_ref, b_ref, v_ref, o_ref, state_ref, new_state_ref, *, hdk, hdv):
    new_state_ref[...] = state_ref[...]
    row_c = lax.broadcasted_iota(jnp.int32, (GLA_CHUNK, 1), 0)
    lane_c = lax.broadcasted_iota(jnp.int32, (GLA_CHUNK, GLA_CHUNK), 1)

    for c in range(SEQ_TILE // GLA_CHUNK):
        r0 = c * GLA_CHUNK
        rows = pl.ds(r0, GLA_CHUNK)
        qs_c = qs_ref[rows, :]
        b_c = b_ref[rows, :]

        def col_body(j, a_heads):
            kj = k_ref[pl.ds(r0 + j, 1), :]
            bj = b_ref[pl.ds(r0 + j, 1), :]
            p = qs_c * kj * jnp.exp(jnp.minimum(b_c - bj, 0.0))
            p = jnp.where(row_c >= j, p, 0.0)
            new = []
            for h in range(GLA_HEADS):
                s = jnp.sum(p[:, h * hdk:(h + 1) * hdk], axis=-1, keepdims=True)
                new.append(jnp.where(lane_c == j, s, a_heads[h]))
            return tuple(new)

        a_heads = lax.fori_loop(
            0, GLA_CHUNK, col_body,
            tuple(jnp.zeros((GLA_CHUNK, GLA_CHUNK), F32) for _ in range(GLA_HEADS)))

        b_last = b_ref[pl.ds(r0 + GLA_CHUNK - 1, 1), :]
        qd = (qs_c * jnp.exp(b_c)).astype(BF16)
        kd = k_ref[rows, :] * jnp.exp(b_last - b_c)
        e_last = jnp.exp(b_last)
        for h in range(GLA_HEADS):
            ks = slice(h * hdk, (h + 1) * hdk)
            vs = slice(h * hdv, (h + 1) * hdv)
            v_ch = v_ref[rows, vs]
            st = new_state_ref[h]
            o_ref[rows, vs] = (_dot(qd[:, ks], st.astype(BF16))
                               + _dot(a_heads[h].astype(BF16), v_ch))
            kd_t = jnp.transpose(kd[:, ks]).astype(BF16)
            new_state_ref[h] = st * _row_to_col(e_last[:, ks]) + _dot(kd_t, v_ch)


def _block_kernel(x_ref, wbig_ref, wal_ref, wgu_ref, bg_ref, gnw_ref, poolw_ref, poolb_ref,
                  pools_ref, wa_ref, wb_ref, wo_ref, lnw_ref, lnb_ref, tri_ref, band_ref, bandlb_ref,
                  out_ref,
                  qs_ref, k_ref, b_ref, v_ref, o_ref, state_ref, new_state_ref, ulast_ref,
                  ya_scale_ref, gate_a_ref, merged_b_ref, resid_ref,
                  *, d_model, dk, dv, alpha, tiles_per_seq, n_tiles):
    ts = SEQ_TILE
    hdk = dk // GLA_HEADS
    hdv = dv // GLA_HEADS
    s = pl.program_id(0)
    t = lax.rem(jnp.minimum(s, n_tiles - 1), tiles_per_seq)

    @pl.when(s == 0)
    def _():
        for ref in (o_ref, ya_scale_ref, gate_a_ref, merged_b_ref, resid_ref):
            ref[...] = jnp.zeros_like(ref)

    @pl.when(t == 0)
    def _():
        state_ref[...] = jnp.zeros_like(state_ref)
        ulast_ref[...] = jnp.zeros_like(ulast_ref)

    xf = x_ref[...]
    xb = xf.astype(BF16)

    def proj(n):
        return _dot(xb, wbig_ref[:, n * d_model:(n + 1) * d_model])

    o = o_ref[...]
    y_a_parts = []
    for h in range(GLA_HEADS):
        oh = o[:, h * hdv:(h + 1) * hdv]
        y_a_parts.append(oh * lax.rsqrt(jnp.mean(oh * oh, axis=-1, keepdims=True) + NORM_EPS))
    y_a = (jnp.concatenate(y_a_parts, axis=-1) * ya_scale_ref[...]).astype(BF16)

    a_low = _dot(xb, wal_ref[...])
    qk = proj(_QK)
    qs_ref[...] = qk[:, :dk] * (hdk ** -0.5)
    k_ref[...] = qk[:, dk:]

    gate_pre = _dot(a_low.astype(BF16), wgu_ref[...].astype(BF16)) + bg_ref[...]
    log_a = _log_sigmoid(gate_pre) * (1.0 / GLA_GATE_TAU)
    la_hi = log_a.astype(BF16)
    la_lo = (log_a - la_hi.astype(F32)).astype(BF16)

    merged = (gate_a_ref[...] * _dot(y_a, wa_ref[...]) + merged_b_ref[...]).astype(BF16)

    b_ref[...] = _dot(tri_ref[...], la_hi) + _dot(tri_ref[...], la_lo)
    factorisable = jnp.min(b_ref[ts - 1:ts, :]) >= -GLA_FACTOR_MAX_DECAY
    v_ref[...] = proj(_V).astype(BF16)

    gla_operands = _gla_factor_operands(qs_ref, k_ref, b_ref)

    r = resid_ref[...] + _dot(merged, wo_ref[...])

    def gate_factor_columns(h):
        def issue():
            cols = slice(h * hdv, (h + 1) * hdv)
            g = _dot(xb, wbig_ref[:, _G * d_model + h * hdv:_G * d_model + (h + 1) * hdv])
            ya_scale_ref[:, cols] = gnw_ref[:, cols] * (g * _sigmoid(g))
        return issue

    _gla_tile_factorised(gla_operands, v_ref, o_ref, state_ref, new_state_ref, hdk=hdk, hdv=hdv,
                         fillers=[gate_factor_columns(h) for h in range(GLA_HEADS)])

    mu = jnp.mean(r, axis=-1, keepdims=True)
    rc = r - mu
    var = jnp.mean(rc * rc, axis=-1, keepdims=True)
    out_ref[...] = (rc * lax.rsqrt(var + NORM_EPS) * lnw_ref[...] + lnb_ref[...]).astype(out_ref.dtype)

    gate_a_ref[...] = _sigmoid(proj(_GATE_A))

    u = proj(_U)
    ub = u.astype(BF16)
    t_abs = t * ts + lax.broadcasted_iota(jnp.int32, (ts, 1), 0)
    gdim = d_model // len(POOL_WINDOWS)
    gslices = [slice(gi * gdim, (gi + 1) * gdim) for gi in range(len(POOL_WINDOWS))]
    win_sums = []
    for gi, gs in enumerate(gslices):
        in_tile = _dot(band_ref[gi], ub[:, gs])
        carried = _dot(bandlb_ref[gi], ulast_ref[:, gs])
        win_sums.append(jnp.concatenate([in_tile[:POOL_LOOKBACK] + carried, in_tile[POOL_LOOKBACK:]], axis=0))
    ulast_ref[...] = ub[ts - POOL_LOOKBACK:, :]
    z = proj(_Z)
    p_parts = []
    for gi, w in enumerate(POOL_WINDOWS):
        count = jnp.minimum(t_abs + 1, w).astype(F32)
        pg = win_sums[gi] / count - u[:, gslices[gi]]
        p_parts.append(_dot(pg.astype(BF16), poolw_ref[gi]))
    gate_b = _sigmoid(proj(_GATE_B))
    y_b = (jnp.concatenate(p_parts, axis=-1) + poolb_ref[...]) * pools_ref[...] * (z * _sigmoid(z))
    resid_ref[...] = alpha * xf
    merged_b_ref[...] = gate_b * _dot(y_b.astype(BF16), wb_ref[...])

    @pl.when(jnp.logical_not(factorisable))
    def _():
        row = lax.broadcasted_iota(jnp.int32, (ts, ts), 0)
        col = lax.broadcasted_iota(jnp.int32, (ts, ts), 1)
        tri_chunk = jnp.where(row // GLA_CHUNK == col // GLA_CHUNK, tri_ref[...], jnp.zeros((), BF16))
        b_ref[...] = _dot(tri_chunk, la_hi) + _dot(tri_chunk, la_lo)
        _gla_tile_direct(qs_ref, k_ref, b_ref, v_ref, o_ref, state_ref, new_state_ref, hdk=hdk, hdv=hdv)

    state_ref[...] = new_state_ref[...]


def _const_spec(shape):
    nd = len(shape)
    return pl.BlockSpec(shape, lambda s: (0,) * nd, pipeline_mode=pl.Buffered(1))


def _tri_matrix():
    r = np.arange(SEQ_TILE)
    return jnp.asarray((r[None, :] <= r[:, None]).astype(np.float32), dtype=BF16)


def _band_matrices():
    r = np.arange(SEQ_TILE)[:, None]
    c = np.arange(SEQ_TILE)[None, :]
    c_lb = np.arange(POOL_LOOKBACK)[None, :] - POOL_LOOKBACK
    in_win = lambda d, w: ((d >= 0) & (d < w)).astype(np.float32)
    bands = np.stack([in_win(r - c, w) for w in POOL_WINDOWS])
    bands_lb = np.stack([in_win(r[:POOL_LOOKBACK] - c_lb, w) for w in POOL_WINDOWS])
    return jnp.asarray(bands, dtype=BF16), jnp.asarray(bands_lb, dtype=BF16)


def _weight_prep_kernel(wint_ref, walt_ref, wa_ref, wb_ref, wo_ref, poolw_ref,
                        wbig_out, wal_out, wa_out, wb_out, wo_out, poolw_out, *, n_row_steps):
    j = pl.program_id(0)
    wbig_out[...] = jnp.transpose(wint_ref[...]).astype(BF16)

    @pl.when(j == 0)
    def _():
        wal_out[...] = jnp.transpose(walt_ref[...]).astype(BF16)

    @pl.when(j < n_row_steps)
    def _():
        wa_out[...] = wa_ref[...].astype(BF16)
        wb_out[...] = wb_ref[...].astype(BF16)
        wo_out[...] = wo_ref[...].astype(BF16)
        poolw_out[...] = poolw_ref[...].astype(BF16)


def _prepare_weights(w_in_t, w_a, w_b, w_o, pool_w, *, n_before_gate, rank):
    d_in, d_model = w_in_t.shape
    groups, gdim, _ = pool_w.shape
    cb = WEIGHT_PREP_COLS
    assert groups * gdim == d_model and d_model % WEIGHT_PREP_ROWS == 0
    assert n_before_gate % cb == 0 and (d_in - rank) % cb == 0 and n_before_gate % rank == 0
    n_col_steps = (d_in - rank) // cb
    n_row_steps = d_model // WEIGHT_PREP_ROWS
    assert n_col_steps >= n_row_steps
    gate_steps = n_before_gate // cb

    def w_in_rows(j):
        return (pl.multiple_of(jnp.where(j < gate_steps, j * cb, j * cb + rank), rank), 0)

    row_block = lambda width: pl.BlockSpec((WEIGHT_PREP_ROWS, width),
                                           lambda j: (jnp.minimum(j, n_row_steps - 1), 0))
    mats = (w_a, w_b, w_o, pool_w.reshape(d_model, gdim))
    outs = pl.pallas_call(
        functools.partial(_weight_prep_kernel, n_row_steps=n_row_steps),
        grid=(n_col_steps,),
        in_specs=[pl.BlockSpec((pl.Element(cb), pl.Element(d_model)), w_in_rows),
                  pl.BlockSpec((rank, d_model), lambda j: (n_before_gate // rank, 0))]
                 + [row_block(m.shape[1]) for m in mats],
        out_specs=[pl.BlockSpec((d_model, cb), lambda j: (0, j)),
                   pl.BlockSpec((d_model, rank), lambda j: (0, 0))]
                  + [row_block(m.shape[1]) for m in mats],
        out_shape=[jax.ShapeDtypeStruct((d_model, d_in - rank), BF16),
                   jax.ShapeDtypeStruct((d_model, rank), BF16)]
                  + [jax.ShapeDtypeStruct(m.shape, BF16) for m in mats],
        compiler_params=pltpu.CompilerParams(dimension_semantics=("arbitrary",)),
        name="weight_prep",
    )(w_in_t, w_in_t, *mats)
    return (*outs[:5], outs[5].reshape(pool_w.shape))


def _layer(x, w_in_t, w_gate_up, b_gate, gn_w, pool_w, pool_b, pool_scale, w_a, w_b, w_o, ln_w, ln_b,
           *, alpha):
    bsz, seq, d_model = x.shape
    rank, dk = w_gate_up.shape
    dv = gn_w.shape[0]
    assert seq % SEQ_TILE == 0 and SEQ_TILE % GLA_CHUNK == 0
    assert dv == d_model and 2 * dk == d_model and rank == GLA_GATE_RANK
    assert pool_w.shape[0] == len(POOL_WINDOWS)
    w_big, w_al, w_a16, w_b16, w_o16, pool_w16 = _prepare_weights(
        w_in_t, w_a, w_b, w_o, pool_w, n_before_gate=2 * dk + 2 * dv, rank=rank)
    row = lambda a: a.reshape(1, -1)
    operands = (x, w_big, w_al, w_gate_up, row(b_gate), row(gn_w), pool_w16, row(pool_b),
                row(pool_scale), w_a16, w_b16, w_o16, row(ln_w),
                row(ln_b), _tri_matrix(), *_band_matrices())
    tiles_per_seq = seq // SEQ_TILE
    n_tiles = bsz * tiles_per_seq

    def tile_block(tile):
        return (tile // tiles_per_seq, tile % tiles_per_seq, 0)

    in_specs = [pl.BlockSpec((None, SEQ_TILE, d_model), lambda s: tile_block(jnp.minimum(s, n_tiles - 1)))]
    in_specs += [_const_spec(a.shape) for a in operands[1:]]
    kernel = functools.partial(_block_kernel, d_model=d_model, dk=dk, dv=dv, alpha=alpha,
                               tiles_per_seq=tiles_per_seq, n_tiles=n_tiles)
    state_shape = (GLA_HEADS, dk // GLA_HEADS, dv // GLA_HEADS)
    return pl.pallas_call(
        kernel,
        grid=(n_tiles + 1,),
        in_specs=in_specs,
        out_specs=pl.BlockSpec((None, SEQ_TILE, d_model), lambda s: tile_block(jnp.maximum(s - 1, 0))),
        out_shape=jax.ShapeDtypeStruct(x.shape, x.dtype),
        scratch_shapes=[
            pltpu.VMEM((SEQ_TILE, dk), F32),
            pltpu.VMEM((SEQ_TILE, dk), F32),
            pltpu.VMEM((SEQ_TILE, dk), F32),
            pltpu.VMEM((SEQ_TILE, dv), BF16),
            pltpu.VMEM((SEQ_TILE, dv), F32),
            pltpu.VMEM(state_shape, F32),
            pltpu.VMEM(state_shape, F32),
            pltpu.VMEM((POOL_LOOKBACK, d_model), BF16),
            pltpu.VMEM((SEQ_TILE, dv), F32),
            pltpu.VMEM((SEQ_TILE, d_model), F32),
            pltpu.VMEM((SEQ_TILE, d_model), F32),
            pltpu.VMEM((SEQ_TILE, d_model), F32),
        ],
        compiler_params=pltpu.CompilerParams(
            dimension_semantics=("arbitrary",),
            vmem_limit_bytes=VMEM_LIMIT_BYTES),
        name="hybrid_gla_pool_layer",
    )(*operands)


def kernel(x, w_in, w_gate_up, b_gate, gn_w, pool_w, pool_b, pool_scale, w_a, w_b, w_o, ln_w, ln_b):
    depth = w_in.shape[0]
    alpha = (2.0 * depth) ** 0.25
    w_in_t = jnp.swapaxes(w_in, 1, 2)
    for l in range(depth):
        x = _layer(x, w_in_t[l], w_gate_up[l], b_gate[l], gn_w[l], pool_w[l], pool_b[l], pool_scale[l],
                   w_a[l], w_b[l], w_o[l], ln_w[l], ln_b[l], alpha=alpha)
    return x
```

```python
import functools

import numpy as np
import jax
import jax.numpy as jnp
from jax import lax
from jax.experimental import pallas as pl
from jax.experimental.pallas import tpu as pltpu

GLA_HEADS = 4
GLA_GATE_RANK = 16
GLA_GATE_TAU = 16.0
POOL_WINDOWS = (2, 4, 8, 16)
POOL_LOOKBACK = 16
NORM_EPS = 1e-5

SEQ_TILE = 256
GLA_CHUNK = 64
WEIGHT_PREP_ROWS = 128
WEIGHT_PREP_COLS = 512

_QK, _V, _G, _U, _Z, _GATE_A, _GATE_B = range(7)
VMEM_LIMIT_BYTES = 56 * 1024 * 1024
GLA_FACTOR_MAX_DECAY = 50.0

F32 = jnp.float32
BF16 = jnp.bfloat16


def _dot(a, b):
    return jnp.dot(a, b, preferred_element_type=F32)


def _sigmoid(x):
    return 1.0 / (1.0 + jnp.exp(-x))


def _log_sigmoid(x):
    return jnp.minimum(x, 0.0) - jnp.log(1.0 + jnp.exp(-jnp.abs(x)))


def _hi_lo(x):
    hi = x.astype(BF16)
    return hi, (x - hi.astype(F32)).astype(BF16)


def _row_to_col(row):
    return jnp.transpose(jnp.broadcast_to(row, (8, row.shape[1])))[:, 0:1]


def _gla_factor_operands(qs_ref, k_ref, b_ref):
    b = b_ref[...]
    e_pos = jnp.exp(b)
    e_last = e_pos[SEQ_TILE - 1:SEQ_TILE, :]
    k_neg = k_ref[...] * jnp.exp(-b)
    qd = (qs_ref[...] * e_pos).astype(BF16)
    kn = k_neg.astype(BF16)
    kd = k_neg * e_last
    return qd, kn, kd, e_last


def _gla_tile_factorised(operands, v_ref, o_ref, state_ref, new_state_ref, *, hdk, hdv, fillers):
    ts = SEQ_TILE
    qd, kn, kd, e_last = operands
    heads = range(GLA_HEADS)
    ks = [slice(h * hdk, (h + 1) * hdk) for h in heads]
    vs = [slice(h * hdv, (h + 1) * hdv) for h in heads]
    causal = (lax.broadcasted_iota(jnp.int32, (ts, ts), 0)
              >= lax.broadcasted_iota(jnp.int32, (ts, ts), 1))
    masked = []
    for h in heads:
        scores = lax.dot_general(qd[:, ks[h]], kn[:, ks[h]], (((1,), (1,)), ((), ())),
                                 preferred_element_type=F32)
        fillers[h]()
        masked.append(jnp.where(causal, scores, 0.0).astype(BF16))
    for h in heads:
        kd_t = jnp.transpose(kd[:, ks[h]]).astype(BF16)
        new_state_ref[h] = state_ref[h] * _row_to_col(e_last[:, ks[h]]) + _dot(kd_t, v_ref[:, vs[h]])
    for h in heads:
        o_ref[:, vs[h]] = _dot(qd[:, ks[h]], state_ref[h].astype(BF16)) + _dot(masked[h], v_ref[:, vs[h]])


def _gla_tile_direct(qs_ref, k_ref, b_ref, v_ref, o_ref, state_ref, new_state_ref, *, hdk, hdv):
    new_state_ref[...] = state_ref[...]
    row_c = lax.broadcasted_iota(jnp.int32, (GLA_CHUNK, 1), 0)
    lane_c = lax.broadcasted_iota(jnp.int32, (GLA_CHUNK, GLA_CHUNK), 1)

    for c in range(SEQ_TILE // GLA_CHUNK):
        r0 = c * GLA_CHUNK
        rows = pl.ds(r0, GLA_CHUNK)
        qs_c = qs_ref[rows, :]
        b_c = b_ref[rows, :]

        def col_body(j, a_heads):
            kj = k_ref[pl.ds(r0 + j, 1), :]
            bj = b_ref[pl.ds(r0 + j, 1), :]
            p = qs_c * kj * jnp.exp(jnp.minimum(b_c - bj, 0.0))
            p = jnp.where(row_c >= j, p, 0.0)
            new = []
            for h in range(GLA_HEADS):
                s = jnp.sum(p[:, h * hdk:(h + 1) * hdk], axis=-1, keepdims=True)
                new.append(jnp.where(lane_c == j, s, a_heads[h]))
            return tuple(new)

        a_heads = lax.fori_loop(
            0, GLA_CHUNK, col_body,
            tuple(jnp.zeros((GLA_CHUNK, GLA_CHUNK), F32) for _ in range(GLA_HEADS)))

        b_last = b_ref[pl.ds(r0 + GLA_CHUNK - 1, 1), :]
        qd = (qs_c * jnp.exp(b_c)).astype(BF16)
        kd = k_ref[rows, :] * jnp.exp(b_last - b_c)
        e_last = jnp.exp(b_last)
        for h in range(GLA_HEADS):
            ks = slice(h * hdk, (h + 1) * hdk)
            vs = slice(h * hdv, (h + 1) * hdv)
            v_ch = v_ref[rows, vs]
            st = new_state_ref[h]
            o_ref[rows, vs] = (_dot(qd[:, ks], st.astype(BF16))
                               + _dot(a_heads[h].astype(BF16), v_ch))
            kd_t = jnp.transpose(kd[:, ks]).astype(BF16)
            new_state_ref[h] = st * _row_to_col(e_last[:, ks]) + _dot(kd_t, v_ch)


def _block_kernel(x_ref, wbig_ref, wal_ref, wgu_ref, bg_ref, gnw_ref, poolw_ref, poolb_ref,
                  pools_ref, wa_ref, wb_ref, wo_ref, lnw_ref, lnb_ref, tri_ref, band_ref, bandlb_ref,
                  out_ref,
                  qs_ref, k_ref, b_ref, v_ref, o_ref, state_ref, new_state_ref, ulast_ref,
                  ya_scale_ref, gate_a_ref, merged_b_ref, resid_ref, la_ref, factorisable_ref,
                  *, d_model, dk, dv, alpha, tiles_per_seq, n_tiles):
    ts = SEQ_TILE
    hdk = dk // GLA_HEADS
    hdv = dv // GLA_HEADS
    s = pl.program_id(0)
    t = lax.rem(jnp.minimum(s, n_tiles - 1), tiles_per_seq)

    @pl.when(t == 0)
    def _():
        state_ref[...] = jnp.zeros_like(state_ref)
        ulast_ref[...] = jnp.zeros_like(ulast_ref)

    def step(do_head, do_tail):
        if do_head:
            xf = x_ref[...]
            xb = xf.astype(BF16)

            def proj(n):
                return _dot(xb, wbig_ref[:, n * d_model:(n + 1) * d_model])

        if do_tail:
            o = o_ref[...]
            y_a_parts = []
            for h in range(GLA_HEADS):
                oh = o[:, h * hdv:(h + 1) * hdv]
                y_a_parts.append(oh * lax.rsqrt(jnp.mean(oh * oh, axis=-1, keepdims=True) + NORM_EPS))
            y_a = (jnp.concatenate(y_a_parts, axis=-1) * ya_scale_ref[...]).astype(BF16)

        if do_head:
            a_low = _dot(xb, wal_ref[...])
            qk = proj(_QK)
            qs_ref[...] = qk[:, :dk] * (hdk ** -0.5)
            k_ref[...] = qk[:, dk:]
            gate_pre = _dot(a_low.astype(BF16), wgu_ref[...].astype(BF16)) + bg_ref[...]
            log_a = _log_sigmoid(gate_pre) * (1.0 / GLA_GATE_TAU)
            la_ref[...] = log_a
            la_hi, la_lo = _hi_lo(log_a)

        if do_tail:
            merged = (gate_a_ref[...] * _dot(y_a, wa_ref[...]) + merged_b_ref[...]).astype(BF16)

        if do_head:
            b_ref[...] = _dot(tri_ref[...], la_hi) + _dot(tri_ref[...], la_lo)
            factorisable = jnp.min(b_ref[ts - 1:ts, :]) >= -GLA_FACTOR_MAX_DECAY
            factorisable_ref[0] = factorisable.astype(jnp.int32)
            v_ref[...] = proj(_V).astype(BF16)
            gla_operands = _gla_factor_operands(qs_ref, k_ref, b_ref)

        if do_tail:
            r = resid_ref[...] + _dot(merged, wo_ref[...])

        if do_head:
            def gate_factor_columns(h):
                def issue():
                    cols = slice(h * hdv, (h + 1) * hdv)
                    g = _dot(xb, wbig_ref[:, _G * d_model + h * hdv:_G * d_model + (h + 1) * hdv])
                    ya_scale_ref[:, cols] = gnw_ref[:, cols] * (g * _sigmoid(g))
                return issue

            _gla_tile_factorised(gla_operands, v_ref, o_ref, state_ref, new_state_ref, hdk=hdk, hdv=hdv,
                                 fillers=[gate_factor_columns(h) for h in range(GLA_HEADS)])

        if do_tail:
            mu = jnp.mean(r, axis=-1, keepdims=True)
            rc = r - mu
            var = jnp.mean(rc * rc, axis=-1, keepdims=True)
            out_ref[...] = (rc * lax.rsqrt(var + NORM_EPS) * lnw_ref[...] + lnb_ref[...]).astype(out_ref.dtype)

        if do_head:
            gate_a_ref[...] = _sigmoid(proj(_GATE_A))
            u = proj(_U)
            ub = u.astype(BF16)
            t_abs = t * ts + lax.broadcasted_iota(jnp.int32, (ts, 1), 0)
            gdim = d_model // len(POOL_WINDOWS)
            gslices = [slice(gi * gdim, (gi + 1) * gdim) for gi in range(len(POOL_WINDOWS))]
            win_sums = []
            for gi, gs in enumerate(gslices):
                in_tile = _dot(band_ref[gi], ub[:, gs])
                carried = _dot(bandlb_ref[gi], ulast_ref[:, gs])
                win_sums.append(jnp.concatenate(
                    [in_tile[:POOL_LOOKBACK] + carried, in_tile[POOL_LOOKBACK:]], axis=0))
            ulast_ref[...] = ub[ts - POOL_LOOKBACK:, :]
            z = proj(_Z)
            p_parts = []
            for gi, w in enumerate(POOL_WINDOWS):
                count = jnp.minimum(t_abs + 1, w).astype(F32)
                pg = win_sums[gi] / count - u[:, gslices[gi]]
                p_parts.append(_dot(pg.astype(BF16), poolw_ref[gi]))
            gate_b = _sigmoid(proj(_GATE_B))
            y_b = (jnp.concatenate(p_parts, axis=-1) + poolb_ref[...]) * pools_ref[...] * (z * _sigmoid(z))
            resid_ref[...] = alpha * xf
            merged_b_ref[...] = gate_b * _dot(y_b.astype(BF16), wb_ref[...])

    pl.when(s == 0)(functools.partial(step, True, False))
    pl.when(jnp.logical_and(s > 0, s < n_tiles))(functools.partial(step, True, True))
    pl.when(s == n_tiles)(functools.partial(step, False, True))

    @pl.when(s < n_tiles)
    def _():
        @pl.when(factorisable_ref[0] == 0)
        def _():
            row = lax.broadcasted_iota(jnp.int32, (ts, ts), 0)
            col = lax.broadcasted_iota(jnp.int32, (ts, ts), 1)
            tri_chunk = jnp.where(row // GLA_CHUNK == col // GLA_CHUNK, tri_ref[...], jnp.zeros((), BF16))
            la_hi, la_lo = _hi_lo(la_ref[...])
            b_ref[...] = _dot(tri_chunk, la_hi) + _dot(tri_chunk, la_lo)
            _gla_tile_direct(qs_ref, k_ref, b_ref, v_ref, o_ref, state_ref, new_state_ref, hdk=hdk, hdv=hdv)

        state_ref[...] = new_state_ref[...]


def _const_spec(shape):
    nd = len(shape)
    return pl.BlockSpec(shape, lambda s: (0,) * nd, pipeline_mode=pl.Buffered(1))


def _tri_matrix():
    r = np.arange(SEQ_TILE)
    return jnp.asarray((r[None, :] <= r[:, None]).astype(np.float32), dtype=BF16)


def _band_matrices():
    r = np.arange(SEQ_TILE)[:, None]
    c = np.arange(SEQ_TILE)[None, :]
    c_lb = np.arange(POOL_LOOKBACK)[None, :] - POOL_LOOKBACK
    in_win = lambda d, w: ((d >= 0) & (d < w)).astype(np.float32)
    bands = np.stack([in_win(r - c, w) for w in POOL_WINDOWS])
    bands_lb = np.stack([in_win(r[:POOL_LOOKBACK] - c_lb, w) for w in POOL_WINDOWS])
    return jnp.asarray(bands, dtype=BF16), jnp.asarray(bands_lb, dtype=BF16)


def _weight_prep_kernel(wint_ref, walt_ref, wa_ref, wb_ref, wo_ref, poolw_ref,
                        wbig_out, wal_out, wa_out, wb_out, wo_out, poolw_out, *, n_row_steps):
    j = pl.program_id(0)
    wbig_out[...] = jnp.transpose(wint_ref[...]).astype(BF16)

    @pl.when(j == 0)
    def _():
        wal_out[...] = jnp.transpose(walt_ref[...]).astype(BF16)

    @pl.when(j < n_row_steps)
    def _():
        wa_out[...] = wa_ref[...].astype(BF16)
        wb_out[...] = wb_ref[...].astype(BF16)
        wo_out[...] = wo_ref[...].astype(BF16)
        poolw_out[...] = poolw_ref[...].astype(BF16)


def _prepare_weights(w_in_t, w_a, w_b, w_o, pool_w, *, n_before_gate, rank):
    d_in, d_model = w_in_t.shape
    groups, gdim, _ = pool_w.shape
    cb = WEIGHT_PREP_COLS
    assert groups * gdim == d_model and d_model % WEIGHT_PREP_ROWS == 0
    assert n_before_gate % cb == 0 and (d_in - rank) % cb == 0 and n_before_gate % rank == 0
    n_col_steps = (d_in - rank) // cb
    n_row_steps = d_model // WEIGHT_PREP_ROWS
    assert n_col_steps >= n_row_steps
    gate_steps = n_before_gate // cb

    def w_in_rows(j):
        return (pl.multiple_of(jnp.where(j < gate_steps, j * cb, j * cb + rank), rank), 0)

    row_block = lambda width: pl.BlockSpec((WEIGHT_PREP_ROWS, width),
                                           lambda j: (jnp.minimum(j, n_row_steps - 1), 0))
    mats = (w_a, w_b, w_o, pool_w.reshape(d_model, gdim))
    outs = pl.pallas_call(
        functools.partial(_weight_prep_kernel, n_row_steps=n_row_steps),
        grid=(n_col_steps,),
        in_specs=[pl.BlockSpec((pl.Element(cb), pl.Element(d_model)), w_in_rows),
                  pl.BlockSpec((rank, d_model), lambda j: (n_before_gate // rank, 0))]
                 + [row_block(m.shape[1]) for m in mats],
        out_specs=[pl.BlockSpec((d_model, cb), lambda j: (0, j)),
                   pl.BlockSpec((d_model, rank), lambda j: (0, 0))]
                  + [row_block(m.shape[1]) for m in mats],
        out_shape=[jax.ShapeDtypeStruct((d_model, d_in - rank), BF16),
                   jax.ShapeDtypeStruct((d_model, rank), BF16)]
                  + [jax.ShapeDtypeStruct(m.shape, BF16) for m in mats],
        compiler_params=pltpu.CompilerParams(dimension_semantics=("arbitrary",)),
        name="weight_prep",
    )(w_in_t, w_in_t, *mats)
    return (*outs[:5], outs[5].reshape(pool_w.shape))


def _layer(x, w_in_t, w_gate_up, b_gate, gn_w, pool_w, pool_b, pool_scale, w_a, w_b, w_o, ln_w, ln_b,
           *, alpha):
    bsz, seq, d_model = x.shape
    rank, dk = w_gate_up.shape
    dv = gn_w.shape[0]
    assert seq % SEQ_TILE == 0 and SEQ_TILE % GLA_CHUNK == 0
    assert dv == d_model and 2 * dk == d_model and rank == GLA_GATE_RANK
    assert pool_w.shape[0] == len(POOL_WINDOWS)
    w_big, w_al, w_a16, w_b16, w_o16, pool_w16 = _prepare_weights(
        w_in_t, w_a, w_b, w_o, pool_w, n_before_gate=2 * dk + 2 * dv, rank=rank)
    row = lambda a: a.reshape(1, -1)
    operands = (x, w_big, w_al, w_gate_up, row(b_gate), row(gn_w), pool_w16, row(pool_b),
                row(pool_scale), w_a16, w_b16, w_o16, row(ln_w),
                row(ln_b), _tri_matrix(), *_band_matrices())
    tiles_per_seq = seq // SEQ_TILE
    n_tiles = bsz * tiles_per_seq

    def tile_block(tile):
        return (tile // tiles_per_seq, tile % tiles_per_seq, 0)

    in_specs = [pl.BlockSpec((None, SEQ_TILE, d_model), lambda s: tile_block(jnp.minimum(s, n_tiles - 1)))]
    in_specs += [_const_spec(a.shape) for a in operands[1:]]
    kernel = functools.partial(_block_kernel, d_model=d_model, dk=dk, dv=dv, alpha=alpha,
                               tiles_per_seq=tiles_per_seq, n_tiles=n_tiles)
    state_shape = (GLA_HEADS, dk // GLA_HEADS, dv // GLA_HEADS)
    return pl.pallas_call(
        kernel,
        grid=(n_tiles + 1,),
        in_specs=in_specs,
        out_specs=pl.BlockSpec((None, SEQ_TILE, d_model), lambda s: tile_block(jnp.maximum(s - 1, 0))),
        out_shape=jax.ShapeDtypeStruct(x.shape, x.dtype),
        scratch_shapes=[
            pltpu.VMEM((SEQ_TILE, dk), F32),
            pltpu.VMEM((SEQ_TILE, dk), F32),
            pltpu.VMEM((SEQ_TILE, dk), F32),
            pltpu.VMEM((SEQ_TILE, dv), BF16),
            pltpu.VMEM((SEQ_TILE, dv), F32),
            pltpu.VMEM(state_shape, F32),
            pltpu.VMEM(state_shape, F32),
            pltpu.VMEM((POOL_LOOKBACK, d_model), BF16),
            pltpu.VMEM((SEQ_TILE, dv), F32),
            pltpu.VMEM((SEQ_TILE, d_model), F32),
            pltpu.VMEM((SEQ_TILE, d_model), F32),
            pltpu.VMEM((SEQ_TILE, d_model), F32),
            pltpu.VMEM((SEQ_TILE, dk), F32),
            pltpu.SMEM((1,), jnp.int32),
        ],
        compiler_params=pltpu.CompilerParams(
            dimension_semantics=("arbitrary",),
            vmem_limit_bytes=VMEM_LIMIT_BYTES),
        name="hybrid_gla_pool_layer",
    )(*operands)


def kernel(x, w_in, w_gate_up, b_gate, gn_w, pool_w, pool_b, pool_scale, w_a, w_b, w_o, ln_w, ln_b):
    depth = w_in.shape[0]
    alpha = (2.0 * depth) ** 0.25
    w_in_t = jnp.swapaxes(w_in, 1, 2)
    for l in range(depth):
        x = _layer(x, w_in_t[l], w_gate_up[l], b_gate[l], gn_w[l], pool_w[l], pool_b[l], pool_scale[l],
                   w_a[l], w_b[l], w_o[l], ln_w[l], ln_b[l], alpha=alpha)
    return x
```

```python
import functools

import numpy as np
import jax
import jax.numpy as jnp
from jax import lax
from jax.experimental import pallas as pl
from jax.experimental.pallas import tpu as pltpu

GLA_HEADS = 4
GLA_GATE_RANK = 16
GLA_GATE_TAU = 16.0
POOL_WINDOWS = (2, 4, 8, 16)
POOL_LOOKBACK = 16
NORM_EPS = 1e-5

SEQ_TILE = 256
GLA_CHUNK = 64
WEIGHT_LOAD_ROWS = 512

_QK, _V, _G, _U, _Z, _GATE_A, _GATE_B = range(7)
VMEM_LIMIT_BYTES = 56 * 1024 * 1024
GLA_FACTOR_MAX_DECAY = 50.0

F32 = jnp.float32
BF16 = jnp.bfloat16


def _dot(a, b):
    return jnp.dot(a, b, preferred_element_type=F32)


def _sigmoid(x):
    return 1.0 / (1.0 + jnp.exp(-x))


def _log_sigmoid(x):
    return jnp.minimum(x, 0.0) - jnp.log(1.0 + jnp.exp(-jnp.abs(x)))


def _hi_lo(x):
    hi = x.astype(BF16)
    return hi, (x - hi.astype(F32)).astype(BF16)


def _row_to_col(row):
    return jnp.transpose(jnp.broadcast_to(row, (8, row.shape[1])))[:, 0:1]


def _gla_factor_operands(qs_ref, k_ref, b_ref):
    b = b_ref[...]
    e_pos = jnp.exp(b)
    e_last = e_pos[SEQ_TILE - 1:SEQ_TILE, :]
    k_neg = k_ref[...] * jnp.exp(-b)
    qd = (qs_ref[...] * e_pos).astype(BF16)
    kn = k_neg.astype(BF16)
    kd = k_neg * e_last
    return qd, kn, kd, e_last


def _gla_tile_factorised(operands, v_ref, o_ref, state_ref, new_state_ref, *, hdk, hdv, fillers):
    ts = SEQ_TILE
    qd, kn, kd, e_last = operands
    heads = range(GLA_HEADS)
    ks = [slice(h * hdk, (h + 1) * hdk) for h in heads]
    vs = [slice(h * hdv, (h + 1) * hdv) for h in heads]
    causal = (lax.broadcasted_iota(jnp.int32, (ts, ts), 0)
              >= lax.broadcasted_iota(jnp.int32, (ts, ts), 1))
    masked = []
    for h in heads:
        scores = lax.dot_general(qd[:, ks[h]], kn[:, ks[h]], (((1,), (1,)), ((), ())),
                                 preferred_element_type=F32)
        fillers[h]()
        masked.append(jnp.where(causal, scores, 0.0).astype(BF16))
    for h in heads:
        kd_t = jnp.transpose(kd[:, ks[h]]).astype(BF16)
        new_state_ref[h] = state_ref[h] * _row_to_col(e_last[:, ks[h]]) + _dot(kd_t, v_ref[:, vs[h]])
    for h in heads:
        o_ref[:, vs[h]] = _dot(qd[:, ks[h]], state_ref[h].astype(BF16)) + _dot(masked[h], v_ref[:, vs[h]])


def _gla_tile_direct(qs_ref, k_ref, b_ref, v_ref, o_ref, state_ref, new_state_ref, *, hdk, hdv):
    new_state_ref[...] = state_ref[...]
    row_c = lax.broadcasted_iota(jnp.int32, (GLA_CHUNK, 1), 0)
    lane_c = lax.broadcasted_iota(jnp.int32, (GLA_CHUNK, GLA_CHUNK), 1)

    for c in range(SEQ_TILE // GLA_CHUNK):
        r0 = c * GLA_CHUNK
        rows = pl.ds(r0, GLA_CHUNK)
        qs_c = qs_ref[rows, :]
        b_c = b_ref[rows, :]

        def col_body(j, a_heads):
            kj = k_ref[pl.ds(r0 + j, 1), :]
            bj = b_ref[pl.ds(r0 + j, 1), :]
            p = qs_c * kj * jnp.exp(jnp.minimum(b_c - bj, 0.0))
            p = jnp.where(row_c >= j, p, 0.0)
            new = []
            for h in range(GLA_HEADS):
                s = jnp.sum(p[:, h * hdk:(h + 1) * hdk], axis=-1, keepdims=True)
                new.append(jnp.where(lane_c == j, s, a_heads[h]))
            return tuple(new)

        a_heads = lax.fori_loop(
            0, GLA_CHUNK, col_body,
            tuple(jnp.zeros((GLA_CHUNK, GLA_CHUNK), F32) for _ in range(GLA_HEADS)))

        b_last = b_ref[pl.ds(r0 + GLA_CHUNK - 1, 1), :]
        qd = (qs_c * jnp.exp(b_c)).astype(BF16)
        kd = k_ref[rows, :] * jnp.exp(b_last - b_c)
        e_last = jnp.exp(b_last)
        for h in range(GLA_HEADS):
            ks = slice(h * hdk, (h + 1) * hdk)
            vs = slice(h * hdv, (h + 1) * hdv)
            v_ch = v_ref[rows, vs]
            st = new_state_ref[h]
            o_ref[rows, vs] = (_dot(qd[:, ks], st.astype(BF16))
                               + _dot(a_heads[h].astype(BF16), v_ch))
            kd_t = jnp.transpose(kd[:, ks]).astype(BF16)
            new_state_ref[h] = st * _row_to_col(e_last[:, ks]) + _dot(kd_t, v_ch)


def _load_weights(wint_hbm, wa_hbm, wb_hbm, wo_hbm, poolw_hbm,
                  wbig_ref, wal_ref, wa_ref, wb_ref, wo_ref, poolw_ref,
                  stage_ref, gate_stage_ref, pool_stage_ref, sems, *, n_before_gate, rank):
    rows = WEIGHT_LOAD_ROWS
    d_in, d_model = wint_hbm.shape
    assert n_before_gate % rows == 0 and (d_in - rank) % rows == 0 and d_model % rows == 0
    jobs = []
    for j in range((d_in - rank) // rows):
        row0 = j * rows if j * rows < n_before_gate else j * rows + rank
        jobs.append((wint_hbm, row0, wbig_ref.at[:, j * rows:(j + 1) * rows]))
    for src, dst in ((wa_hbm, wa_ref), (wb_hbm, wb_ref), (wo_hbm, wo_ref)):
        for r0 in range(0, d_model, rows):
            jobs.append((src, r0, dst.at[r0:r0 + rows, :]))

    def staged_copy(i):
        src, row0, _ = jobs[i]
        return pltpu.make_async_copy(src.at[pl.ds(row0, rows), :], stage_ref.at[i % 2], sems.at[i % 2])

    gate_copy = pltpu.make_async_copy(wint_hbm.at[pl.ds(n_before_gate, rank), :], gate_stage_ref, sems.at[2])
    pool_copy = pltpu.make_async_copy(poolw_hbm, pool_stage_ref, sems.at[3])
    gate_copy.start()
    pool_copy.start()
    staged_copy(0).start()
    for i, (src, _, dst) in enumerate(jobs):
        if i + 1 < len(jobs):
            staged_copy(i + 1).start()
        staged_copy(i).wait()
        block = stage_ref[i % 2]
        dst[...] = (jnp.transpose(block) if src is wint_hbm else block).astype(BF16)
    gate_copy.wait()
    wal_ref[...] = jnp.transpose(gate_stage_ref[...]).astype(BF16)
    pool_copy.wait()
    poolw_ref[...] = pool_stage_ref[...].astype(BF16)


def _block_kernel(x_ref, wint_hbm, wa_hbm, wb_hbm, wo_hbm, poolw_hbm,
                  wgu_ref, bg_ref, gnw_ref, poolb_ref, pools_ref, lnw_ref, lnb_ref,
                  tri_ref, band_ref, bandlb_ref,
                  out_ref,
                  qs_ref, k_ref, b_ref, v_ref, o_ref, state_ref, new_state_ref, ulast_ref,
                  ya_scale_ref, gate_a_ref, merged_b_ref, resid_ref, la_ref, factorisable_ref,
                  wbig_ref, wal_ref, wa_ref, wb_ref, wo_ref, poolw_ref,
                  stage_ref, gate_stage_ref, pool_stage_ref, load_sems,
                  *, d_model, dk, dv, alpha, tiles_per_seq, n_tiles):
    ts = SEQ_TILE
    hdk = dk // GLA_HEADS
    hdv = dv // GLA_HEADS
    s = pl.program_id(0)
    t = lax.rem(jnp.minimum(s, n_tiles - 1), tiles_per_seq)

    @pl.when(s == 0)
    def _():
        _load_weights(wint_hbm, wa_hbm, wb_hbm, wo_hbm, poolw_hbm,
                      wbig_ref, wal_ref, wa_ref, wb_ref, wo_ref, poolw_ref,
                      stage_ref, gate_stage_ref, pool_stage_ref, load_sems,
                      n_before_gate=2 * dk + 2 * dv, rank=GLA_GATE_RANK)

    @pl.when(t == 0)
    def _():
        state_ref[...] = jnp.zeros_like(state_ref)
        ulast_ref[...] = jnp.zeros_like(ulast_ref)

    def step(do_head, do_tail):
        if do_head:
            xf = x_ref[...]
            xb = xf.astype(BF16)

            def proj(n):
                return _dot(xb, wbig_ref[:, n * d_model:(n + 1) * d_model])

        if do_tail:
            o = o_ref[...]
            y_a_parts = []
            for h in range(GLA_HEADS):
                oh = o[:, h * hdv:(h + 1) * hdv]
                y_a_parts.append(oh * lax.rsqrt(jnp.mean(oh * oh, axis=-1, keepdims=True) + NORM_EPS))
            y_a = (jnp.concatenate(y_a_parts, axis=-1) * ya_scale_ref[...]).astype(BF16)

        if do_head:
            a_low = _dot(xb, wal_ref[...])
            qk = proj(_QK)
            qs_ref[...] = qk[:, :dk] * (hdk ** -0.5)
            k_ref[...] = qk[:, dk:]
            gate_pre = _dot(a_low.astype(BF16), wgu_ref[...].astype(BF16)) + bg_ref[...]
            log_a = _log_sigmoid(gate_pre) * (1.0 / GLA_GATE_TAU)
            la_ref[...] = log_a
            la_hi, la_lo = _hi_lo(log_a)

        if do_tail:
            merged = (gate_a_ref[...] * _dot(y_a, wa_ref[...]) + merged_b_ref[...]).astype(BF16)

        if do_head:
            b_ref[...] = _dot(tri_ref[...], la_hi) + _dot(tri_ref[...], la_lo)
            factorisable = jnp.min(b_ref[ts - 1:ts, :]) >= -GLA_FACTOR_MAX_DECAY
            factorisable_ref[0] = factorisable.astype(jnp.int32)
            v_ref[...] = proj(_V).astype(BF16)
            gla_operands = _gla_factor_operands(qs_ref, k_ref, b_ref)

        if do_tail:
            r = resid_ref[...] + _dot(merged, wo_ref[...])

        if do_head:
            def gate_factor_columns(h):
                def issue():
                    cols = slice(h * hdv, (h + 1) * hdv)
                    g = _dot(xb, wbig_ref[:, _G * d_model + h * hdv:_G * d_model + (h + 1) * hdv])
                    ya_scale_ref[:, cols] = gnw_ref[:, cols] * (g * _sigmoid(g))
                return issue

            _gla_tile_factorised(gla_operands, v_ref, o_ref, state_ref, new_state_ref, hdk=hdk, hdv=hdv,
                                 fillers=[gate_factor_columns(h) for h in range(GLA_HEADS)])

        if do_tail:
            mu = jnp.mean(r, axis=-1, keepdims=True)
            rc = r - mu
            var = jnp.mean(rc * rc, axis=-1, keepdims=True)
            out_ref[...] = (rc * lax.rsqrt(var + NORM_EPS) * lnw_ref[...] + lnb_ref[...]).astype(out_ref.dtype)

        if do_head:
            gate_a_ref[...] = _sigmoid(proj(_GATE_A))
            u = proj(_U)
            ub = u.astype(BF16)
            t_abs = t * ts + lax.broadcasted_iota(jnp.int32, (ts, 1), 0)
            gdim = d_model // len(POOL_WINDOWS)
            gslices = [slice(gi * gdim, (gi + 1) * gdim) for gi in range(len(POOL_WINDOWS))]
            win_sums = []
            for gi, gs in enumerate(gslices):
                in_tile = _dot(band_ref[gi], ub[:, gs])
                carried = _dot(bandlb_ref[gi], ulast_ref[:, gs])
                win_sums.append(jnp.concatenate(
                    [in_tile[:POOL_LOOKBACK] + carried, in_tile[POOL_LOOKBACK:]], axis=0))
            ulast_ref[...] = ub[ts - POOL_LOOKBACK:, :]
            z = proj(_Z)
            p_parts = []
            for gi, w in enumerate(POOL_WINDOWS):
                count = jnp.minimum(t_abs + 1, w).astype(F32)
                pg = win_sums[gi] / count - u[:, gslices[gi]]
                p_parts.append(_dot(pg.astype(BF16), poolw_ref[gslices[gi], :]))
            gate_b = _sigmoid(proj(_GATE_B))
            y_b = (jnp.concatenate(p_parts, axis=-1) + poolb_ref[...]) * pools_ref[...] * (z * _sigmoid(z))
            resid_ref[...] = alpha * xf
            merged_b_ref[...] = gate_b * _dot(y_b.astype(BF16), wb_ref[...])

    pl.when(s == 0)(functools.partial(step, True, False))
    pl.when(jnp.logical_and(s > 0, s < n_tiles))(functools.partial(step, True, True))
    pl.when(s == n_tiles)(functools.partial(step, False, True))

    @pl.when(s < n_tiles)
    def _():
        @pl.when(factorisable_ref[0] == 0)
        def _():
            row = lax.broadcasted_iota(jnp.int32, (ts, ts), 0)
            col = lax.broadcasted_iota(jnp.int32, (ts, ts), 1)
            tri_chunk = jnp.where(row // GLA_CHUNK == col // GLA_CHUNK, tri_ref[...], jnp.zeros((), BF16))
            la_hi, la_lo = _hi_lo(la_ref[...])
            b_ref[...] = _dot(tri_chunk, la_hi) + _dot(tri_chunk, la_lo)
            _gla_tile_direct(qs_ref, k_ref, b_ref, v_ref, o_ref, state_ref, new_state_ref, hdk=hdk, hdv=hdv)

        state_ref[...] = new_state_ref[...]


def _const_spec(shape):
    nd = len(shape)
    return pl.BlockSpec(shape, lambda s: (0,) * nd, pipeline_mode=pl.Buffered(1))


def _tri_matrix():
    r = np.arange(SEQ_TILE)
    return jnp.asarray((r[None, :] <= r[:, None]).astype(np.float32), dtype=BF16)


def _band_matrices():
    r = np.arange(SEQ_TILE)[:, None]
    c = np.arange(SEQ_TILE)[None, :]
    c_lb = np.arange(POOL_LOOKBACK)[None, :] - POOL_LOOKBACK
    in_win = lambda d, w: ((d >= 0) & (d < w)).astype(np.float32)
    bands = np.stack([in_win(r - c, w) for w in POOL_WINDOWS])
    bands_lb = np.stack([in_win(r[:POOL_LOOKBACK] - c_lb, w) for w in POOL_WINDOWS])
    return jnp.asarray(bands, dtype=BF16), jnp.asarray(bands_lb, dtype=BF16)


def _layer(x, w_in_t, w_gate_up, b_gate, gn_w, pool_w, pool_b, pool_scale, w_a, w_b, w_o, ln_w, ln_b,
           *, alpha):
    bsz, seq, d_model = x.shape
    rank, dk = w_gate_up.shape
    dv = gn_w.shape[0]
    assert seq % SEQ_TILE == 0 and SEQ_TILE % GLA_CHUNK == 0
    assert dv == d_model and 2 * dk == d_model and rank == GLA_GATE_RANK
    assert pool_w.shape[0] == len(POOL_WINDOWS)
    d_in = w_in_t.shape[0]
    gdim = pool_w.shape[1]
    row = lambda a: a.reshape(1, -1)
    hbm_weights = (w_in_t, w_a, w_b, w_o, pool_w.reshape(d_model, gdim))
    small = (w_gate_up, row(b_gate), row(gn_w), row(pool_b), row(pool_scale), row(ln_w), row(ln_b),
             _tri_matrix(), *_band_matrices())
    operands = (x, *hbm_weights, *small)
    tiles_per_seq = seq // SEQ_TILE
    n_tiles = bsz * tiles_per_seq

    def tile_block(tile):
        return (tile // tiles_per_seq, tile % tiles_per_seq, 0)

    in_specs = [pl.BlockSpec((None, SEQ_TILE, d_model), lambda s: tile_block(jnp.minimum(s, n_tiles - 1)))]
    in_specs += [pl.BlockSpec(memory_space=pl.ANY) for _ in hbm_weights]
    in_specs += [_const_spec(a.shape) for a in small]
    kernel = functools.partial(_block_kernel, d_model=d_model, dk=dk, dv=dv, alpha=alpha,
                               tiles_per_seq=tiles_per_seq, n_tiles=n_tiles)
    state_shape = (GLA_HEADS, dk // GLA_HEADS, dv // GLA_HEADS)
    return pl.pallas_call(
        kernel,
        grid=(n_tiles + 1,),
        in_specs=in_specs,
        out_specs=pl.BlockSpec((None, SEQ_TILE, d_model), lambda s: tile_block(jnp.maximum(s - 1, 0))),
        out_shape=jax.ShapeDtypeStruct(x.shape, x.dtype),
        scratch_shapes=[
            pltpu.VMEM((SEQ_TILE, dk), F32),
            pltpu.VMEM((SEQ_TILE, dk), F32),
            pltpu.VMEM((SEQ_TILE, dk), F32),
            pltpu.VMEM((SEQ_TILE, dv), BF16),
            pltpu.VMEM((SEQ_TILE, dv), F32),
            pltpu.VMEM(state_shape, F32),
            pltpu.VMEM(state_shape, F32),
            pltpu.VMEM((POOL_LOOKBACK, d_model), BF16),
            pltpu.VMEM((SEQ_TILE, dv), F32),
            pltpu.VMEM((SEQ_TILE, d_model), F32),
            pltpu.VMEM((SEQ_TILE, d_model), F32),
            pltpu.VMEM((SEQ_TILE, d_model), F32),
            pltpu.VMEM((SEQ_TILE, dk), F32),
            pltpu.SMEM((1,), jnp.int32),
            pltpu.VMEM((d_model, d_in - rank), BF16),
            pltpu.VMEM((d_model, rank), BF16),
            pltpu.VMEM(w_a.shape, BF16),
            pltpu.VMEM(w_b.shape, BF16),
            pltpu.VMEM(w_o.shape, BF16),
            pltpu.VMEM((d_model, gdim), BF16),
            pltpu.VMEM((2, WEIGHT_LOAD_ROWS, d_model), F32),
            pltpu.VMEM((rank, d_model), F32),
            pltpu.VMEM((d_model, gdim), F32),
            pltpu.SemaphoreType.DMA((4,)),
        ],
        compiler_params=pltpu.CompilerParams(
            dimension_semantics=("arbitrary",),
            vmem_limit_bytes=VMEM_LIMIT_BYTES),
        name="hybrid_gla_pool_layer",
    )(*operands)


def kernel(x, w_in, w_gate_up, b_gate, gn_w, pool_w, pool_b, pool_scale, w_a, w_b, w_o, ln_w, ln_b):
    depth = w_in.shape[0]
    alpha = (2.0 * depth) ** 0.25
    w_in_t = jnp.swapaxes(w_in, 1, 2)
    for l in range(depth):
        x = _layer(x, w_in_t[l], w_gate_up[l], b_gate[l], gn_w[l], pool_w[l], pool_b[l], pool_scale[l],
                   w_a[l], w_b[l], w_o[l], ln_w[l], ln_b[l], alpha=alpha)
    return x
```

```python
import functools

import numpy as np
import jax
import jax.numpy as jnp
from jax import lax
from jax.experimental import pallas as pl
from jax.experimental.pallas import tpu as pltpu

GLA_HEADS = 4
GLA_GATE_RANK = 16
GLA_GATE_TAU = 16.0
POOL_WINDOWS = (2, 4, 8, 16)
POOL_LOOKBACK = 16
NORM_EPS = 1e-5

SEQ_TILE = 512
GLA_FAST_CHUNK = 256
GLA_CHUNK = 64
WEIGHT_LOAD_ROWS = 512
VMEM_LIMIT_BYTES = 60 * 1024 * 1024
GLA_FACTOR_MAX_DECAY = 50.0

_QK, _V, _G, _U, _Z, _GATE_A, _GATE_B = range(7)

F32 = jnp.float32
BF16 = jnp.bfloat16


def _dot(a, b):
    return jnp.dot(a, b, preferred_element_type=F32)


def _sigmoid(x):
    return 1.0 / (1.0 + jnp.exp(-x))


def _log_sigmoid(x):
    return jnp.minimum(x, 0.0) - jnp.log(1.0 + jnp.exp(-jnp.abs(x)))


def _hi_lo(x):
    hi = x.astype(BF16)
    return hi, (x - hi.astype(F32)).astype(BF16)


def _row_to_col(row):
    return jnp.transpose(jnp.broadcast_to(row, (8, row.shape[1])))[:, 0:1]


def _gla_factor_operands(qs_ref, k_ref, b_ref, rows):
    b = b_ref[rows, :]
    e_pos = jnp.exp(b)
    e_last = e_pos[GLA_FAST_CHUNK - 1:GLA_FAST_CHUNK, :]
    k_neg = k_ref[rows, :] * jnp.exp(-b)
    qd = (qs_ref[rows, :] * e_pos).astype(BF16)
    kn = k_neg.astype(BF16)
    kd = k_neg * e_last
    return qd, kn, kd, e_last


def _gla_chunk_factorised(operands, v_ref, o_ref, rows, state_ref, new_state_ref, *, hdk, hdv, fillers):
    ts = GLA_FAST_CHUNK
    qd, kn, kd, e_last = operands
    heads = range(GLA_HEADS)
    ks = [slice(h * hdk, (h + 1) * hdk) for h in heads]
    vs = [slice(h * hdv, (h + 1) * hdv) for h in heads]
    causal = (lax.broadcasted_iota(jnp.int32, (ts, ts), 0)
              >= lax.broadcasted_iota(jnp.int32, (ts, ts), 1))
    masked = []
    for h in heads:
        scores = lax.dot_general(qd[:, ks[h]], kn[:, ks[h]], (((1,), (1,)), ((), ())),
                                 preferred_element_type=F32)
        fillers[h]()
        masked.append(jnp.where(causal, scores, 0.0).astype(BF16))
    for h in heads:
        kd_t = jnp.transpose(kd[:, ks[h]]).astype(BF16)
        new_state_ref[h] = state_ref[h] * _row_to_col(e_last[:, ks[h]]) + _dot(kd_t, v_ref[rows, vs[h]])
    for h in heads:
        o_ref[rows, vs[h]] = (_dot(qd[:, ks[h]], state_ref[h].astype(BF16))
                              + _dot(masked[h], v_ref[rows, vs[h]]))


def _gla_tile_direct(qs_ref, k_ref, b_ref, v_ref, o_ref, state_ref, new_state_ref, *, hdk, hdv):
    new_state_ref[...] = state_ref[...]
    row_c = lax.broadcasted_iota(jnp.int32, (GLA_CHUNK, 1), 0)
    lane_c = lax.broadcasted_iota(jnp.int32, (GLA_CHUNK, GLA_CHUNK), 1)

    for c in range(SEQ_TILE // GLA_CHUNK):
        r0 = c * GLA_CHUNK
        rows = pl.ds(r0, GLA_CHUNK)
        qs_c = qs_ref[rows, :]
        b_c = b_ref[rows, :]

        def col_body(j, a_heads):
            kj = k_ref[pl.ds(r0 + j, 1), :]
            bj = b_ref[pl.ds(r0 + j, 1), :]
            p = qs_c * kj * jnp.exp(jnp.minimum(b_c - bj, 0.0))
            p = jnp.where(row_c >= j, p, 0.0)
            new = []
            for h in range(GLA_HEADS):
                s = jnp.sum(p[:, h * hdk:(h + 1) * hdk], axis=-1, keepdims=True)
                new.append(jnp.where(lane_c == j, s, a_heads[h]))
            return tuple(new)

        a_heads = lax.fori_loop(
            0, GLA_CHUNK, col_body,
            tuple(jnp.zeros((GLA_CHUNK, GLA_CHUNK), F32) for _ in range(GLA_HEADS)))

        b_last = b_ref[pl.ds(r0 + GLA_CHUNK - 1, 1), :]
        qd = (qs_c * jnp.exp(b_c)).astype(BF16)
        kd = k_ref[rows, :] * jnp.exp(b_last - b_c)
        e_last = jnp.exp(b_last)
        for h in range(GLA_HEADS):
            ks = slice(h * hdk, (h + 1) * hdk)
            vs = slice(h * hdv, (h + 1) * hdv)
            v_ch = v_ref[rows, vs]
            st = new_state_ref[h]
            o_ref[rows, vs] = (_dot(qd[:, ks], st.astype(BF16))
                               + _dot(a_heads[h].astype(BF16), v_ch))
            kd_t = jnp.transpose(kd[:, ks]).astype(BF16)
            new_state_ref[h] = st * _row_to_col(e_last[:, ks]) + _dot(kd_t, v_ch)


def _load_weights(wint_hbm, wa_hbm, wb_hbm, wo_hbm, poolw_hbm,
                  wbig_ref, wal_ref, wa_ref, wb_ref, wo_ref, poolw_ref,
                  stage_ref, gate_stage_ref, pool_stage_ref, sems, *, n_before_gate, rank):
    rows = WEIGHT_LOAD_ROWS
    d_in, d_model = wint_hbm.shape
    assert n_before_gate % rows == 0 and (d_in - rank) % rows == 0 and d_model % rows == 0
    jobs = []
    for j in range((d_in - rank) // rows):
        row0 = j * rows if j * rows < n_before_gate else j * rows + rank
        jobs.append((wint_hbm, row0, wbig_ref.at[:, j * rows:(j + 1) * rows]))
    for src, dst in ((wa_hbm, wa_ref), (wb_hbm, wb_ref), (wo_hbm, wo_ref)):
        for r0 in range(0, d_model, rows):
            jobs.append((src, r0, dst.at[r0:r0 + rows, :]))

    def staged_copy(i):
        src, row0, _ = jobs[i]
        return pltpu.make_async_copy(src.at[pl.ds(row0, rows), :], stage_ref.at[i % 2], sems.at[i % 2])

    gate_copy = pltpu.make_async_copy(wint_hbm.at[pl.ds(n_before_gate, rank), :], gate_stage_ref, sems.at[2])
    pool_copy = pltpu.make_async_copy(poolw_hbm, pool_stage_ref, sems.at[3])
    gate_copy.start()
    pool_copy.start()
    staged_copy(0).start()
    for i, (src, _, dst) in enumerate(jobs):
        if i + 1 < len(jobs):
            staged_copy(i + 1).start()
        staged_copy(i).wait()
        block = stage_ref[i % 2]
        dst[...] = (jnp.transpose(block) if src is wint_hbm else block).astype(BF16)
    gate_copy.wait()
    wal_ref[...] = jnp.transpose(gate_stage_ref[...]).astype(BF16)
    pool_copy.wait()
    poolw_ref[...] = pool_stage_ref[...].astype(BF16)


def _block_kernel(x_ref, wint_hbm, wa_hbm, wb_hbm, wo_hbm, poolw_hbm,
                  wgu_ref, bg_ref, gnw_ref, poolb_ref, pools_ref, lnw_ref, lnb_ref,
                  tri_ref, band_ref, bandlb_ref,
                  out_ref,
                  qs_ref, k_ref, b_ref, v_ref, o_ref, state_ref, mid_state_ref, new_state_ref, ulast_ref,
                  ya_scale_ref, gate_a_ref, merged_b_ref, resid_ref, la_ref, factorisable_ref,
                  wbig_ref, wal_ref, wa_ref, wb_ref, wo_ref, poolw_ref,
                  stage_ref, gate_stage_ref, pool_stage_ref, load_sems,
                  *, d_model, dk, dv, alpha, tiles_per_seq, n_tiles):
    ts = SEQ_TILE
    hdk = dk // GLA_HEADS
    hdv = dv // GLA_HEADS
    s = pl.program_id(0)
    t = lax.rem(jnp.minimum(s, n_tiles - 1), tiles_per_seq)
    fast_chunks = [slice(r0, r0 + GLA_FAST_CHUNK) for r0 in range(0, ts, GLA_FAST_CHUNK)]
    assert len(fast_chunks) == 2

    @pl.when(s == 0)
    def _():
        _load_weights(wint_hbm, wa_hbm, wb_hbm, wo_hbm, poolw_hbm,
                      wbig_ref, wal_ref, wa_ref, wb_ref, wo_ref, poolw_ref,
                      stage_ref, gate_stage_ref, pool_stage_ref, load_sems,
                      n_before_gate=2 * dk + 2 * dv, rank=GLA_GATE_RANK)

    @pl.when(t == 0)
    def _():
        state_ref[...] = jnp.zeros_like(state_ref)
        ulast_ref[...] = jnp.zeros_like(ulast_ref)

    def step(do_head, do_tail):
        if do_head:
            xf = x_ref[...]
            xb = xf.astype(BF16)

            def proj(n):
                return _dot(xb, wbig_ref[:, n * d_model:(n + 1) * d_model])

        if do_tail:
            o = o_ref[...]
            y_a_parts = []
            for h in range(GLA_HEADS):
                oh = o[:, h * hdv:(h + 1) * hdv]
                y_a_parts.append(oh * lax.rsqrt(jnp.mean(oh * oh, axis=-1, keepdims=True) + NORM_EPS))
            y_a = (jnp.concatenate(y_a_parts, axis=-1) * ya_scale_ref[...]).astype(BF16)

        if do_head:
            a_low = _dot(xb, wal_ref[...])
            qk = proj(_QK)
            qs_ref[...] = qk[:, :dk] * (hdk ** -0.5)
            k_ref[...] = qk[:, dk:]
            gate_pre = _dot(a_low.astype(BF16), wgu_ref[...].astype(BF16)) + bg_ref[...]
            log_a = _log_sigmoid(gate_pre) * (1.0 / GLA_GATE_TAU)
            la_ref[...] = log_a
            la_hi, la_lo = _hi_lo(log_a)

        if do_tail:
            merged = (gate_a_ref[...] * _dot(y_a, wa_ref[...]) + merged_b_ref[...]).astype(BF16)

        if do_head:
            for rows in fast_chunks:
                b_ref[rows, :] = _dot(tri_ref[...], la_hi[rows, :]) + _dot(tri_ref[...], la_lo[rows, :])
            min_b = functools.reduce(
                jnp.minimum, [jnp.min(b_ref[rows.stop - 1:rows.stop, :]) for rows in fast_chunks])
            factorisable_ref[0] = (min_b >= -GLA_FACTOR_MAX_DECAY).astype(jnp.int32)
            v_ref[...] = proj(_V).astype(BF16)
            gla_operands = [_gla_factor_operands(qs_ref, k_ref, b_ref, rows) for rows in fast_chunks]

        if do_tail:
            r = resid_ref[...] + _dot(merged, wo_ref[...])

        if do_head:
            def gate_factor_columns(h):
                def issue():
                    cols = slice(h * hdv, (h + 1) * hdv)
                    g = _dot(xb, wbig_ref[:, _G * d_model + h * hdv:_G * d_model + (h + 1) * hdv])
                    ya_scale_ref[:, cols] = gnw_ref[:, cols] * (g * _sigmoid(g))
                return issue

            def merge_gate_columns(h):
                def issue():
                    cols = slice(_GATE_A * d_model + h * hdv, _GATE_A * d_model + (h + 1) * hdv)
                    gate_a_ref[:, h * hdv:(h + 1) * hdv] = _sigmoid(_dot(xb, wbig_ref[:, cols]))
                return issue

            chunk_states = (state_ref, mid_state_ref, new_state_ref)
            chunk_fillers = (gate_factor_columns, merge_gate_columns)
            for c, rows in enumerate(fast_chunks):
                _gla_chunk_factorised(gla_operands[c], v_ref, o_ref, rows, chunk_states[c], chunk_states[c + 1],
                                      hdk=hdk, hdv=hdv,
                                      fillers=[chunk_fillers[c](h) for h in range(GLA_HEADS)])

        if do_tail:
            mu = jnp.mean(r, axis=-1, keepdims=True)
            rc = r - mu
            var = jnp.mean(rc * rc, axis=-1, keepdims=True)
            out_ref[...] = (rc * lax.rsqrt(var + NORM_EPS) * lnw_ref[...] + lnb_ref[...]).astype(out_ref.dtype)

        if do_head:
            u = proj(_U)
            ub = u.astype(BF16)
            t_abs = t * ts + lax.broadcasted_iota(jnp.int32, (ts, 1), 0)
            gdim = d_model // len(POOL_WINDOWS)
            gslices = [slice(gi * gdim, (gi + 1) * gdim) for gi in range(len(POOL_WINDOWS))]
            win_sums = []
            for gi, gs in enumerate(gslices):
                parts = []
                for rows in fast_chunks:
                    look_back = (ulast_ref[:, gs] if rows.start == 0
                                 else ub[rows.start - POOL_LOOKBACK:rows.start, gs])
                    in_chunk = _dot(band_ref[gi], ub[rows, gs])
                    carried = _dot(bandlb_ref[gi], look_back)
                    parts += [in_chunk[:POOL_LOOKBACK] + carried, in_chunk[POOL_LOOKBACK:]]
                win_sums.append(jnp.concatenate(parts, axis=0))
            ulast_ref[...] = ub[ts - POOL_LOOKBACK:, :]
            z = proj(_Z)
            p_parts = []
            for gi, w in enumerate(POOL_WINDOWS):
                count = jnp.minimum(t_abs + 1, w).astype(F32)
                pg = win_sums[gi] / count - u[:, gslices[gi]]
                p_parts.append(_dot(pg.astype(BF16), poolw_ref[gslices[gi], :]))
            gate_b = _sigmoid(proj(_GATE_B))
            y_b = (jnp.concatenate(p_parts, axis=-1) + poolb_ref[...]) * pools_ref[...] * (z * _sigmoid(z))
            resid_ref[...] = alpha * xf
            merged_b_ref[...] = gate_b * _dot(y_b.astype(BF16), wb_ref[...])

    pl.when(s == 0)(functools.partial(step, True, False))
    pl.when(jnp.logical_and(s > 0, s < n_tiles))(functools.partial(step, True, True))
    pl.when(s == n_tiles)(functools.partial(step, False, True))

    @pl.when(s < n_tiles)
    def _():
        @pl.when(factorisable_ref[0] == 0)
        def _():
            span = GLA_FAST_CHUNK
            row = lax.broadcasted_iota(jnp.int32, (span, span), 0)
            col = lax.broadcasted_iota(jnp.int32, (span, span), 1)
            tri_chunk = jnp.where(row // GLA_CHUNK == col // GLA_CHUNK, tri_ref[...], jnp.zeros((), BF16))
            la_hi, la_lo = _hi_lo(la_ref[...])
            for rows in fast_chunks:
                b_ref[rows, :] = _dot(tri_chunk, la_hi[rows, :]) + _dot(tri_chunk, la_lo[rows, :])
            _gla_tile_direct(qs_ref, k_ref, b_ref, v_ref, o_ref, state_ref, new_state_ref, hdk=hdk, hdv=hdv)

        state_ref[...] = new_state_ref[...]


def _const_spec(shape):
    nd = len(shape)
    return pl.BlockSpec(shape, lambda s: (0,) * nd, pipeline_mode=pl.Buffered(1))


def _tri_matrix():
    r = np.arange(GLA_FAST_CHUNK)
    return jnp.asarray((r[None, :] <= r[:, None]).astype(np.float32), dtype=BF16)


def _band_matrices():
    r = np.arange(GLA_FAST_CHUNK)[:, None]
    c = np.arange(GLA_FAST_CHUNK)[None, :]
    c_lb = np.arange(POOL_LOOKBACK)[None, :] - POOL_LOOKBACK
    in_win = lambda d, w: ((d >= 0) & (d < w)).astype(np.float32)
    bands = np.stack([in_win(r - c, w) for w in POOL_WINDOWS])
    bands_lb = np.stack([in_win(r[:POOL_LOOKBACK] - c_lb, w) for w in POOL_WINDOWS])
    return jnp.asarray(bands, dtype=BF16), jnp.asarray(bands_lb, dtype=BF16)


def _layer(x, w_in_t, w_gate_up, b_gate, gn_w, pool_w, pool_b, pool_scale, w_a, w_b, w_o, ln_w, ln_b,
           *, alpha):
    bsz, seq, d_model = x.shape
    rank, dk = w_gate_up.shape
    dv = gn_w.shape[0]
    assert seq % SEQ_TILE == 0 and SEQ_TILE % GLA_FAST_CHUNK == 0 and GLA_FAST_CHUNK % GLA_CHUNK == 0
    assert dv == d_model and 2 * dk == d_model and rank == GLA_GATE_RANK
    assert pool_w.shape[0] == len(POOL_WINDOWS)
    d_in = w_in_t.shape[0]
    gdim = pool_w.shape[1]
    row = lambda a: a.reshape(1, -1)
    hbm_weights = (w_in_t, w_a, w_b, w_o, pool_w.reshape(d_model, gdim))
    small = (w_gate_up, row(b_gate), row(gn_w), row(pool_b), row(pool_scale), row(ln_w), row(ln_b),
             _tri_matrix(), *_band_matrices())
    operands = (x, *hbm_weights, *small)
    tiles_per_seq = seq // SEQ_TILE
    n_tiles = bsz * tiles_per_seq

    def tile_block(tile):
        return (tile // tiles_per_seq, tile % tiles_per_seq, 0)

    in_specs = [pl.BlockSpec((None, SEQ_TILE, d_model), lambda s: tile_block(jnp.minimum(s, n_tiles - 1)))]
    in_specs += [pl.BlockSpec(memory_space=pl.ANY) for _ in hbm_weights]
    in_specs += [_const_spec(a.shape) for a in small]
    kernel = functools.partial(_block_kernel, d_model=d_model, dk=dk, dv=dv, alpha=alpha,
                               tiles_per_seq=tiles_per_seq, n_tiles=n_tiles)
    state_shape = (GLA_HEADS, dk // GLA_HEADS, dv // GLA_HEADS)
    return pl.pallas_call(
        kernel,
        grid=(n_tiles + 1,),
        in_specs=in_specs,
        out_specs=pl.BlockSpec((None, SEQ_TILE, d_model), lambda s: tile_block(jnp.maximum(s - 1, 0))),
        out_shape=jax.ShapeDtypeStruct(x.shape, x.dtype),
        scratch_shapes=[
            pltpu.VMEM((SEQ_TILE, dk), F32),
            pltpu.VMEM((SEQ_TILE, dk), F32),
            pltpu.VMEM((SEQ_TILE, dk), F32),
            pltpu.VMEM((SEQ_TILE, dv), BF16),
            pltpu.VMEM((SEQ_TILE, dv), F32),
            pltpu.VMEM(state_shape, F32),
            pltpu.VMEM(state_shape, F32),
            pltpu.VMEM(state_shape, F32),
            pltpu.VMEM((POOL_LOOKBACK, d_model), BF16),
            pltpu.VMEM((SEQ_TILE, dv), F32),
            pltpu.VMEM((SEQ_TILE, d_model), F32),
            pltpu.VMEM((SEQ_TILE, d_model), F32),
            pltpu.VMEM((SEQ_TILE, d_model), F32),
            pltpu.VMEM((SEQ_TILE, dk), F32),
            pltpu.SMEM((1,), jnp.int32),
            pltpu.VMEM((d_model, d_in - rank), BF16),
            pltpu.VMEM((d_model, rank), BF16),
            pltpu.VMEM(w_a.shape, BF16),
            pltpu.VMEM(w_b.shape, BF16),
            pltpu.VMEM(w_o.shape, BF16),
            pltpu.VMEM((d_model, gdim), BF16),
            pltpu.VMEM((2, WEIGHT_LOAD_ROWS, d_model), F32),
            pltpu.VMEM((rank, d_model), F32),
            pltpu.VMEM((d_model, gdim), F32),
            pltpu.SemaphoreType.DMA((4,)),
        ],
        compiler_params=pltpu.CompilerParams(
            dimension_semantics=("arbitrary",),
            vmem_limit_bytes=VMEM_LIMIT_BYTES),
        name="hybrid_gla_pool_layer",
    )(*operands)


def kernel(x, w_in, w_gate_up, b_gate, gn_w, pool_w, pool_b, pool_scale, w_a, w_b, w_o, ln_w, ln_b):
    depth = w_in.shape[0]
    alpha = (2.0 * depth) ** 0.25
    w_in_t = jnp.swapaxes(w_in, 1, 2)
    for l in range(depth):
        x = _layer(x, w_in_t[l], w_gate_up[l], b_gate[l], gn_w[l], pool_w[l], pool_b[l], pool_scale[l],
                   w_a[l], w_b[l], w_o[l], ln_w[l], ln_b[l], alpha=alpha)
    return x
```

```python
import functools

import numpy as np
import jax
import jax.numpy as jnp
from jax import lax
from jax.experimental import pallas as pl
from jax.experimental.pallas import tpu as pltpu

GLA_HEADS = 4
GLA_GATE_RANK = 16
GLA_GATE_TAU = 16.0
POOL_WINDOWS = (2, 4, 8, 16)
POOL_LOOKBACK = 16
NORM_EPS = 1e-5

SEQ_TILE = 512
GLA_FAST_CHUNK = 256
GLA_CHUNK = 64
WEIGHT_LOAD_ROWS = 256
VMEM_LIMIT_BYTES = 60 * 1024 * 1024
GLA_FACTOR_MAX_DECAY = 50.0

_QK, _V, _G, _U, _Z, _GATE_A, _GATE_B = range(7)

F32 = jnp.float32
BF16 = jnp.bfloat16


def _dot(a, b):
    return jnp.dot(a, b, preferred_element_type=F32)


def _sigmoid(x):
    return 1.0 / (1.0 + jnp.exp(-x))


def _log_sigmoid(x):
    return jnp.minimum(x, 0.0) - jnp.log(1.0 + jnp.exp(-jnp.abs(x)))


def _hi_lo(x):
    hi = x.astype(BF16)
    return hi, (x - hi.astype(F32)).astype(BF16)


def _row_to_col(row):
    return jnp.transpose(jnp.broadcast_to(row, (8, row.shape[1])))[:, 0:1]


def _gla_factor_operands(qs_ref, k_ref, b_ref, rows):
    b = b_ref[rows, :]
    e_pos = jnp.exp(b)
    e_last = e_pos[GLA_FAST_CHUNK - 1:GLA_FAST_CHUNK, :]
    k_neg = k_ref[rows, :] * jnp.exp(-b)
    qd = (qs_ref[rows, :] * e_pos).astype(BF16)
    kn = k_neg.astype(BF16)
    kd = k_neg * e_last
    return qd, kn, kd, e_last


def _gla_chunk_factorised(operands, v_ref, o_ref, rows, state_ref, new_state_ref, *, hdk, hdv, fillers):
    ts = GLA_FAST_CHUNK
    qd, kn, kd, e_last = operands
    heads = range(GLA_HEADS)
    ks = [slice(h * hdk, (h + 1) * hdk) for h in heads]
    vs = [slice(h * hdv, (h + 1) * hdv) for h in heads]
    causal = (lax.broadcasted_iota(jnp.int32, (ts, ts), 0)
              >= lax.broadcasted_iota(jnp.int32, (ts, ts), 1))
    masked = []
    for h in heads:
        scores = lax.dot_general(qd[:, ks[h]], kn[:, ks[h]], (((1,), (1,)), ((), ())),
                                 preferred_element_type=F32)
        fillers[h]()
        masked.append(jnp.where(causal, scores, 0.0).astype(BF16))
    for h in heads:
        kd_t = jnp.transpose(kd[:, ks[h]]).astype(BF16)
        new_state_ref[h] = state_ref[h] * _row_to_col(e_last[:, ks[h]]) + _dot(kd_t, v_ref[rows, vs[h]])
    for h in heads:
        o_ref[rows, vs[h]] = (_dot(qd[:, ks[h]], state_ref[h].astype(BF16))
                              + _dot(masked[h], v_ref[rows, vs[h]]))


def _gla_tile_direct(qs_ref, k_ref, b_ref, v_ref, o_ref, state_ref, new_state_ref, *, hdk, hdv):
    new_state_ref[...] = state_ref[...]
    row_c = lax.broadcasted_iota(jnp.int32, (GLA_CHUNK, 1), 0)
    lane_c = lax.broadcasted_iota(jnp.int32, (GLA_CHUNK, GLA_CHUNK), 1)

    for c in range(SEQ_TILE // GLA_CHUNK):
        r0 = c * GLA_CHUNK
        rows = pl.ds(r0, GLA_CHUNK)
        qs_c = qs_ref[rows, :]
        b_c = b_ref[rows, :]

        def col_body(j, a_heads):
            kj = k_ref[pl.ds(r0 + j, 1), :]
            bj = b_ref[pl.ds(r0 + j, 1), :]
            p = qs_c * kj * jnp.exp(jnp.minimum(b_c - bj, 0.0))
            p = jnp.where(row_c >= j, p, 0.0)
            new = []
            for h in range(GLA_HEADS):
                s = jnp.sum(p[:, h * hdk:(h + 1) * hdk], axis=-1, keepdims=True)
                new.append(jnp.where(lane_c == j, s, a_heads[h]))
            return tuple(new)

        a_heads = lax.fori_loop(
            0, GLA_CHUNK, col_body,
            tuple(jnp.zeros((GLA_CHUNK, GLA_CHUNK), F32) for _ in range(GLA_HEADS)))

        b_last = b_ref[pl.ds(r0 + GLA_CHUNK - 1, 1), :]
        qd = (qs_c * jnp.exp(b_c)).astype(BF16)
        kd = k_ref[rows, :] * jnp.exp(b_last - b_c)
        e_last = jnp.exp(b_last)
        for h in range(GLA_HEADS):
            ks = slice(h * hdk, (h + 1) * hdk)
            vs = slice(h * hdv, (h + 1) * hdv)
            v_ch = v_ref[rows, vs]
            st = new_state_ref[h]
            o_ref[rows, vs] = (_dot(qd[:, ks], st.astype(BF16))
                               + _dot(a_heads[h].astype(BF16), v_ch))
            kd_t = jnp.transpose(kd[:, ks]).astype(BF16)
            new_state_ref[h] = st * _row_to_col(e_last[:, ks]) + _dot(kd_t, v_ch)


class _WeightLoader:
    ORDER = (_QK, _V, _G, _GATE_A, _U, _Z, _GATE_B, "w_b", "w_a", "w_o")

    def __init__(self, wint_hbm, wa_hbm, wb_hbm, wo_hbm, poolw_hbm,
                 wbig_ref, wal_ref, wa_ref, wb_ref, wo_ref, poolw_ref,
                 stage_ref, gate_stage_ref, pool_stage_ref, sems, *, n_before_gate, rank):
        rows = WEIGHT_LOAD_ROWS
        d_in, d_model = wint_hbm.shape
        n_groups = sum(not isinstance(name, str) for name in self.ORDER)
        assert n_before_gate % rows == 0 and d_model % rows == 0 and d_in == n_groups * d_model + rank
        self.rows, self.stage_ref, self.sems, self.wint_hbm = rows, stage_ref, sems, wint_hbm
        others = {"w_a": (wa_hbm, wa_ref), "w_b": (wb_hbm, wb_ref), "w_o": (wo_hbm, wo_ref)}
        self.jobs = []
        self.last_job = {}
        for name in self.ORDER:
            for r0 in range(0, d_model, rows):
                if name in others:
                    src, dst = others[name]
                    self.jobs.append((src, r0, dst.at[r0:r0 + rows, :]))
                else:
                    col0 = name * d_model + r0
                    row0 = col0 if col0 < n_before_gate else col0 + rank
                    self.jobs.append((wint_hbm, row0, wbig_ref.at[:, col0:col0 + rows]))
            self.last_job[name] = len(self.jobs) - 1
        self.done = 0
        self.gate = (pltpu.make_async_copy(wint_hbm.at[pl.ds(n_before_gate, rank), :], gate_stage_ref,
                                           sems.at[2]), gate_stage_ref, wal_ref)
        self.pool = (pltpu.make_async_copy(poolw_hbm, pool_stage_ref, sems.at[3]), pool_stage_ref, poolw_ref)
        self.gate[0].start()
        self.pool[0].start()
        self._staged_copy(0).start()

    def _staged_copy(self, i):
        src, row0, _ = self.jobs[i]
        return pltpu.make_async_copy(src.at[pl.ds(row0, self.rows), :], self.stage_ref.at[i % 2],
                                     self.sems.at[i % 2])

    def need(self, name):
        if name == "w_al":
            copy, stage, dst = self.gate
            copy.wait()
            dst[...] = jnp.transpose(stage[...]).astype(BF16)
        elif name == "pool_w":
            copy, stage, dst = self.pool
            copy.wait()
            dst[...] = stage[...].astype(BF16)
        else:
            while self.done <= self.last_job[name]:
                i = self.done
                if i + 1 < len(self.jobs):
                    self._staged_copy(i + 1).start()
                self._staged_copy(i).wait()
                src, _, dst = self.jobs[i]
                block = self.stage_ref[i % 2]
                dst[...] = (jnp.transpose(block) if src is self.wint_hbm else block).astype(BF16)
                self.done += 1


class _WeightsResident:
    def need(self, name):
        pass


def _block_kernel(x_ref, wint_hbm, wa_hbm, wb_hbm, wo_hbm, poolw_hbm,
                  wgu_ref, bg_ref, gnw_ref, poolb_ref, pools_ref, lnw_ref, lnb_ref,
                  tri_ref, band_ref, bandlb_ref,
                  out_ref,
                  qs_ref, k_ref, b_ref, v_ref, o_ref, state_ref, mid_state_ref, new_state_ref, ulast_ref,
                  ya_scale_ref, gate_a_ref, merged_b_ref, resid_ref, la_ref, factorisable_ref,
                  wbig_ref, wal_ref, wa_ref, wb_ref, wo_ref, poolw_ref,
                  stage_ref, gate_stage_ref, pool_stage_ref, load_sems,
                  *, d_model, dk, dv, alpha, tiles_per_seq, n_tiles):
    ts = SEQ_TILE
    hdk = dk // GLA_HEADS
    hdv = dv // GLA_HEADS
    s = pl.program_id(0)
    t = lax.rem(jnp.minimum(s, n_tiles - 1), tiles_per_seq)
    fast_chunks = [slice(r0, r0 + GLA_FAST_CHUNK) for r0 in range(0, ts, GLA_FAST_CHUNK)]
    assert len(fast_chunks) == 2

    @pl.when(t == 0)
    def _():
        state_ref[...] = jnp.zeros_like(state_ref)
        ulast_ref[...] = jnp.zeros_like(ulast_ref)

    def step(do_head, do_tail):
        if do_head and not do_tail:
            weights = _WeightLoader(wint_hbm, wa_hbm, wb_hbm, wo_hbm, poolw_hbm,
                                    wbig_ref, wal_ref, wa_ref, wb_ref, wo_ref, poolw_ref,
                                    stage_ref, gate_stage_ref, pool_stage_ref, load_sems,
                                    n_before_gate=2 * dk + 2 * dv, rank=GLA_GATE_RANK)
        else:
            weights = _WeightsResident()

        if do_head:
            xf = x_ref[...]
            xb = xf.astype(BF16)

            def proj(n):
                weights.need(n)
                return _dot(xb, wbig_ref[:, n * d_model:(n + 1) * d_model])

        if do_tail:
            o = o_ref[...]
            y_a_parts = []
            for h in range(GLA_HEADS):
                oh = o[:, h * hdv:(h + 1) * hdv]
                y_a_parts.append(oh * lax.rsqrt(jnp.mean(oh * oh, axis=-1, keepdims=True) + NORM_EPS))
            y_a = (jnp.concatenate(y_a_parts, axis=-1) * ya_scale_ref[...]).astype(BF16)

        if do_head:
            weights.need("w_al")
            a_low = _dot(xb, wal_ref[...])
            qk = proj(_QK)
            qs_ref[...] = qk[:, :dk] * (hdk ** -0.5)
            k_ref[...] = qk[:, dk:]
            gate_pre = _dot(a_low.astype(BF16), wgu_ref[...].astype(BF16)) + bg_ref[...]
            log_a = _log_sigmoid(gate_pre) * (1.0 / GLA_GATE_TAU)
            la_ref[...] = log_a
            la_hi, la_lo = _hi_lo(log_a)

        if do_tail:
            merged = (gate_a_ref[...] * _dot(y_a, wa_ref[...]) + merged_b_ref[...]).astype(BF16)

        if do_head:
            for rows in fast_chunks:
                b_ref[rows, :] = _dot(tri_ref[...], la_hi[rows, :]) + _dot(tri_ref[...], la_lo[rows, :])
            min_b = functools.reduce(
                jnp.minimum, [jnp.min(b_ref[rows.stop - 1:rows.stop, :]) for rows in fast_chunks])
            factorisable_ref[0] = (min_b >= -GLA_FACTOR_MAX_DECAY).astype(jnp.int32)
            v_ref[...] = proj(_V).astype(BF16)
            gla_operands = [_gla_factor_operands(qs_ref, k_ref, b_ref, rows) for rows in fast_chunks]

        if do_tail:
            r = resid_ref[...] + _dot(merged, wo_ref[...])

        if do_head:
            def gate_factor_columns(h):
                def issue():
                    cols = slice(h * hdv, (h + 1) * hdv)
                    weights.need(_G)
                    g = _dot(xb, wbig_ref[:, _G * d_model + h * hdv:_G * d_model + (h + 1) * hdv])
                    ya_scale_ref[:, cols] = gnw_ref[:, cols] * (g * _sigmoid(g))
                return issue

            def merge_gate_columns(h):
                def issue():
                    cols = slice(_GATE_A * d_model + h * hdv, _GATE_A * d_model + (h + 1) * hdv)
                    weights.need(_GATE_A)
                    gate_a_ref[:, h * hdv:(h + 1) * hdv] = _sigmoid(_dot(xb, wbig_ref[:, cols]))
                return issue

            chunk_states = (state_ref, mid_state_ref, new_state_ref)
            chunk_fillers = (gate_factor_columns, merge_gate_columns)
            for c, rows in enumerate(fast_chunks):
                _gla_chunk_factorised(gla_operands[c], v_ref, o_ref, rows, chunk_states[c], chunk_states[c + 1],
                                      hdk=hdk, hdv=hdv,
                                      fillers=[chunk_fillers[c](h) for h in range(GLA_HEADS)])

        if do_tail:
            mu = jnp.mean(r, axis=-1, keepdims=True)
            rc = r - mu
            var = jnp.mean(rc * rc, axis=-1, keepdims=True)
            out_ref[...] = (rc * lax.rsqrt(var + NORM_EPS) * lnw_ref[...] + lnb_ref[...]).astype(out_ref.dtype)

        if do_head:
            u = proj(_U)
            ub = u.astype(BF16)
            t_abs = t * ts + lax.broadcasted_iota(jnp.int32, (ts, 1), 0)
            gdim = d_model // len(POOL_WINDOWS)
            gslices = [slice(gi * gdim, (gi + 1) * gdim) for gi in range(len(POOL_WINDOWS))]
            win_sums = []
            for gi, gs in enumerate(gslices):
                parts = []
                for rows in fast_chunks:
                    look_back = (ulast_ref[:, gs] if rows.start == 0
                                 else ub[rows.start - POOL_LOOKBACK:rows.start, gs])
                    in_chunk = _dot(band_ref[gi], ub[rows, gs])
                    carried = _dot(bandlb_ref[gi], look_back)
                    parts += [in_chunk[:POOL_LOOKBACK] + carried, in_chunk[POOL_LOOKBACK:]]
                win_sums.append(jnp.concatenate(parts, axis=0))
            ulast_ref[...] = ub[ts - POOL_LOOKBACK:, :]
            z = proj(_Z)
            weights.need("pool_w")
            p_parts = []
            for gi, w in enumerate(POOL_WINDOWS):
                count = jnp.minimum(t_abs + 1, w).astype(F32)
                pg = win_sums[gi] / count - u[:, gslices[gi]]
                p_parts.append(_dot(pg.astype(BF16), poolw_ref[gslices[gi], :]))
            gate_b = _sigmoid(proj(_GATE_B))
            y_b = (jnp.concatenate(p_parts, axis=-1) + poolb_ref[...]) * pools_ref[...] * (z * _sigmoid(z))
            resid_ref[...] = alpha * xf
            weights.need("w_b")
            merged_b_ref[...] = gate_b * _dot(y_b.astype(BF16), wb_ref[...])
            weights.need("w_o")

    pl.when(s == 0)(functools.partial(step, True, False))
    pl.when(jnp.logical_and(s > 0, s < n_tiles))(functools.partial(step, True, True))
    pl.when(s == n_tiles)(functools.partial(step, False, True))

    @pl.when(s < n_tiles)
    def _():
        @pl.when(factorisable_ref[0] == 0)
        def _():
            span = GLA_FAST_CHUNK
            row = lax.broadcasted_iota(jnp.int32, (span, span), 0)
            col = lax.broadcasted_iota(jnp.int32, (span, span), 1)
            tri_chunk = jnp.where(row // GLA_CHUNK == col // GLA_CHUNK, tri_ref[...], jnp.zeros((), BF16))
            la_hi, la_lo = _hi_lo(la_ref[...])
            for rows in fast_chunks:
                b_ref[rows, :] = _dot(tri_chunk, la_hi[rows, :]) + _dot(tri_chunk, la_lo[rows, :])
            _gla_tile_direct(qs_ref, k_ref, b_ref, v_ref, o_ref, state_ref, new_state_ref, hdk=hdk, hdv=hdv)

        state_ref[...] = new_state_ref[...]


def _const_spec(shape):
    nd = len(shape)
    return pl.BlockSpec(shape, lambda s: (0,) * nd, pipeline_mode=pl.Buffered(1))


def _tri_matrix():
    r = np.arange(GLA_FAST_CHUNK)
    return jnp.asarray((r[None, :] <= r[:, None]).astype(np.float32), dtype=BF16)


def _band_matrices():
    r = np.arange(GLA_FAST_CHUNK)[:, None]
    c = np.arange(GLA_FAST_CHUNK)[None, :]
    c_lb = np.arange(POOL_LOOKBACK)[None, :] - POOL_LOOKBACK
    in_win = lambda d, w: ((d >= 0) & (d < w)).astype(np.float32)
    bands = np.stack([in_win(r - c, w) for w in POOL_WINDOWS])
    bands_lb = np.stack([in_win(r[:POOL_LOOKBACK] - c_lb, w) for w in POOL_WINDOWS])
    return jnp.asarray(bands, dtype=BF16), jnp.asarray(bands_lb, dtype=BF16)


def _layer(x, w_in_t, w_gate_up, b_gate, gn_w, pool_w, pool_b, pool_scale, w_a, w_b, w_o, ln_w, ln_b,
           *, alpha):
    bsz, seq, d_model = x.shape
    rank, dk = w_gate_up.shape
    dv = gn_w.shape[0]
    assert seq % SEQ_TILE == 0 and SEQ_TILE % GLA_FAST_CHUNK == 0 and GLA_FAST_CHUNK % GLA_CHUNK == 0
    assert dv == d_model and 2 * dk == d_model and rank == GLA_GATE_RANK
    assert pool_w.shape[0] == len(POOL_WINDOWS)
    d_in = w_in_t.shape[0]
    gdim = pool_w.shape[1]
    row = lambda a: a.reshape(1, -1)
    hbm_weights = (w_in_t, w_a, w_b, w_o, pool_w.reshape(d_model, gdim))
    small = (w_gate_up, row(b_gate), row(gn_w), row(pool_b), row(pool_scale), row(ln_w), row(ln_b),
             _tri_matrix(), *_band_matrices())
    operands = (x, *hbm_weights, *small)
    tiles_per_seq = seq // SEQ_TILE
    n_tiles = bsz * tiles_per_seq

    def tile_block(tile):
        return (tile // tiles_per_seq, tile % tiles_per_seq, 0)

    in_specs = [pl.BlockSpec((None, SEQ_TILE, d_model), lambda s: tile_block(jnp.minimum(s, n_tiles - 1)))]
    in_specs += [pl.BlockSpec(memory_space=pl.ANY) for _ in hbm_weights]
    in_specs += [_const_spec(a.shape) for a in small]
    kernel = functools.partial(_block_kernel, d_model=d_model, dk=dk, dv=dv, alpha=alpha,
                               tiles_per_seq=tiles_per_seq, n_tiles=n_tiles)
    state_shape = (GLA_HEADS, dk // GLA_HEADS, dv // GLA_HEADS)
    return pl.pallas_call(
        kernel,
        grid=(n_tiles + 1,),
        in_specs=in_specs,
        out_specs=pl.BlockSpec((None, SEQ_TILE, d_model), lambda s: tile_block(jnp.maximum(s - 1, 0))),
        out_shape=jax.ShapeDtypeStruct(x.shape, x.dtype),
        scratch_shapes=[
            pltpu.VMEM((SEQ_TILE, dk), F32),
            pltpu.VMEM((SEQ_TILE, dk), F32),
            pltpu.VMEM((SEQ_TILE, dk), F32),
            pltpu.VMEM((SEQ_TILE, dv), BF16),
            pltpu.VMEM((SEQ_TILE, dv), F32),
            pltpu.VMEM(state_shape, F32),
            pltpu.VMEM(state_shape, F32),
            pltpu.VMEM(state_shape, F32),
            pltpu.VMEM((POOL_LOOKBACK, d_model), BF16),
            pltpu.VMEM((SEQ_TILE, dv), F32),
            pltpu.VMEM((SEQ_TILE, d_model), F32),
            pltpu.VMEM((SEQ_TILE, d_model), F32),
            pltpu.VMEM((SEQ_TILE, d_model), F32),
            pltpu.VMEM((SEQ_TILE, dk), F32),
            pltpu.SMEM((1,), jnp.int32),
            pltpu.VMEM((d_model, d_in - rank), BF16),
            pltpu.VMEM((d_model, rank), BF16),
            pltpu.VMEM(w_a.shape, BF16),
            pltpu.VMEM(w_b.shape, BF16),
            pltpu.VMEM(w_o.shape, BF16),
            pltpu.VMEM((d_model, gdim), BF16),
            pltpu.VMEM((2, WEIGHT_LOAD_ROWS, d_model), F32),
            pltpu.VMEM((rank, d_model), F32),
            pltpu.VMEM((d_model, gdim), F32),
            pltpu.SemaphoreType.DMA((4,)),
        ],
        compiler_params=pltpu.CompilerParams(
            dimension_semantics=("arbitrary",),
            vmem_limit_bytes=VMEM_LIMIT_BYTES),
        name="hybrid_gla_pool_layer",
    )(*operands)


def kernel(x, w_in, w_gate_up, b_gate, gn_w, pool_w, pool_b, pool_scale, w_a, w_b, w_o, ln_w, ln_b):
    depth = w_in.shape[0]
    alpha = (2.0 * depth) ** 0.25
    w_in_t = jnp.swapaxes(w_in, 1, 2)
    for l in range(depth):
        x = _layer(x, w_in_t[l], w_gate_up[l], b_gate[l], gn_w[l], pool_w[l], pool_b[l], pool_scale[l],
                   w_a[l], w_b[l], w_o[l], ln_w[l], ln_b[l], alpha=alpha)
    return x
```

```python
import functools

import numpy as np
import jax
import jax.numpy as jnp
from jax import lax
from jax.experimental import pallas as pl
from jax.experimental.pallas import tpu as pltpu

GLA_HEADS = 4
GLA_GATE_RANK = 16
GLA_GATE_TAU = 16.0
POOL_WINDOWS = (2, 4, 8, 16)
POOL_LOOKBACK = 16
NORM_EPS = 1e-5

SEQ_TILE = 512
GLA_FAST_CHUNK = 256
GLA_CHUNK = 64
WEIGHT_LOAD_ROWS = 512
VMEM_LIMIT_BYTES = 60 * 1024 * 1024
GLA_FACTOR_MAX_DECAY = 50.0

_QK, _V, _G, _U, _Z, _GATE_A, _GATE_B = range(7)

F32 = jnp.float32
BF16 = jnp.bfloat16


def _dot(a, b):
    return jnp.dot(a, b, preferred_element_type=F32)


def _sigmoid(x):
    return 1.0 / (1.0 + jnp.exp(-x))


def _log_sigmoid(x):
    return jnp.minimum(x, 0.0) - jnp.log(1.0 + jnp.exp(-jnp.abs(x)))


def _hi_lo(x):
    hi = x.astype(BF16)
    return hi, (x - hi.astype(F32)).astype(BF16)


def _row_to_col(row):
    return jnp.transpose(jnp.broadcast_to(row, (8, row.shape[1])))[:, 0:1]


def _gla_factor_operands(qs_ref, k_ref, b_ref, rows):
    b = b_ref[rows, :]
    e_pos = jnp.exp(b)
    e_last = e_pos[GLA_FAST_CHUNK - 1:GLA_FAST_CHUNK, :]
    k_neg = k_ref[rows, :] * jnp.exp(-b)
    qd = (qs_ref[rows, :] * e_pos).astype(BF16)
    kn = k_neg.astype(BF16)
    kd = k_neg * e_last
    return qd, kn, kd, e_last


def _gla_chunk_factorised(operands, v_ref, o_ref, rows, state_ref, new_state_ref, *, hdk, hdv, fillers):
    ts = GLA_FAST_CHUNK
    qd, kn, kd, e_last = operands
    heads = range(GLA_HEADS)
    ks = [slice(h * hdk, (h + 1) * hdk) for h in heads]
    vs = [slice(h * hdv, (h + 1) * hdv) for h in heads]
    causal = (lax.broadcasted_iota(jnp.int32, (ts, ts), 0)
              >= lax.broadcasted_iota(jnp.int32, (ts, ts), 1))
    masked = []
    for h in heads:
        scores = lax.dot_general(qd[:, ks[h]], kn[:, ks[h]], (((1,), (1,)), ((), ())),
                                 preferred_element_type=F32)
        fillers[h]()
        masked.append(jnp.where(causal, scores, 0.0).astype(BF16))
    for h in heads:
        kd_t = jnp.transpose(kd[:, ks[h]]).astype(BF16)
        new_state_ref[h] = state_ref[h] * _row_to_col(e_last[:, ks[h]]) + _dot(kd_t, v_ref[rows, vs[h]])
    for h in heads:
        o_ref[rows, vs[h]] = (_dot(qd[:, ks[h]], state_ref[h].astype(BF16))
                              + _dot(masked[h], v_ref[rows, vs[h]]))


def _gla_tile_direct(qs_ref, k_ref, b_ref, v_ref, o_ref, state_ref, new_state_ref, *, hdk, hdv):
    new_state_ref[...] = state_ref[...]
    row_c = lax.broadcasted_iota(jnp.int32, (GLA_CHUNK, 1), 0)
    lane_c = lax.broadcasted_iota(jnp.int32, (GLA_CHUNK, GLA_CHUNK), 1)

    for c in range(SEQ_TILE // GLA_CHUNK):
        r0 = c * GLA_CHUNK
        rows = pl.ds(r0, GLA_CHUNK)
        qs_c = qs_ref[rows, :]
        b_c = b_ref[rows, :]

        def col_body(j, a_heads):
            kj = k_ref[pl.ds(r0 + j, 1), :]
            bj = b_ref[pl.ds(r0 + j, 1), :]
            p = qs_c * kj * jnp.exp(jnp.minimum(b_c - bj, 0.0))
            p = jnp.where(row_c >= j, p, 0.0)
            new = []
            for h in range(GLA_HEADS):
                s = jnp.sum(p[:, h * hdk:(h + 1) * hdk], axis=-1, keepdims=True)
                new.append(jnp.where(lane_c == j, s, a_heads[h]))
            return tuple(new)

        a_heads = lax.fori_loop(
            0, GLA_CHUNK, col_body,
            tuple(jnp.zeros((GLA_CHUNK, GLA_CHUNK), F32) for _ in range(GLA_HEADS)))

        b_last = b_ref[pl.ds(r0 + GLA_CHUNK - 1, 1), :]
        qd = (qs_c * jnp.exp(b_c)).astype(BF16)
        kd = k_ref[rows, :] * jnp.exp(b_last - b_c)
        e_last = jnp.exp(b_last)
        for h in range(GLA_HEADS):
            ks = slice(h * hdk, (h + 1) * hdk)
            vs = slice(h * hdv, (h + 1) * hdv)
            v_ch = v_ref[rows, vs]
            st = new_state_ref[h]
            o_ref[rows, vs] = (_dot(qd[:, ks], st.astype(BF16))
                               + _dot(a_heads[h].astype(BF16), v_ch))
            kd_t = jnp.transpose(kd[:, ks]).astype(BF16)
            new_state_ref[h] = st * _row_to_col(e_last[:, ks]) + _dot(kd_t, v_ch)


def _load_weights(wint_hbm, wa_hbm, wb_hbm, wo_hbm, poolw_hbm,
                  wbig_ref, wal_ref, wa_ref, wb_ref, wo_ref, poolw_ref,
                  stage_ref, gate_stage_ref, pool_stage_ref, sems, *, n_before_gate, rank):
    rows = WEIGHT_LOAD_ROWS
    d_in, d_model = wint_hbm.shape
    assert n_before_gate % rows == 0 and (d_in - rank) % rows == 0 and d_model % rows == 0
    jobs = []
    for j in range((d_in - rank) // rows):
        row0 = j * rows if j * rows < n_before_gate else j * rows + rank
        jobs.append((wint_hbm, row0, wbig_ref.at[:, j * rows:(j + 1) * rows]))
    for src, dst in ((wa_hbm, wa_ref), (wb_hbm, wb_ref), (wo_hbm, wo_ref)):
        for r0 in range(0, d_model, rows):
            jobs.append((src, r0, dst.at[r0:r0 + rows, :]))

    def staged_copy(i):
        src, row0, _ = jobs[i]
        return pltpu.make_async_copy(src.at[pl.ds(row0, rows), :], stage_ref.at[i % 2], sems.at[i % 2])

    gate_copy = pltpu.make_async_copy(wint_hbm.at[pl.ds(n_before_gate, rank), :], gate_stage_ref, sems.at[2])
    pool_copy = pltpu.make_async_copy(poolw_hbm, pool_stage_ref, sems.at[3])
    gate_copy.start()
    pool_copy.start()
    staged_copy(0).start()
    for i, (src, _, dst) in enumerate(jobs):
        if i + 1 < len(jobs):
            staged_copy(i + 1).start()
        staged_copy(i).wait()
        block = stage_ref[i % 2]
        dst[...] = (jnp.transpose(block) if src is wint_hbm else block).astype(BF16)
    gate_copy.wait()
    wal_ref[...] = jnp.transpose(gate_stage_ref[...]).astype(BF16)
    pool_copy.wait()
    poolw_ref[...] = pool_stage_ref[...].astype(BF16)


def _block_kernel(x_ref, wint_hbm, wa_hbm, wb_hbm, wo_hbm, poolw_hbm,
                  wgu_ref, bg_ref, gnw_ref, poolb_ref, pools_ref, lnw_ref, lnb_ref,
                  tri_ref, band_ref, bandlb_ref,
                  out_ref,
                  qs_ref, k_ref, b_ref, v_ref, o_ref, state_ref, mid_state_ref, new_state_ref, ulast_ref,
                  ya_scale_ref, gate_a_ref, merged_b_ref, resid_ref, la_ref, factorisable_ref,
                  wbig_ref, wal_ref, wa_ref, wb_ref, wo_ref, poolw_ref,
                  stage_ref, gate_stage_ref, pool_stage_ref, load_sems,
                  *, d_model, dk, dv, alpha, tiles_per_seq, n_tiles):
    ts = SEQ_TILE
    hdk = dk // GLA_HEADS
    hdv = dv // GLA_HEADS
    s = pl.program_id(0)
    t = lax.rem(jnp.minimum(s, n_tiles - 1), tiles_per_seq)
    fast_chunks = [slice(r0, r0 + GLA_FAST_CHUNK) for r0 in range(0, ts, GLA_FAST_CHUNK)]
    assert len(fast_chunks) == 2

    @pl.when(s == 0)
    def _():
        _load_weights(wint_hbm, wa_hbm, wb_hbm, wo_hbm, poolw_hbm,
                      wbig_ref, wal_ref, wa_ref, wb_ref, wo_ref, poolw_ref,
                      stage_ref, gate_stage_ref, pool_stage_ref, load_sems,
                      n_before_gate=2 * dk + 2 * dv, rank=GLA_GATE_RANK)

    @pl.when(t == 0)
    def _():
        state_ref[...] = jnp.zeros_like(state_ref)
        ulast_ref[...] = jnp.zeros_like(ulast_ref)

    def step(do_head, do_tail):
        if do_head:
            xf = x_ref[...]
            xb = xf.astype(BF16)

            def proj(n):
                return _dot(xb, wbig_ref[:, n * d_model:(n + 1) * d_model])

        if do_tail:
            o = o_ref[...]
            y_a_parts = []
            for h in range(GLA_HEADS):
                oh = o[:, h * hdv:(h + 1) * hdv]
                y_a_parts.append(oh * lax.rsqrt(jnp.mean(oh * oh, axis=-1, keepdims=True) + NORM_EPS))
            y_a = (jnp.concatenate(y_a_parts, axis=-1) * ya_scale_ref[...]).astype(BF16)

        if do_head:
            a_low = _dot(xb, wal_ref[...])
            qk = proj(_QK)
            qs_ref[...] = qk[:, :dk] * (hdk ** -0.5)
            k_ref[...] = qk[:, dk:]
            gate_pre = _dot(a_low.astype(BF16), wgu_ref[...].astype(BF16)) + bg_ref[...]
            log_a = _log_sigmoid(gate_pre) * (1.0 / GLA_GATE_TAU)
            la_ref[...] = log_a
            la_hi, la_lo = _hi_lo(log_a)

        if do_tail:
            merged = (gate_a_ref[...] * _dot(y_a, wa_ref[...]) + merged_b_ref[...]).astype(BF16)

        if do_head:
            for rows in fast_chunks:
                b_ref[rows, :] = _dot(tri_ref[...], la_hi[rows, :]) + _dot(tri_ref[...], la_lo[rows, :])
            min_b = functools.reduce(
                jnp.minimum, [jnp.min(b_ref[rows.stop - 1:rows.stop, :]) for rows in fast_chunks])
            factorisable_ref[0] = (min_b >= -GLA_FACTOR_MAX_DECAY).astype(jnp.int32)
            v_ref[...] = proj(_V).astype(BF16)
            gla_operands = [_gla_factor_operands(qs_ref, k_ref, b_ref, rows) for rows in fast_chunks]

        if do_tail:
            out_ref[...] = resid_ref[...] + _dot(merged, wo_ref[...])

        if do_head:
            u = proj(_U)
            ub = u.astype(BF16)
            t_abs = t * ts + lax.broadcasted_iota(jnp.int32, (ts, 1), 0)
            gdim = d_model // len(POOL_WINDOWS)
            gslices = [slice(gi * gdim, (gi + 1) * gdim) for gi in range(len(POOL_WINDOWS))]
            pooled_in = []
            for gi, gs in enumerate(gslices):
                parts = []
                for rows in fast_chunks:
                    look_back = (ulast_ref[:, gs] if rows.start == 0
                                 else ub[rows.start - POOL_LOOKBACK:rows.start, gs])
                    in_chunk = _dot(band_ref[gi], ub[rows, gs])
                    carried = _dot(bandlb_ref[gi], look_back)
                    parts += [in_chunk[:POOL_LOOKBACK] + carried, in_chunk[POOL_LOOKBACK:]]
                count = jnp.minimum(t_abs + 1, POOL_WINDOWS[gi]).astype(F32)
                pooled_in.append((jnp.concatenate(parts, axis=0) / count - u[:, gs]).astype(BF16))
            ulast_ref[...] = ub[ts - POOL_LOOKBACK:, :]

        if do_head:
            def gate_factor_columns(h):
                def issue():
                    cols = slice(h * hdv, (h + 1) * hdv)
                    g = _dot(xb, wbig_ref[:, _G * d_model + h * hdv:_G * d_model + (h + 1) * hdv])
                    ya_scale_ref[:, cols] = gnw_ref[:, cols] * (g * _sigmoid(g))
                return issue

            def merge_gate_columns(h):
                def issue():
                    cols = slice(_GATE_A * d_model + h * hdv, _GATE_A * d_model + (h + 1) * hdv)
                    gate_a_ref[:, h * hdv:(h + 1) * hdv] = _sigmoid(_dot(xb, wbig_ref[:, cols]))
                return issue

            chunk_states = (state_ref, mid_state_ref, new_state_ref)
            chunk_fillers = (gate_factor_columns, merge_gate_columns)
            for c, rows in enumerate(fast_chunks):
                _gla_chunk_factorised(gla_operands[c], v_ref, o_ref, rows, chunk_states[c], chunk_states[c + 1],
                                      hdk=hdk, hdv=hdv,
                                      fillers=[chunk_fillers[c](h) for h in range(GLA_HEADS)])

        if do_head:
            z = proj(_Z)
            silu_z = z * _sigmoid(z)

        if do_tail:
            r = out_ref[...]
            mu = jnp.mean(r, axis=-1, keepdims=True)
            rc = r - mu
            var = jnp.mean(rc * rc, axis=-1, keepdims=True)
            out_ref[...] = (rc * lax.rsqrt(var + NORM_EPS) * lnw_ref[...] + lnb_ref[...]).astype(out_ref.dtype)

        if do_head:

            p_parts = [_dot(pooled_in[gi], poolw_ref[gs, :]) for gi, gs in enumerate(gslices)]
            gate_b = _sigmoid(proj(_GATE_B))
            y_b = (jnp.concatenate(p_parts, axis=-1) + poolb_ref[...]) * pools_ref[...] * silu_z
            resid_ref[...] = alpha * xf
            merged_b_ref[...] = gate_b * _dot(y_b.astype(BF16), wb_ref[...])

    pl.when(s == 0)(functools.partial(step, True, False))
    pl.when(jnp.logical_and(s > 0, s < n_tiles))(functools.partial(step, True, True))
    pl.when(s == n_tiles)(functools.partial(step, False, True))

    @pl.when(s < n_tiles)
    def _():
        @pl.when(factorisable_ref[0] == 0)
        def _():
            span = GLA_FAST_CHUNK
            row = lax.broadcasted_iota(jnp.int32, (span, span), 0)
            col = lax.broadcasted_iota(jnp.int32, (span, span), 1)
            tri_chunk = jnp.where(row // GLA_CHUNK == col // GLA_CHUNK, tri_ref[...], jnp.zeros((), BF16))
            la_hi, la_lo = _hi_lo(la_ref[...])
            for rows in fast_chunks:
                b_ref[rows, :] = _dot(tri_chunk, la_hi[rows, :]) + _dot(tri_chunk, la_lo[rows, :])
            _gla_tile_direct(qs_ref, k_ref, b_ref, v_ref, o_ref, state_ref, new_state_ref, hdk=hdk, hdv=hdv)

        state_ref[...] = new_state_ref[...]


def _const_spec(shape):
    nd = len(shape)
    return pl.BlockSpec(shape, lambda s: (0,) * nd, pipeline_mode=pl.Buffered(1))


def _tri_matrix():
    r = np.arange(GLA_FAST_CHUNK)
    return jnp.asarray((r[None, :] <= r[:, None]).astype(np.float32), dtype=BF16)


def _band_matrices():
    r = np.arange(GLA_FAST_CHUNK)[:, None]
    c = np.arange(GLA_FAST_CHUNK)[None, :]
    c_lb = np.arange(POOL_LOOKBACK)[None, :] - POOL_LOOKBACK
    in_win = lambda d, w: ((d >= 0) & (d < w)).astype(np.float32)
    bands = np.stack([in_win(r - c, w) for w in POOL_WINDOWS])
    bands_lb = np.stack([in_win(r[:POOL_LOOKBACK] - c_lb, w) for w in POOL_WINDOWS])
    return jnp.asarray(bands, dtype=BF16), jnp.asarray(bands_lb, dtype=BF16)


def _layer(x, w_in_t, w_gate_up, b_gate, gn_w, pool_w, pool_b, pool_scale, w_a, w_b, w_o, ln_w, ln_b,
           *, alpha):
    bsz, seq, d_model = x.shape
    rank, dk = w_gate_up.shape
    dv = gn_w.shape[0]
    assert seq % SEQ_TILE == 0 and SEQ_TILE % GLA_FAST_CHUNK == 0 and GLA_FAST_CHUNK % GLA_CHUNK == 0
    assert dv == d_model and 2 * dk == d_model and rank == GLA_GATE_RANK
    assert pool_w.shape[0] == len(POOL_WINDOWS)
    d_in = w_in_t.shape[0]
    gdim = pool_w.shape[1]
    row = lambda a: a.reshape(1, -1)
    hbm_weights = (w_in_t, w_a, w_b, w_o, pool_w.reshape(d_model, gdim))
    small = (w_gate_up, row(b_gate), row(gn_w), row(pool_b), row(pool_scale), row(ln_w), row(ln_b),
             _tri_matrix(), *_band_matrices())
    operands = (x, *hbm_weights, *small)
    tiles_per_seq = seq // SEQ_TILE
    n_tiles = bsz * tiles_per_seq

    def tile_block(tile):
        return (tile // tiles_per_seq, tile % tiles_per_seq, 0)

    in_specs = [pl.BlockSpec((None, SEQ_TILE, d_model), lambda s: tile_block(jnp.minimum(s, n_tiles - 1)))]
    in_specs += [pl.BlockSpec(memory_space=pl.ANY) for _ in hbm_weights]
    in_specs += [_const_spec(a.shape) for a in small]
    kernel = functools.partial(_block_kernel, d_model=d_model, dk=dk, dv=dv, alpha=alpha,
                               tiles_per_seq=tiles_per_seq, n_tiles=n_tiles)
    state_shape = (GLA_HEADS, dk // GLA_HEADS, dv // GLA_HEADS)
    return pl.pallas_call(
        kernel,
        grid=(n_tiles + 1,),
        in_specs=in_specs,
        out_specs=pl.BlockSpec((None, SEQ_TILE, d_model), lambda s: tile_block(jnp.maximum(s - 1, 0))),
        out_shape=jax.ShapeDtypeStruct(x.shape, x.dtype),
        scratch_shapes=[
            pltpu.VMEM((SEQ_TILE, dk), F32),
            pltpu.VMEM((SEQ_TILE, dk), F32),
            pltpu.VMEM((SEQ_TILE, dk), F32),
            pltpu.VMEM((SEQ_TILE, dv), BF16),
            pltpu.VMEM((SEQ_TILE, dv), F32),
            pltpu.VMEM(state_shape, F32),
            pltpu.VMEM(state_shape, F32),
            pltpu.VMEM(state_shape, F32),
            pltpu.VMEM((POOL_LOOKBACK, d_model), BF16),
            pltpu.VMEM((SEQ_TILE, dv), F32),
            pltpu.VMEM((SEQ_TILE, d_model), F32),
            pltpu.VMEM((SEQ_TILE, d_model), F32),
            pltpu.VMEM((SEQ_TILE, d_model), F32),
            pltpu.VMEM((SEQ_TILE, dk), F32),
            pltpu.SMEM((1,), jnp.int32),
            pltpu.VMEM((d_model, d_in - rank), BF16),
            pltpu.VMEM((d_model, rank), BF16),
            pltpu.VMEM(w_a.shape, BF16),
            pltpu.VMEM(w_b.shape, BF16),
            pltpu.VMEM(w_o.shape, BF16),
            pltpu.VMEM((d_model, gdim), BF16),
            pltpu.VMEM((2, WEIGHT_LOAD_ROWS, d_model), F32),
            pltpu.VMEM((rank, d_model), F32),
            pltpu.VMEM((d_model, gdim), F32),
            pltpu.SemaphoreType.DMA((4,)),
        ],
        compiler_params=pltpu.CompilerParams(
            dimension_semantics=("arbitrary",),
            vmem_limit_bytes=VMEM_LIMIT_BYTES),
        name="hybrid_gla_pool_layer",
    )(*operands)


def kernel(x, w_in, w_gate_up, b_gate, gn_w, pool_w, pool_b, pool_scale, w_a, w_b, w_o, ln_w, ln_b):
    depth = w_in.shape[0]
    alpha = (2.0 * depth) ** 0.25
    w_in_t = jnp.swapaxes(w_in, 1, 2)
    for l in range(depth):
        x = _layer(x, w_in_t[l], w_gate_up[l], b_gate[l], gn_w[l], pool_w[l], pool_b[l], pool_scale[l],
                   w_a[l], w_b[l], w_o[l], ln_w[l], ln_b[l], alpha=alpha)
    return x
```

```python
import functools

import numpy as np
import jax
import jax.numpy as jnp
from jax import lax
from jax.experimental import pallas as pl
from jax.experimental.pallas import tpu as pltpu

GLA_HEADS = 4
GLA_GATE_RANK = 16
GLA_GATE_TAU = 16.0
POOL_WINDOWS = (2, 4, 8, 16)
POOL_LOOKBACK = 16
NORM_EPS = 1e-5

SEQ_TILE = 512
GLA_FAST_CHUNK = 256
GLA_CHUNK = 64
WEIGHT_LOAD_ROWS = 512
VMEM_LIMIT_BYTES = 60 * 1024 * 1024
GLA_FACTOR_MAX_DECAY = 50.0

_QK, _V, _G, _U, _Z, _GATE_A, _GATE_B = range(7)

F32 = jnp.float32
BF16 = jnp.bfloat16


def _dot(a, b):
    return jnp.dot(a, b, preferred_element_type=F32)


def _sigmoid(x):
    return 1.0 / (1.0 + jnp.exp(-x))


def _log_sigmoid(x):
    return jnp.minimum(x, 0.0) - jnp.log(1.0 + jnp.exp(-jnp.abs(x)))


def _hi_lo(x):
    hi = x.astype(BF16)
    return hi, (x - hi.astype(F32)).astype(BF16)


def _row_to_col(row):
    return jnp.transpose(jnp.broadcast_to(row, (8, row.shape[1])))[:, 0:1]


def _gla_factor_operands(qs_ref, k_ref, b_ref, rows):
    b = b_ref[rows, :]
    e_pos = jnp.exp(b)
    e_last = e_pos[GLA_FAST_CHUNK - 1:GLA_FAST_CHUNK, :]
    k_neg = k_ref[rows, :] * jnp.exp(-b)
    qd = (qs_ref[rows, :] * e_pos).astype(BF16)
    kn = k_neg.astype(BF16)
    kd = k_neg * e_last
    return qd, kn, kd, e_last


def _gla_chunk_factorised(operands, v_ref, o_ref, rows, state_ref, new_state_ref, *, hdk, hdv, fillers):
    ts = GLA_FAST_CHUNK
    qd, kn, kd, e_last = operands
    heads = range(GLA_HEADS)
    ks = [slice(h * hdk, (h + 1) * hdk) for h in heads]
    vs = [slice(h * hdv, (h + 1) * hdv) for h in heads]
    causal = (lax.broadcasted_iota(jnp.int32, (ts, ts), 0)
              >= lax.broadcasted_iota(jnp.int32, (ts, ts), 1))
    masked = []
    for h in heads:
        scores = lax.dot_general(qd[:, ks[h]], kn[:, ks[h]], (((1,), (1,)), ((), ())),
                                 preferred_element_type=F32)
        fillers[h]()
        masked.append(jnp.where(causal, scores, 0.0).astype(BF16))
    for h in heads:
        kd_t = jnp.transpose(kd[:, ks[h]]).astype(BF16)
        new_state_ref[h] = state_ref[h] * _row_to_col(e_last[:, ks[h]]) + _dot(kd_t, v_ref[rows, vs[h]])
    for h in heads:
        o_ref[rows, vs[h]] = (_dot(qd[:, ks[h]], state_ref[h].astype(BF16))
                              + _dot(masked[h], v_ref[rows, vs[h]]))


def _gla_tile_direct(qs_ref, k_ref, b_ref, v_ref, o_ref, state_ref, new_state_ref, *, hdk, hdv):
    new_state_ref[...] = state_ref[...]
    row_c = lax.broadcasted_iota(jnp.int32, (GLA_CHUNK, 1), 0)
    lane_c = lax.broadcasted_iota(jnp.int32, (GLA_CHUNK, GLA_CHUNK), 1)

    for c in range(SEQ_TILE // GLA_CHUNK):
        r0 = c * GLA_CHUNK
        rows = pl.ds(r0, GLA_CHUNK)
        qs_c = qs_ref[rows, :]
        b_c = b_ref[rows, :]

        def col_body(j, a_heads):
            kj = k_ref[pl.ds(r0 + j, 1), :]
            bj = b_ref[pl.ds(r0 + j, 1), :]
            p = qs_c * kj * jnp.exp(jnp.minimum(b_c - bj, 0.0))
            p = jnp.where(row_c >= j, p, 0.0)
            new = []
            for h in range(GLA_HEADS):
                s = jnp.sum(p[:, h * hdk:(h + 1) * hdk], axis=-1, keepdims=True)
                new.append(jnp.where(lane_c == j, s, a_heads[h]))
            return tuple(new)

        a_heads = lax.fori_loop(
            0, GLA_CHUNK, col_body,
            tuple(jnp.zeros((GLA_CHUNK, GLA_CHUNK), F32) for _ in range(GLA_HEADS)))

        b_last = b_ref[pl.ds(r0 + GLA_CHUNK - 1, 1), :]
        qd = (qs_c * jnp.exp(b_c)).astype(BF16)
        kd = k_ref[rows, :] * jnp.exp(b_last - b_c)
        e_last = jnp.exp(b_last)
        for h in range(GLA_HEADS):
            ks = slice(h * hdk, (h + 1) * hdk)
            vs = slice(h * hdv, (h + 1) * hdv)
            v_ch = v_ref[rows, vs]
            st = new_state_ref[h]
            o_ref[rows, vs] = (_dot(qd[:, ks], st.astype(BF16))
                               + _dot(a_heads[h].astype(BF16), v_ch))
            kd_t = jnp.transpose(kd[:, ks]).astype(BF16)
            new_state_ref[h] = st * _row_to_col(e_last[:, ks]) + _dot(kd_t, v_ch)


def _load_weights(wint_hbm, wa_hbm, wb_hbm, wo_hbm, poolw_hbm,
                  wbig_ref, wal_ref, wa_ref, wb_ref, wo_ref, poolw_ref,
                  stage_ref, gate_stage_ref, pool_stage_ref, sems, *, n_before_gate, rank):
    rows = WEIGHT_LOAD_ROWS
    d_in, d_model = wint_hbm.shape
    assert n_before_gate % rows == 0 and (d_in - rank) % rows == 0 and d_model % rows == 0
    jobs = []
    for j in range((d_in - rank) // rows):
        row0 = j * rows if j * rows < n_before_gate else j * rows + rank
        jobs.append((wint_hbm, row0, wbig_ref.at[:, j * rows:(j + 1) * rows]))
    for src, dst in ((wa_hbm, wa_ref), (wb_hbm, wb_ref), (wo_hbm, wo_ref)):
        for r0 in range(0, d_model, rows):
            jobs.append((src, r0, dst.at[r0:r0 + rows, :]))

    def staged_copy(i):
        src, row0, _ = jobs[i]
        return pltpu.make_async_copy(src.at[pl.ds(row0, rows), :], stage_ref.at[i % 2], sems.at[i % 2])

    gate_copy = pltpu.make_async_copy(wint_hbm.at[pl.ds(n_before_gate, rank), :], gate_stage_ref, sems.at[2])
    pool_copy = pltpu.make_async_copy(poolw_hbm, pool_stage_ref, sems.at[3])
    gate_copy.start()
    pool_copy.start()
    staged_copy(0).start()
    for i, (src, _, dst) in enumerate(jobs):
        if i + 1 < len(jobs):
            staged_copy(i + 1).start()
        staged_copy(i).wait()
        block = stage_ref[i % 2]
        dst[...] = (jnp.transpose(block) if src is wint_hbm else block).astype(BF16)
    gate_copy.wait()
    wal_ref[...] = jnp.transpose(gate_stage_ref[...]).astype(BF16)
    pool_copy.wait()
    poolw_ref[...] = pool_stage_ref[...].astype(BF16)


def _block_kernel(x_ref, wint_hbm, wa_hbm, wb_hbm, wo_hbm, poolw_hbm,
                  wgu_ref, bg_ref, gnw_ref, poolb_ref, pools_ref, lnw_ref, lnb_ref,
                  tri_ref, band_ref, bandlb_ref,
                  out_ref,
                  qs_ref, k_ref, b_ref, v_ref, o_ref, state_ref, mid_state_ref, new_state_ref, ulast_ref,
                  ya_scale_ref, gate_a_ref, merged_b_ref, resid_ref, la_ref, factorisable_ref,
                  wbig_ref, wal_ref, wa_ref, wb_ref, wo_ref, poolw_ref,
                  stage_ref, gate_stage_ref, pool_stage_ref, load_sems,
                  *, d_model, dk, dv, alpha, tiles_per_seq, n_tiles):
    ts = SEQ_TILE
    hdk = dk // GLA_HEADS
    hdv = dv // GLA_HEADS
    s = pl.program_id(0)
    t = lax.rem(jnp.minimum(s, n_tiles - 1), tiles_per_seq)
    fast_chunks = [slice(r0, r0 + GLA_FAST_CHUNK) for r0 in range(0, ts, GLA_FAST_CHUNK)]
    assert len(fast_chunks) == 2

    @pl.when(s == 0)
    def _():
        _load_weights(wint_hbm, wa_hbm, wb_hbm, wo_hbm, poolw_hbm,
                      wbig_ref, wal_ref, wa_ref, wb_ref, wo_ref, poolw_ref,
                      stage_ref, gate_stage_ref, pool_stage_ref, load_sems,
                      n_before_gate=2 * dk + 2 * dv, rank=GLA_GATE_RANK)

    @pl.when(t == 0)
    def _():
        state_ref[...] = jnp.zeros_like(state_ref)
        ulast_ref[...] = jnp.zeros_like(ulast_ref)

    def step(do_head, do_tail):
        if do_head:
            xf = x_ref[...]
            xb = xf.astype(BF16)

            def proj(n):
                return _dot(xb, wbig_ref[:, n * d_model:(n + 1) * d_model])

        if do_tail:
            o = o_ref[...]
            y_a_parts = []
            for h in range(GLA_HEADS):
                oh = o[:, h * hdv:(h + 1) * hdv]
                y_a_parts.append(oh * lax.rsqrt(jnp.mean(oh * oh, axis=-1, keepdims=True) + NORM_EPS))
            y_a = (jnp.concatenate(y_a_parts, axis=-1) * ya_scale_ref[...]).astype(BF16)

        if do_head:
            a_low = _dot(xb, wal_ref[...])
            qk = proj(_QK)
            qs_ref[...] = qk[:, :dk] * (hdk ** -0.5)
            k_ref[...] = qk[:, dk:]
            gate_pre = _dot(a_low.astype(BF16), wgu_ref[...].astype(BF16)) + bg_ref[...]
            log_a = _log_sigmoid(gate_pre) * (1.0 / GLA_GATE_TAU)
            la_ref[...] = log_a
            la_hi, la_lo = _hi_lo(log_a)

        if do_tail:
            merged = (gate_a_ref[...] * _dot(y_a, wa_ref[...]) + merged_b_ref[...]).astype(BF16)

        if do_head:
            for rows in fast_chunks:
                b_ref[rows, :] = _dot(tri_ref[...], la_hi[rows, :]) + _dot(tri_ref[...], la_lo[rows, :])
            min_b = functools.reduce(
                jnp.minimum, [jnp.min(b_ref[rows.stop - 1:rows.stop, :]) for rows in fast_chunks])
            factorisable_ref[0] = (min_b >= -GLA_FACTOR_MAX_DECAY).astype(jnp.int32)
            v_ref[...] = proj(_V).astype(BF16)
            gla_operands = [_gla_factor_operands(qs_ref, k_ref, b_ref, rows) for rows in fast_chunks]

        if do_tail:
            out_ref[...] = resid_ref[...] + _dot(merged, wo_ref[...])

        if do_head:
            u = proj(_U)
            ub = u.astype(BF16)
            t_abs = t * ts + lax.broadcasted_iota(jnp.int32, (ts, 1), 0)
            gdim = d_model // len(POOL_WINDOWS)
            gslices = [slice(gi * gdim, (gi + 1) * gdim) for gi in range(len(POOL_WINDOWS))]
            pooled_in = []
            for gi, gs in enumerate(gslices):
                parts = []
                for rows in fast_chunks:
                    look_back = (ulast_ref[:, gs] if rows.start == 0
                                 else ub[rows.start - POOL_LOOKBACK:rows.start, gs])
                    in_chunk = _dot(band_ref[gi], ub[rows, gs])
                    carried = _dot(bandlb_ref[gi], look_back)
                    parts += [in_chunk[:POOL_LOOKBACK] + carried, in_chunk[POOL_LOOKBACK:]]
                count = jnp.minimum(t_abs + 1, POOL_WINDOWS[gi]).astype(F32)
                pooled_in.append((jnp.concatenate(parts, axis=0) / count - u[:, gs]).astype(BF16))
            ulast_ref[...] = ub[ts - POOL_LOOKBACK:, :]

        if do_head:
            def gate_factor_columns(h):
                def issue():
                    cols = slice(h * hdv, (h + 1) * hdv)
                    g = _dot(xb, wbig_ref[:, _G * d_model + h * hdv:_G * d_model + (h + 1) * hdv])
                    ya_scale_ref[:, cols] = (gnw_ref[:, cols] * (g * _sigmoid(g))).astype(BF16)
                return issue

            def merge_gate_columns(h):
                def issue():
                    cols = slice(_GATE_A * d_model + h * hdv, _GATE_A * d_model + (h + 1) * hdv)
                    gate_a_ref[:, h * hdv:(h + 1) * hdv] = _sigmoid(_dot(xb, wbig_ref[:, cols])).astype(BF16)
                return issue

            chunk_states = (state_ref, mid_state_ref, new_state_ref)
            chunk_fillers = (gate_factor_columns, merge_gate_columns)
            for c, rows in enumerate(fast_chunks):
                _gla_chunk_factorised(gla_operands[c], v_ref, o_ref, rows, chunk_states[c], chunk_states[c + 1],
                                      hdk=hdk, hdv=hdv,
                                      fillers=[chunk_fillers[c](h) for h in range(GLA_HEADS)])

        if do_head:
            z = proj(_Z)
            silu_z = z * _sigmoid(z)

        if do_tail:
            r = out_ref[...]
            mu = jnp.mean(r, axis=-1, keepdims=True)
            rc = r - mu
            var = jnp.mean(rc * rc, axis=-1, keepdims=True)
            out_ref[...] = (rc * lax.rsqrt(var + NORM_EPS) * lnw_ref[...] + lnb_ref[...]).astype(out_ref.dtype)

        if do_head:

            p_parts = [_dot(pooled_in[gi], poolw_ref[gs, :]) for gi, gs in enumerate(gslices)]
            gate_b = _sigmoid(proj(_GATE_B))
            y_b = (jnp.concatenate(p_parts, axis=-1) + poolb_ref[...]) * pools_ref[...] * silu_z
            resid_ref[...] = alpha * xf
            merged_b_ref[...] = (gate_b * _dot(y_b.astype(BF16), wb_ref[...])).astype(BF16)

    pl.when(s == 0)(functools.partial(step, True, False))
    pl.when(jnp.logical_and(s > 0, s < n_tiles))(functools.partial(step, True, True))
    pl.when(s == n_tiles)(functools.partial(step, False, True))

    @pl.when(s < n_tiles)
    def _():
        @pl.when(factorisable_ref[0] == 0)
        def _():
            span = GLA_FAST_CHUNK
            row = lax.broadcasted_iota(jnp.int32, (span, span), 0)
            col = lax.broadcasted_iota(jnp.int32, (span, span), 1)
            tri_chunk = jnp.where(row // GLA_CHUNK == col // GLA_CHUNK, tri_ref[...], jnp.zeros((), BF16))
            la_hi, la_lo = _hi_lo(la_ref[...])
            for rows in fast_chunks:
                b_ref[rows, :] = _dot(tri_chunk, la_hi[rows, :]) + _dot(tri_chunk, la_lo[rows, :])
            _gla_tile_direct(qs_ref, k_ref, b_ref, v_ref, o_ref, state_ref, new_state_ref, hdk=hdk, hdv=hdv)

        state_ref[...] = new_state_ref[...]


def _const_spec(shape):
    nd = len(shape)
    return pl.BlockSpec(shape, lambda s: (0,) * nd, pipeline_mode=pl.Buffered(1))


def _tri_matrix():
    r = np.arange(GLA_FAST_CHUNK)
    return jnp.asarray((r[None, :] <= r[:, None]).astype(np.float32), dtype=BF16)


def _band_matrices():
    r = np.arange(GLA_FAST_CHUNK)[:, None]
    c = np.arange(GLA_FAST_CHUNK)[None, :]
    c_lb = np.arange(POOL_LOOKBACK)[None, :] - POOL_LOOKBACK
    in_win = lambda d, w: ((d >= 0) & (d < w)).astype(np.float32)
    bands = np.stack([in_win(r - c, w) for w in POOL_WINDOWS])
    bands_lb = np.stack([in_win(r[:POOL_LOOKBACK] - c_lb, w) for w in POOL_WINDOWS])
    return jnp.asarray(bands, dtype=BF16), jnp.asarray(bands_lb, dtype=BF16)


def _layer(x, w_in_t, w_gate_up, b_gate, gn_w, pool_w, pool_b, pool_scale, w_a, w_b, w_o, ln_w, ln_b,
           *, alpha):
    bsz, seq, d_model = x.shape
    rank, dk = w_gate_up.shape
    dv = gn_w.shape[0]
    assert seq % SEQ_TILE == 0 and SEQ_TILE % GLA_FAST_CHUNK == 0 and GLA_FAST_CHUNK % GLA_CHUNK == 0
    assert dv == d_model and 2 * dk == d_model and rank == GLA_GATE_RANK
    assert pool_w.shape[0] == len(POOL_WINDOWS)
    d_in = w_in_t.shape[0]
    gdim = pool_w.shape[1]
    row = lambda a: a.reshape(1, -1)
    hbm_weights = (w_in_t, w_a, w_b, w_o, pool_w.reshape(d_model, gdim))
    small = (w_gate_up, row(b_gate), row(gn_w), row(pool_b), row(pool_scale), row(ln_w), row(ln_b),
             _tri_matrix(), *_band_matrices())
    operands = (x, *hbm_weights, *small)
    tiles_per_seq = seq // SEQ_TILE
    n_tiles = bsz * tiles_per_seq

    def tile_block(tile):
        return (tile // tiles_per_seq, tile % tiles_per_seq, 0)

    in_specs = [pl.BlockSpec((None, SEQ_TILE, d_model), lambda s: tile_block(jnp.minimum(s, n_tiles - 1)))]
    in_specs += [pl.BlockSpec(memory_space=pl.ANY) for _ in hbm_weights]
    in_specs += [_const_spec(a.shape) for a in small]
    kernel = functools.partial(_block_kernel, d_model=d_model, dk=dk, dv=dv, alpha=alpha,
                               tiles_per_seq=tiles_per_seq, n_tiles=n_tiles)
    state_shape = (GLA_HEADS, dk // GLA_HEADS, dv // GLA_HEADS)
    return pl.pallas_call(
        kernel,
        grid=(n_tiles + 1,),
        in_specs=in_specs,
        out_specs=pl.BlockSpec((None, SEQ_TILE, d_model), lambda s: tile_block(jnp.maximum(s - 1, 0))),
        out_shape=jax.ShapeDtypeStruct(x.shape, x.dtype),
        scratch_shapes=[
            pltpu.VMEM((SEQ_TILE, dk), F32),
            pltpu.VMEM((SEQ_TILE, dk), F32),
            pltpu.VMEM((SEQ_TILE, dk), F32),
            pltpu.VMEM((SEQ_TILE, dv), BF16),
            pltpu.VMEM((SEQ_TILE, dv), F32),
            pltpu.VMEM(state_shape, F32),
            pltpu.VMEM(state_shape, F32),
            pltpu.VMEM(state_shape, F32),
            pltpu.VMEM((POOL_LOOKBACK, d_model), BF16),
            pltpu.VMEM((SEQ_TILE, dv), BF16),
            pltpu.VMEM((SEQ_TILE, d_model), BF16),
            pltpu.VMEM((SEQ_TILE, d_model), BF16),
            pltpu.VMEM((SEQ_TILE, d_model), F32),
            pltpu.VMEM((SEQ_TILE, dk), F32),
            pltpu.SMEM((1,), jnp.int32),
            pltpu.VMEM((d_model, d_in - rank), BF16),
            pltpu.VMEM((d_model, rank), BF16),
            pltpu.VMEM(w_a.shape, BF16),
            pltpu.VMEM(w_b.shape, BF16),
            pltpu.VMEM(w_o.shape, BF16),
            pltpu.VMEM((d_model, gdim), BF16),
            pltpu.VMEM((2, WEIGHT_LOAD_ROWS, d_model), F32),
            pltpu.VMEM((rank, d_model), F32),
            pltpu.VMEM((d_model, gdim), F32),
            pltpu.SemaphoreType.DMA((4,)),
        ],
        compiler_params=pltpu.CompilerParams(
            dimension_semantics=("arbitrary",),
            vmem_limit_bytes=VMEM_LIMIT_BYTES),
        name="hybrid_gla_pool_layer",
    )(*operands)


def kernel(x, w_in, w_gate_up, b_gate, gn_w, pool_w, pool_b, pool_scale, w_a, w_b, w_o, ln_w, ln_b):
    depth = w_in.shape[0]
    alpha = (2.0 * depth) ** 0.25
    w_in_t = jnp.swapaxes(w_in, 1, 2)
    for l in range(depth):
        x = _layer(x, w_in_t[l], w_gate_up[l], b_gate[l], gn_w[l], pool_w[l], pool_b[l], pool_scale[l],
                   w_a[l], w_b[l], w_o[l], ln_w[l], ln_b[l], alpha=alpha)
    return x
```

```python
import functools

import numpy as np
import jax
import jax.numpy as jnp
from jax import lax
from jax.experimental import pallas as pl
from jax.experimental.pallas import tpu as pltpu

GLA_HEADS = 4
GLA_GATE_RANK = 16
GLA_GATE_TAU = 16.0
POOL_WINDOWS = (2, 4, 8, 16)
POOL_LOOKBACK = 16
NORM_EPS = 1e-5

SEQ_TILE = 512
GLA_FAST_CHUNK = 256
GLA_CHUNK = 64
WEIGHT_LOAD_ROWS = 256
WEIGHT_STAGE_SLOTS = 4
VMEM_LIMIT_BYTES = 60 * 1024 * 1024
GLA_FACTOR_MAX_DECAY = 50.0

_QK, _V, _G, _U, _Z, _GATE_A, _GATE_B = range(7)

F32 = jnp.float32
BF16 = jnp.bfloat16


def _dot(a, b):
    return jnp.dot(a, b, preferred_element_type=F32)


def _sigmoid(x):
    return 1.0 / (1.0 + jnp.exp(-x))


def _log_sigmoid(x):
    return jnp.minimum(x, 0.0) - jnp.log(1.0 + jnp.exp(-jnp.abs(x)))


def _hi_lo(x):
    hi = x.astype(BF16)
    return hi, (x - hi.astype(F32)).astype(BF16)


def _row_to_col(row):
    return jnp.transpose(jnp.broadcast_to(row, (8, row.shape[1])))[:, 0:1]


def _gla_factor_operands(qs_ref, k_ref, b_ref, rows):
    b = b_ref[rows, :]
    e_pos = jnp.exp(b)
    e_last = e_pos[GLA_FAST_CHUNK - 1:GLA_FAST_CHUNK, :]
    k_neg = k_ref[rows, :] * jnp.exp(-b)
    qd = (qs_ref[rows, :] * e_pos).astype(BF16)
    kn = k_neg.astype(BF16)
    kd = k_neg * e_last
    return qd, kn, kd, e_last


def _gla_chunk_factorised(operands, v_ref, o_ref, rows, state_ref, new_state_ref, *, hdk, hdv, fillers):
    ts = GLA_FAST_CHUNK
    qd, kn, kd, e_last = operands
    heads = range(GLA_HEADS)
    ks = [slice(h * hdk, (h + 1) * hdk) for h in heads]
    vs = [slice(h * hdv, (h + 1) * hdv) for h in heads]
    causal = (lax.broadcasted_iota(jnp.int32, (ts, ts), 0)
              >= lax.broadcasted_iota(jnp.int32, (ts, ts), 1))
    masked = []
    for h in heads:
        scores = lax.dot_general(qd[:, ks[h]], kn[:, ks[h]], (((1,), (1,)), ((), ())),
                                 preferred_element_type=F32)
        fillers[h]()
        masked.append(jnp.where(causal, scores, 0.0).astype(BF16))
    for h in heads:
        kd_t = jnp.transpose(kd[:, ks[h]]).astype(BF16)
        new_state_ref[h] = state_ref[h] * _row_to_col(e_last[:, ks[h]]) + _dot(kd_t, v_ref[rows, vs[h]])
    for h in heads:
        o_ref[rows, vs[h]] = (_dot(qd[:, ks[h]], state_ref[h].astype(BF16))
                              + _dot(masked[h], v_ref[rows, vs[h]]))


def _gla_tile_direct(qs_ref, k_ref, b_ref, v_ref, o_ref, state_ref, new_state_ref, *, hdk, hdv):
    new_state_ref[...] = state_ref[...]
    row_c = lax.broadcasted_iota(jnp.int32, (GLA_CHUNK, 1), 0)
    lane_c = lax.broadcasted_iota(jnp.int32, (GLA_CHUNK, GLA_CHUNK), 1)

    for c in range(SEQ_TILE // GLA_CHUNK):
        r0 = c * GLA_CHUNK
        rows = pl.ds(r0, GLA_CHUNK)
        qs_c = qs_ref[rows, :]
        b_c = b_ref[rows, :]

        def col_body(j, a_heads):
            kj = k_ref[pl.ds(r0 + j, 1), :]
            bj = b_ref[pl.ds(r0 + j, 1), :]
            p = qs_c * kj * jnp.exp(jnp.minimum(b_c - bj, 0.0))
            p = jnp.where(row_c >= j, p, 0.0)
            new = []
            for h in range(GLA_HEADS):
                s = jnp.sum(p[:, h * hdk:(h + 1) * hdk], axis=-1, keepdims=True)
                new.append(jnp.where(lane_c == j, s, a_heads[h]))
            return tuple(new)

        a_heads = lax.fori_loop(
            0, GLA_CHUNK, col_body,
            tuple(jnp.zeros((GLA_CHUNK, GLA_CHUNK), F32) for _ in range(GLA_HEADS)))

        b_last = b_ref[pl.ds(r0 + GLA_CHUNK - 1, 1), :]
        qd = (qs_c * jnp.exp(b_c)).astype(BF16)
        kd = k_ref[rows, :] * jnp.exp(b_last - b_c)
        e_last = jnp.exp(b_last)
        for h in range(GLA_HEADS):
            ks = slice(h * hdk, (h + 1) * hdk)
            vs = slice(h * hdv, (h + 1) * hdv)
            v_ch = v_ref[rows, vs]
            st = new_state_ref[h]
            o_ref[rows, vs] = (_dot(qd[:, ks], st.astype(BF16))
                               + _dot(a_heads[h].astype(BF16), v_ch))
            kd_t = jnp.transpose(kd[:, ks]).astype(BF16)
            new_state_ref[h] = st * _row_to_col(e_last[:, ks]) + _dot(kd_t, v_ch)


def _load_weights(wint_hbm, wa_hbm, wb_hbm, wo_hbm, poolw_hbm,
                  wbig_ref, wal_ref, wa_ref, wb_ref, wo_ref, poolw_ref,
                  stage_ref, gate_stage_ref, pool_stage_ref, sems, *, n_before_gate, rank):
    rows = WEIGHT_LOAD_ROWS
    d_in, d_model = wint_hbm.shape
    assert n_before_gate % rows == 0 and (d_in - rank) % rows == 0 and d_model % rows == 0
    jobs = []
    for j in range((d_in - rank) // rows):
        row0 = j * rows if j * rows < n_before_gate else j * rows + rank
        jobs.append((wint_hbm, row0, wbig_ref.at[:, j * rows:(j + 1) * rows]))
    for src, dst in ((wa_hbm, wa_ref), (wb_hbm, wb_ref), (wo_hbm, wo_ref)):
        for r0 in range(0, d_model, rows):
            jobs.append((src, r0, dst.at[r0:r0 + rows, :]))

    slots = stage_ref.shape[0]

    def staged_copy(i):
        src, row0, _ = jobs[i]
        return pltpu.make_async_copy(src.at[pl.ds(row0, rows), :], stage_ref.at[i % slots], sems.at[i % slots])

    gate_copy = pltpu.make_async_copy(wint_hbm.at[pl.ds(n_before_gate, rank), :], gate_stage_ref, sems.at[slots])
    pool_copy = pltpu.make_async_copy(poolw_hbm, pool_stage_ref, sems.at[slots + 1])
    gate_copy.start()
    pool_copy.start()
    ahead = slots - 1
    for i in range(min(ahead, len(jobs))):
        staged_copy(i).start()
    for i, (src, _, dst) in enumerate(jobs):
        if i + ahead < len(jobs):
            staged_copy(i + ahead).start()
        staged_copy(i).wait()
        block = stage_ref[i % slots]
        dst[...] = (jnp.transpose(block) if src is wint_hbm else block).astype(BF16)
    gate_copy.wait()
    wal_ref[...] = jnp.transpose(gate_stage_ref[...]).astype(BF16)
    pool_copy.wait()
    poolw_ref[...] = pool_stage_ref[...].astype(BF16)


def _block_kernel(x_ref, wint_hbm, wa_hbm, wb_hbm, wo_hbm, poolw_hbm,
                  wgu_ref, bg_ref, gnw_ref, poolb_ref, pools_ref, lnw_ref, lnb_ref,
                  tri_ref, band_ref, bandlb_ref,
                  out_ref,
                  qs_ref, k_ref, b_ref, v_ref, o_ref, state_ref, mid_state_ref, new_state_ref, ulast_ref,
                  ya_scale_ref, gate_a_ref, merged_b_ref, resid_ref, la_ref, factorisable_ref,
                  wbig_ref, wal_ref, wa_ref, wb_ref, wo_ref, poolw_ref,
                  stage_ref, gate_stage_ref, pool_stage_ref, load_sems,
                  *, d_model, dk, dv, alpha, tiles_per_seq, n_tiles):
    ts = SEQ_TILE
    hdk = dk // GLA_HEADS
    hdv = dv // GLA_HEADS
    s = pl.program_id(0)
    t = lax.rem(jnp.minimum(s, n_tiles - 1), tiles_per_seq)
    fast_chunks = [slice(r0, r0 + GLA_FAST_CHUNK) for r0 in range(0, ts, GLA_FAST_CHUNK)]
    assert len(fast_chunks) == 2

    @pl.when(s == 0)
    def _():
        _load_weights(wint_hbm, wa_hbm, wb_hbm, wo_hbm, poolw_hbm,
                      wbig_ref, wal_ref, wa_ref, wb_ref, wo_ref, poolw_ref,
                      stage_ref, gate_stage_ref, pool_stage_ref, load_sems,
                      n_before_gate=2 * dk + 2 * dv, rank=GLA_GATE_RANK)

    @pl.when(t == 0)
    def _():
        state_ref[...] = jnp.zeros_like(state_ref)
        ulast_ref[...] = jnp.zeros_like(ulast_ref)

    def step(do_head, do_tail):
        if do_head:
            xf = x_ref[...]
            xb = xf.astype(BF16)

            def proj(n):
                return _dot(xb, wbig_ref[:, n * d_model:(n + 1) * d_model])

        if do_tail:
            o = o_ref[...]
            y_a_parts = []
            for h in range(GLA_HEADS):
                oh = o[:, h * hdv:(h + 1) * hdv]
                y_a_parts.append(oh * lax.rsqrt(jnp.mean(oh * oh, axis=-1, keepdims=True) + NORM_EPS))
            y_a = (jnp.concatenate(y_a_parts, axis=-1) * ya_scale_ref[...]).astype(BF16)

        if do_head:
            a_low = _dot(xb, wal_ref[...])
            qk = proj(_QK)
            qs_ref[...] = qk[:, :dk] * (hdk ** -0.5)
            k_ref[...] = qk[:, dk:]
            gate_pre = _dot(a_low.astype(BF16), wgu_ref[...].astype(BF16)) + bg_ref[...]
            log_a = _log_sigmoid(gate_pre) * (1.0 / GLA_GATE_TAU)
            la_ref[...] = log_a
            la_hi, la_lo = _hi_lo(log_a)

        if do_tail:
            merged = (gate_a_ref[...] * _dot(y_a, wa_ref[...]) + merged_b_ref[...]).astype(BF16)

        if do_head:
            for rows in fast_chunks:
                b_ref[rows, :] = _dot(tri_ref[...], la_hi[rows, :]) + _dot(tri_ref[...], la_lo[rows, :])
            min_b = functools.reduce(
                jnp.minimum, [jnp.min(b_ref[rows.stop - 1:rows.stop, :]) for rows in fast_chunks])
            factorisable_ref[0] = (min_b >= -GLA_FACTOR_MAX_DECAY).astype(jnp.int32)
            v_ref[...] = proj(_V).astype(BF16)
            gla_operands = [_gla_factor_operands(qs_ref, k_ref, b_ref, rows) for rows in fast_chunks]

        if do_tail:
            out_ref[...] = resid_ref[...] + _dot(merged, wo_ref[...])

        if do_head:
            u = proj(_U)
            ub = u.astype(BF16)
            t_abs = t * ts + lax.broadcasted_iota(jnp.int32, (ts, 1), 0)
            gdim = d_model // len(POOL_WINDOWS)
            gslices = [slice(gi * gdim, (gi + 1) * gdim) for gi in range(len(POOL_WINDOWS))]
            pooled_in = []
            for gi, gs in enumerate(gslices):
                parts = []
                for rows in fast_chunks:
                    look_back = (ulast_ref[:, gs] if rows.start == 0
                                 else ub[rows.start - POOL_LOOKBACK:rows.start, gs])
                    in_chunk = _dot(band_ref[gi], ub[rows, gs])
                    carried = _dot(bandlb_ref[gi], look_back)
                    parts += [in_chunk[:POOL_LOOKBACK] + carried, in_chunk[POOL_LOOKBACK:]]
                count = jnp.minimum(t_abs + 1, POOL_WINDOWS[gi]).astype(F32)
                pooled_in.append((jnp.concatenate(parts, axis=0) / count - u[:, gs]).astype(BF16))
            ulast_ref[...] = ub[ts - POOL_LOOKBACK:, :]

        if do_head:
            def gate_factor_columns(h):
                def issue():
                    cols = slice(h * hdv, (h + 1) * hdv)
                    g = _dot(xb, wbig_ref[:, _G * d_model + h * hdv:_G * d_model + (h + 1) * hdv])
                    ya_scale_ref[:, cols] = gnw_ref[:, cols] * (g * _sigmoid(g))
                return issue

            def merge_gate_columns(h):
                def issue():
                    cols = slice(_GATE_A * d_model + h * hdv, _GATE_A * d_model + (h + 1) * hdv)
                    gate_a_ref[:, h * hdv:(h + 1) * hdv] = _sigmoid(_dot(xb, wbig_ref[:, cols]))
                return issue

            chunk_states = (state_ref, mid_state_ref, new_state_ref)
            chunk_fillers = (gate_factor_columns, merge_gate_columns)
            for c, rows in enumerate(fast_chunks):
                _gla_chunk_factorised(gla_operands[c], v_ref, o_ref, rows, chunk_states[c], chunk_states[c + 1],
                                      hdk=hdk, hdv=hdv,
                                      fillers=[chunk_fillers[c](h) for h in range(GLA_HEADS)])

        if do_head:
            z = proj(_Z)
            silu_z = z * _sigmoid(z)

        if do_tail:
            r = out_ref[...]
            mu = jnp.mean(r, axis=-1, keepdims=True)
            rc = r - mu
            var = jnp.mean(rc * rc, axis=-1, keepdims=True)
            out_ref[...] = (rc * lax.rsqrt(var + NORM_EPS) * lnw_ref[...] + lnb_ref[...]).astype(out_ref.dtype)

        if do_head:

            p_parts = [_dot(pooled_in[gi], poolw_ref[gs, :]) for gi, gs in enumerate(gslices)]
            gate_b = _sigmoid(proj(_GATE_B))
            y_b = (jnp.concatenate(p_parts, axis=-1) + poolb_ref[...]) * pools_ref[...] * silu_z
            resid_ref[...] = alpha * xf
            merged_b_ref[...] = gate_b * _dot(y_b.astype(BF16), wb_ref[...])

    pl.when(s == 0)(functools.partial(step, True, False))
    pl.when(jnp.logical_and(s > 0, s < n_tiles))(functools.partial(step, True, True))
    pl.when(s == n_tiles)(functools.partial(step, False, True))

    @pl.when(s < n_tiles)
    def _():
        @pl.when(factorisable_ref[0] == 0)
        def _():
            span = GLA_FAST_CHUNK
            row = lax.broadcasted_iota(jnp.int32, (span, span), 0)
            col = lax.broadcasted_iota(jnp.int32, (span, span), 1)
            tri_chunk = jnp.where(row // GLA_CHUNK == col // GLA_CHUNK, tri_ref[...], jnp.zeros((), BF16))
            la_hi, la_lo = _hi_lo(la_ref[...])
            for rows in fast_chunks:
                b_ref[rows, :] = _dot(tri_chunk, la_hi[rows, :]) + _dot(tri_chunk, la_lo[rows, :])
            _gla_tile_direct(qs_ref, k_ref, b_ref, v_ref, o_ref, state_ref, new_state_ref, hdk=hdk, hdv=hdv)

        state_ref[...] = new_state_ref[...]


def _const_spec(shape):
    nd = len(shape)
    return pl.BlockSpec(shape, lambda s: (0,) * nd, pipeline_mode=pl.Buffered(1))


def _tri_matrix():
    r = np.arange(GLA_FAST_CHUNK)
    return jnp.asarray((r[None, :] <= r[:, None]).astype(np.float32), dtype=BF16)


def _band_matrices():
    r = np.arange(GLA_FAST_CHUNK)[:, None]
    c = np.arange(GLA_FAST_CHUNK)[None, :]
    c_lb = np.arange(POOL_LOOKBACK)[None, :] - POOL_LOOKBACK
    in_win = lambda d, w: ((d >= 0) & (d < w)).astype(np.float32)
    bands = np.stack([in_win(r - c, w) for w in POOL_WINDOWS])
    bands_lb = np.stack([in_win(r[:POOL_LOOKBACK] - c_lb, w) for w in POOL_WINDOWS])
    return jnp.asarray(bands, dtype=BF16), jnp.asarray(bands_lb, dtype=BF16)


def _layer(x, w_in_t, w_gate_up, b_gate, gn_w, pool_w, pool_b, pool_scale, w_a, w_b, w_o, ln_w, ln_b,
           *, alpha):
    bsz, seq, d_model = x.shape
    rank, dk = w_gate_up.shape
    dv = gn_w.shape[0]
    assert seq % SEQ_TILE == 0 and SEQ_TILE % GLA_FAST_CHUNK == 0 and GLA_FAST_CHUNK % GLA_CHUNK == 0
    assert dv == d_model and 2 * dk == d_model and rank == GLA_GATE_RANK
    assert pool_w.shape[0] == len(POOL_WINDOWS)
    d_in = w_in_t.shape[0]
    gdim = pool_w.shape[1]
    row = lambda a: a.reshape(1, -1)
    hbm_weights = (w_in_t, w_a, w_b, w_o, pool_w.reshape(d_model, gdim))
    small = (w_gate_up, row(b_gate), row(gn_w), row(pool_b), row(pool_scale), row(ln_w), row(ln_b),
             _tri_matrix(), *_band_matrices())
    operands = (x, *hbm_weights, *small)
    tiles_per_seq = seq // SEQ_TILE
    n_tiles = bsz * tiles_per_seq

    def tile_block(tile):
        return (tile // tiles_per_seq, tile % tiles_per_seq, 0)

    in_specs = [pl.BlockSpec((None, SEQ_TILE, d_model), lambda s: tile_block(jnp.minimum(s, n_tiles - 1)))]
    in_specs += [pl.BlockSpec(memory_space=pl.ANY) for _ in hbm_weights]
    in_specs += [_const_spec(a.shape) for a in small]
    kernel = functools.partial(_block_kernel, d_model=d_model, dk=dk, dv=dv, alpha=alpha,
                               tiles_per_seq=tiles_per_seq, n_tiles=n_tiles)
    state_shape = (GLA_HEADS, dk // GLA_HEADS, dv // GLA_HEADS)
    return pl.pallas_call(
        kernel,
        grid=(n_tiles + 1,),
        in_specs=in_specs,
        out_specs=pl.BlockSpec((None, SEQ_TILE, d_model), lambda s: tile_block(jnp.maximum(s - 1, 0))),
        out_shape=jax.ShapeDtypeStruct(x.shape, x.dtype),
        scratch_shapes=[
            pltpu.VMEM((SEQ_TILE, dk), F32),
            pltpu.VMEM((SEQ_TILE, dk), F32),
            pltpu.VMEM((SEQ_TILE, dk), F32),
            pltpu.VMEM((SEQ_TILE, dv), BF16),
            pltpu.VMEM((SEQ_TILE, dv), F32),
            pltpu.VMEM(state_shape, F32),
            pltpu.VMEM(state_shape, F32),
            pltpu.VMEM(state_shape, F32),
            pltpu.VMEM((POOL_LOOKBACK, d_model), BF16),
            pltpu.VMEM((SEQ_TILE, dv), F32),
            pltpu.VMEM((SEQ_TILE, d_model), F32),
            pltpu.VMEM((SEQ_TILE, d_model), F32),
            pltpu.VMEM((SEQ_TILE, d_model), F32),
            pltpu.VMEM((SEQ_TILE, dk), F32),
            pltpu.SMEM((1,), jnp.int32),
            pltpu.VMEM((d_model, d_in - rank), BF16),
            pltpu.VMEM((d_model, rank), BF16),
            pltpu.VMEM(w_a.shape, BF16),
            pltpu.VMEM(w_b.shape, BF16),
            pltpu.VMEM(w_o.shape, BF16),
            pltpu.VMEM((d_model, gdim), BF16),
            pltpu.VMEM((WEIGHT_STAGE_SLOTS, WEIGHT_LOAD_ROWS, d_model), F32),
            pltpu.VMEM((rank, d_model), F32),
            pltpu.VMEM((d_model, gdim), F32),
            pltpu.SemaphoreType.DMA((WEIGHT_STAGE_SLOTS + 2,)),
        ],
        compiler_params=pltpu.CompilerParams(
            dimension_semantics=("arbitrary",),
            vmem_limit_bytes=VMEM_LIMIT_BYTES),
        name="hybrid_gla_pool_layer",
    )(*operands)


def kernel(x, w_in, w_gate_up, b_gate, gn_w, pool_w, pool_b, pool_scale, w_a, w_b, w_o, ln_w, ln_b):
    depth = w_in.shape[0]
    alpha = (2.0 * depth) ** 0.25
    w_in_t = jnp.swapaxes(w_in, 1, 2)
    for l in range(depth):
        x = _layer(x, w_in_t[l], w_gate_up[l], b_gate[l], gn_w[l], pool_w[l], pool_b[l], pool_scale[l],
                   w_a[l], w_b[l], w_o[l], ln_w[l], ln_b[l], alpha=alpha)
    return x
```

```python
import functools

import numpy as np
import jax
import jax.numpy as jnp
from jax import lax
from jax.experimental import pallas as pl
from jax.experimental.pallas import tpu as pltpu

GLA_HEADS = 4
GLA_GATE_RANK = 16
GLA_GATE_TAU = 16.0
POOL_WINDOWS = (2, 4, 8, 16)
POOL_LOOKBACK = 16
NORM_EPS = 1e-5

SEQ_TILE = 512
GLA_FAST_CHUNK = 256
GLA_CHUNK = 64
WEIGHT_LOAD_ROWS = 256
WEIGHT_STAGE_SLOTS = 4
VMEM_LIMIT_BYTES = 60 * 1024 * 1024
GLA_FACTOR_MAX_DECAY = 50.0

_QK, _V, _G, _U, _Z, _GATE_A, _GATE_B = range(7)

F32 = jnp.float32
BF16 = jnp.bfloat16


def _dot(a, b):
    return jnp.dot(a, b, preferred_element_type=F32)


def _sigmoid(x):
    return 1.0 / (1.0 + jnp.exp(-x))


def _log_sigmoid(x):
    return jnp.minimum(x, 0.0) - jnp.log(1.0 + jnp.exp(-jnp.abs(x)))


def _hi_lo(x):
    hi = x.astype(BF16)
    return hi, (x - hi.astype(F32)).astype(BF16)


def _row_to_col(row):
    return jnp.transpose(jnp.broadcast_to(row, (8, row.shape[1])))[:, 0:1]


def _gla_factor_operands(qs_ref, k_ref, b_ref, rows):
    b = b_ref[rows, :]
    e_pos = jnp.exp(b)
    e_last = e_pos[GLA_FAST_CHUNK - 1:GLA_FAST_CHUNK, :]
    k_neg = k_ref[rows, :] * jnp.exp(-b)
    qd = (qs_ref[rows, :] * e_pos).astype(BF16)
    kn = k_neg.astype(BF16)
    kd = k_neg * e_last
    return qd, kn, kd, e_last


def _gla_chunk_factorised(operands, v_ref, o_ref, rows, state_ref, new_state_ref, *, hdk, hdv, fillers):
    ts = GLA_FAST_CHUNK
    qd, kn, kd, e_last = operands
    heads = range(GLA_HEADS)
    ks = [slice(h * hdk, (h + 1) * hdk) for h in heads]
    vs = [slice(h * hdv, (h + 1) * hdv) for h in heads]
    causal = (lax.broadcasted_iota(jnp.int32, (ts, ts), 0)
              >= lax.broadcasted_iota(jnp.int32, (ts, ts), 1))
    masked = []
    for h in heads:
        scores = lax.dot_general(qd[:, ks[h]], kn[:, ks[h]], (((1,), (1,)), ((), ())),
                                 preferred_element_type=F32)
        fillers[h]()
        masked.append(jnp.where(causal, scores, 0.0).astype(BF16))
    for h in heads:
        kd_t = jnp.transpose(kd[:, ks[h]]).astype(BF16)
        new_state_ref[h] = state_ref[h] * _row_to_col(e_last[:, ks[h]]) + _dot(kd_t, v_ref[rows, vs[h]])
    for h in heads:
        o_ref[rows, vs[h]] = (_dot(qd[:, ks[h]], state_ref[h].astype(BF16))
                              + _dot(masked[h], v_ref[rows, vs[h]]))


def _gla_tile_direct(qs_ref, k_ref, b_ref, v_ref, o_ref, state_ref, new_state_ref, *, hdk, hdv):
    new_state_ref[...] = state_ref[...]
    row_c = lax.broadcasted_iota(jnp.int32, (GLA_CHUNK, 1), 0)
    lane_c = lax.broadcasted_iota(jnp.int32, (GLA_CHUNK, GLA_CHUNK), 1)

    for c in range(SEQ_TILE // GLA_CHUNK):
        r0 = c * GLA_CHUNK
        rows = pl.ds(r0, GLA_CHUNK)
        qs_c = qs_ref[rows, :]
        b_c = b_ref[rows, :]

        def col_body(j, a_heads):
            kj = k_ref[pl.ds(r0 + j, 1), :]
            bj = b_ref[pl.ds(r0 + j, 1), :]
            p = qs_c * kj * jnp.exp(jnp.minimum(b_c - bj, 0.0))
            p = jnp.where(row_c >= j, p, 0.0)
            new = []
            for h in range(GLA_HEADS):
                s = jnp.sum(p[:, h * hdk:(h + 1) * hdk], axis=-1, keepdims=True)
                new.append(jnp.where(lane_c == j, s, a_heads[h]))
            return tuple(new)

        a_heads = lax.fori_loop(
            0, GLA_CHUNK, col_body,
            tuple(jnp.zeros((GLA_CHUNK, GLA_CHUNK), F32) for _ in range(GLA_HEADS)))

        b_last = b_ref[pl.ds(r0 + GLA_CHUNK - 1, 1), :]
        qd = (qs_c * jnp.exp(b_c)).astype(BF16)
        kd = k_ref[rows, :] * jnp.exp(b_last - b_c)
        e_last = jnp.exp(b_last)
        for h in range(GLA_HEADS):
            ks = slice(h * hdk, (h + 1) * hdk)
            vs = slice(h * hdv, (h + 1) * hdv)
            v_ch = v_ref[rows, vs]
            st = new_state_ref[h]
            o_ref[rows, vs] = (_dot(qd[:, ks], st.astype(BF16))
                               + _dot(a_heads[h].astype(BF16), v_ch))
            kd_t = jnp.transpose(kd[:, ks]).astype(BF16)
            new_state_ref[h] = st * _row_to_col(e_last[:, ks]) + _dot(kd_t, v_ch)


def _load_weights(wint_hbm, wa_hbm, wb_hbm, wo_hbm, poolw_hbm,
                  wbig_ref, wal_ref, wa_ref, wb_ref, wo_ref, poolw_ref,
                  stage_ref, gate_stage_ref, pool_stage_ref, sems, *, n_before_gate, rank):
    rows = WEIGHT_LOAD_ROWS
    d_in, d_model = wint_hbm.shape
    assert n_before_gate % rows == 0 and (d_in - rank) % rows == 0 and d_model % rows == 0
    jobs = []
    for j in range((d_in - rank) // rows):
        row0 = j * rows if j * rows < n_before_gate else j * rows + rank
        jobs.append((wint_hbm, row0, wbig_ref.at[:, j * rows:(j + 1) * rows]))
    for src, dst in ((wa_hbm, wa_ref), (wb_hbm, wb_ref), (wo_hbm, wo_ref)):
        for r0 in range(0, d_model, rows):
            jobs.append((src, r0, dst.at[r0:r0 + rows, :]))

    slots = stage_ref.shape[0]

    def staged_copy(i):
        src, row0, _ = jobs[i]
        return pltpu.make_async_copy(src.at[pl.ds(row0, rows), :], stage_ref.at[i % slots], sems.at[i % slots])

    gate_copy = pltpu.make_async_copy(wint_hbm.at[pl.ds(n_before_gate, rank), :], gate_stage_ref, sems.at[slots])
    pool_copy = pltpu.make_async_copy(poolw_hbm, pool_stage_ref, sems.at[slots + 1])
    gate_copy.start()
    pool_copy.start()
    ahead = slots - 1
    for i in range(min(ahead, len(jobs))):
        staged_copy(i).start()
    for i, (src, _, dst) in enumerate(jobs):
        if i + ahead < len(jobs):
            staged_copy(i + ahead).start()
        staged_copy(i).wait()
        block = stage_ref[i % slots]
        dst[...] = (jnp.transpose(block) if src is wint_hbm else block).astype(BF16)
    gate_copy.wait()
    wal_ref[...] = jnp.transpose(gate_stage_ref[...]).astype(BF16)
    pool_copy.wait()
    poolw_ref[...] = pool_stage_ref[...].astype(BF16)


def _block_kernel(x_ref, wint_hbm, wa_hbm, wb_hbm, wo_hbm, poolw_hbm,
                  wgu_ref, bg_ref, gnw_ref, poolb_ref, pools_ref, lnw_ref, lnb_ref,
                  tri_ref, band_ref, bandlb_ref,
                  out_ref,
                  qs_ref, k_ref, b_ref, v_ref, o_ref, state_ref, mid_state_ref, new_state_ref, ulast_ref,
                  ya_scale_ref, gate_a_ref, merged_b_ref, resid_ref, la_ref, factorisable_ref,
                  wbig_ref, wal_ref, wa_ref, wb_ref, wo_ref, poolw_ref,
                  stage_ref, gate_stage_ref, pool_stage_ref, load_sems,
                  *, d_model, dk, dv, alpha, tiles_per_seq, n_tiles):
    ts = SEQ_TILE
    hdk = dk // GLA_HEADS
    hdv = dv // GLA_HEADS
    s = pl.program_id(0)
    t = lax.rem(jnp.minimum(s, n_tiles - 1), tiles_per_seq)
    fast_chunks = [slice(r0, r0 + GLA_FAST_CHUNK) for r0 in range(0, ts, GLA_FAST_CHUNK)]
    assert len(fast_chunks) == 2

    @pl.when(s == 0)
    def _():
        _load_weights(wint_hbm, wa_hbm, wb_hbm, wo_hbm, poolw_hbm,
                      wbig_ref, wal_ref, wa_ref, wb_ref, wo_ref, poolw_ref,
                      stage_ref, gate_stage_ref, pool_stage_ref, load_sems,
                      n_before_gate=2 * dk + 2 * dv, rank=GLA_GATE_RANK)

    @pl.when(t == 0)
    def _():
        state_ref[...] = jnp.zeros_like(state_ref)
        ulast_ref[...] = jnp.zeros_like(ulast_ref)

    def step(do_head, do_tail):
        if do_head:
            xf = x_ref[...]
            xb = xf.astype(BF16)

            def proj(n):
                return _dot(xb, wbig_ref[:, n * d_model:(n + 1) * d_model])

        if do_tail:
            o = o_ref[...]
            y_a_parts = []
            for h in range(GLA_HEADS):
                oh = o[:, h * hdv:(h + 1) * hdv]
                y_a_parts.append(oh * lax.rsqrt(jnp.mean(oh * oh, axis=-1, keepdims=True) + NORM_EPS))
            y_a = (jnp.concatenate(y_a_parts, axis=-1) * ya_scale_ref[...]).astype(BF16)

        if do_head:
            a_low = _dot(xb, wal_ref[...])
            qk = proj(_QK)
            qs_ref[...] = qk[:, :dk] * (hdk ** -0.5)
            k_ref[...] = qk[:, dk:]
            gate_pre = _dot(a_low.astype(BF16), wgu_ref[...].astype(BF16)) + bg_ref[...]
            log_a = _log_sigmoid(gate_pre) * (1.0 / GLA_GATE_TAU)
            la_ref[...] = log_a
            la_hi, la_lo = _hi_lo(log_a)

        if do_tail:
            merged = (gate_a_ref[...] * _dot(y_a, wa_ref[...]) + merged_b_ref[...]).astype(BF16)

        if do_head:
            for rows in fast_chunks:
                b_ref[rows, :] = _dot(tri_ref[...], la_hi[rows, :]) + _dot(tri_ref[...], la_lo[rows, :])
            min_b = functools.reduce(
                jnp.minimum, [jnp.min(b_ref[rows.stop - 1:rows.stop, :]) for rows in fast_chunks])
            factorisable_ref[0] = (min_b >= -GLA_FACTOR_MAX_DECAY).astype(jnp.int32)
            v_ref[...] = proj(_V).astype(BF16)
            gla_operands = [_gla_factor_operands(qs_ref, k_ref, b_ref, rows) for rows in fast_chunks]

        if do_tail:
            out_ref[...] = resid_ref[...] + _dot(merged, wo_ref[...])

        if do_head:
            u = proj(_U)
            ub = u.astype(BF16)
            t_abs = t * ts + lax.broadcasted_iota(jnp.int32, (ts, 1), 0)
            gdim = d_model // len(POOL_WINDOWS)
            gslices = [slice(gi * gdim, (gi + 1) * gdim) for gi in range(len(POOL_WINDOWS))]
            pooled_in = []
            for gi, gs in enumerate(gslices):
                parts = []
                for rows in fast_chunks:
                    look_back = (ulast_ref[:, gs] if rows.start == 0
                                 else ub[rows.start - POOL_LOOKBACK:rows.start, gs])
                    in_chunk = _dot(band_ref[gi], ub[rows, gs])
                    carried = _dot(bandlb_ref[gi], look_back)
                    parts += [in_chunk[:POOL_LOOKBACK] + carried, in_chunk[POOL_LOOKBACK:]]
                count = jnp.minimum(t_abs + 1, POOL_WINDOWS[gi]).astype(F32)
                pooled_in.append((jnp.concatenate(parts, axis=0) / count - u[:, gs]).astype(BF16))
            ulast_ref[...] = ub[ts - POOL_LOOKBACK:, :]

        if do_head:
            def gate_factor_columns(h):
                def issue():
                    cols = slice(h * hdv, (h + 1) * hdv)
                    g = _dot(xb, wbig_ref[:, _G * d_model + h * hdv:_G * d_model + (h + 1) * hdv])
                    ya_scale_ref[:, cols] = gnw_ref[:, cols] * (g * _sigmoid(g))
                return issue

            def merge_gate_columns(h):
                def issue():
                    cols = slice(_GATE_A * d_model + h * hdv, _GATE_A * d_model + (h + 1) * hdv)
                    gate_a_ref[:, h * hdv:(h + 1) * hdv] = _sigmoid(_dot(xb, wbig_ref[:, cols]))
                return issue

            chunk_states = (state_ref, mid_state_ref, new_state_ref)
            chunk_fillers = (gate_factor_columns, merge_gate_columns)
            for c, rows in enumerate(fast_chunks):
                _gla_chunk_factorised(gla_operands[c], v_ref, o_ref, rows, chunk_states[c], chunk_states[c + 1],
                                      hdk=hdk, hdv=hdv,
                                      fillers=[chunk_fillers[c](h) for h in range(GLA_HEADS)])

        if do_head:
            z = proj(_Z)
            silu_z = z * _sigmoid(z)

        if do_tail:
            if do_head:
                gla_done = (jnp.max(o_ref[ts - 8:ts, 0:128]) > -jnp.inf).astype(jnp.int32)
                r = out_ref[pl.ds(pl.multiple_of(jnp.minimum(gla_done, 0) * 8, 8), ts), :]
            else:
                r = out_ref[...]
            mu = jnp.mean(r, axis=-1, keepdims=True)
            rc = r - mu
            var = jnp.mean(rc * rc, axis=-1, keepdims=True)
            out_ref[...] = (rc * lax.rsqrt(var + NORM_EPS) * lnw_ref[...] + lnb_ref[...]).astype(out_ref.dtype)

        if do_head:

            p_parts = [_dot(pooled_in[gi], poolw_ref[gs, :]) for gi, gs in enumerate(gslices)]
            gate_b = _sigmoid(proj(_GATE_B))
            y_b = (jnp.concatenate(p_parts, axis=-1) + poolb_ref[...]) * pools_ref[...] * silu_z
            resid_ref[...] = alpha * xf
            merged_b_ref[...] = gate_b * _dot(y_b.astype(BF16), wb_ref[...])

    pl.when(s == 0)(functools.partial(step, True, False))
    pl.when(jnp.logical_and(s > 0, s < n_tiles))(functools.partial(step, True, True))
    pl.when(s == n_tiles)(functools.partial(step, False, True))

    @pl.when(s < n_tiles)
    def _():
        @pl.when(factorisable_ref[0] == 0)
        def _():
            span = GLA_FAST_CHUNK
            row = lax.broadcasted_iota(jnp.int32, (span, span), 0)
            col = lax.broadcasted_iota(jnp.int32, (span, span), 1)
            tri_chunk = jnp.where(row // GLA_CHUNK == col // GLA_CHUNK, tri_ref[...], jnp.zeros((), BF16))
            la_hi, la_lo = _hi_lo(la_ref[...])
            for rows in fast_chunks:
                b_ref[rows, :] = _dot(tri_chunk, la_hi[rows, :]) + _dot(tri_chunk, la_lo[rows, :])
            _gla_tile_direct(qs_ref, k_ref, b_ref, v_ref, o_ref, state_ref, new_state_ref, hdk=hdk, hdv=hdv)

        state_ref[...] = new_state_ref[...]


def _const_spec(shape):
    nd = len(shape)
    return pl.BlockSpec(shape, lambda s: (0,) * nd, pipeline_mode=pl.Buffered(1))


def _tri_matrix():
    r = np.arange(GLA_FAST_CHUNK)
    return jnp.asarray((r[None, :] <= r[:, None]).astype(np.float32), dtype=BF16)


def _band_matrices():
    r = np.arange(GLA_FAST_CHUNK)[:, None]
    c = np.arange(GLA_FAST_CHUNK)[None, :]
    c_lb = np.arange(POOL_LOOKBACK)[None, :] - POOL_LOOKBACK
    in_win = lambda d, w: ((d >= 0) & (d < w)).astype(np.float32)
    bands = np.stack([in_win(r - c, w) for w in POOL_WINDOWS])
    bands_lb = np.stack([in_win(r[:POOL_LOOKBACK] - c_lb, w) for w in POOL_WINDOWS])
    return jnp.asarray(bands, dtype=BF16), jnp.asarray(bands_lb, dtype=BF16)


def _layer(x, w_in_t, w_gate_up, b_gate, gn_w, pool_w, pool_b, pool_scale, w_a, w_b, w_o, ln_w, ln_b,
           *, alpha):
    bsz, seq, d_model = x.shape
    rank, dk = w_gate_up.shape
    dv = gn_w.shape[0]
    assert seq % SEQ_TILE == 0 and SEQ_TILE % GLA_FAST_CHUNK == 0 and GLA_FAST_CHUNK % GLA_CHUNK == 0
    assert dv == d_model and 2 * dk == d_model and rank == GLA_GATE_RANK
    assert pool_w.shape[0] == len(POOL_WINDOWS)
    d_in = w_in_t.shape[0]
    gdim = pool_w.shape[1]
    row = lambda a: a.reshape(1, -1)
    hbm_weights = (w_in_t, w_a, w_b, w_o, pool_w.reshape(d_model, gdim))
    small = (w_gate_up, row(b_gate), row(gn_w), row(pool_b), row(pool_scale), row(ln_w), row(ln_b),
             _tri_matrix(), *_band_matrices())
    operands = (x, *hbm_weights, *small)
    tiles_per_seq = seq // SEQ_TILE
    n_tiles = bsz * tiles_per_seq

    def tile_block(tile):
        return (tile // tiles_per_seq, tile % tiles_per_seq, 0)

    in_specs = [pl.BlockSpec((None, SEQ_TILE, d_model), lambda s: tile_block(jnp.minimum(s, n_tiles - 1)))]
    in_specs += [pl.BlockSpec(memory_space=pl.ANY) for _ in hbm_weights]
    in_specs += [_const_spec(a.shape) for a in small]
    kernel = functools.partial(_block_kernel, d_model=d_model, dk=dk, dv=dv, alpha=alpha,
                               tiles_per_seq=tiles_per_seq, n_tiles=n_tiles)
    state_shape = (GLA_HEADS, dk // GLA_HEADS, dv // GLA_HEADS)
    return pl.pallas_call(
        kernel,
        grid=(n_tiles + 1,),
        in_specs=in_specs,
        out_specs=pl.BlockSpec((None, SEQ_TILE, d_model), lambda s: tile_block(jnp.maximum(s - 1, 0))),
        out_shape=jax.ShapeDtypeStruct(x.shape, x.dtype),
        scratch_shapes=[
            pltpu.VMEM((SEQ_TILE, dk), F32),
            pltpu.VMEM((SEQ_TILE, dk), F32),
            pltpu.VMEM((SEQ_TILE, dk), F32),
            pltpu.VMEM((SEQ_TILE, dv), BF16),
            pltpu.VMEM((SEQ_TILE, dv), F32),
            pltpu.VMEM(state_shape, F32),
            pltpu.VMEM(state_shape, F32),
            pltpu.VMEM(state_shape, F32),
            pltpu.VMEM((POOL_LOOKBACK, d_model), BF16),
            pltpu.VMEM((SEQ_TILE, dv), F32),
            pltpu.VMEM((SEQ_TILE, d_model), F32),
            pltpu.VMEM((SEQ_TILE, d_model), F32),
            pltpu.VMEM((SEQ_TILE, d_model), F32),
            pltpu.VMEM((SEQ_TILE, dk), F32),
            pltpu.SMEM((1,), jnp.int32),
            pltpu.VMEM((d_model, d_in - rank), BF16),
            pltpu.VMEM((d_model, rank), BF16),
            pltpu.VMEM(w_a.shape, BF16),
            pltpu.VMEM(w_b.shape, BF16),
            pltpu.VMEM(w_o.shape, BF16),
            pltpu.VMEM((d_model, gdim), BF16),
            pltpu.VMEM((WEIGHT_STAGE_SLOTS, WEIGHT_LOAD_ROWS, d_model), F32),
            pltpu.VMEM((rank, d_model), F32),
            pltpu.VMEM((d_model, gdim), F32),
            pltpu.SemaphoreType.DMA((WEIGHT_STAGE_SLOTS + 2,)),
        ],
        compiler_params=pltpu.CompilerParams(
            dimension_semantics=("arbitrary",),
            vmem_limit_bytes=VMEM_LIMIT_BYTES),
        name="hybrid_gla_pool_layer",
    )(*operands)


def kernel(x, w_in, w_gate_up, b_gate, gn_w, pool_w, pool_b, pool_scale, w_a, w_b, w_o, ln_w, ln_b):
    depth = w_in.shape[0]
    alpha = (2.0 * depth) ** 0.25
    w_in_t = jnp.swapaxes(w_in, 1, 2)
    for l in range(depth):
        x = _layer(x, w_in_t[l], w_gate_up[l], b_gate[l], gn_w[l], pool_w[l], pool_b[l], pool_scale[l],
                   w_a[l], w_b[l], w_o[l], ln_w[l], ln_b[l], alpha=alpha)
    return x
```

```python
import functools

import numpy as np
import jax
import jax.numpy as jnp
from jax import lax
from jax.experimental import pallas as pl
from jax.experimental.pallas import tpu as pltpu

GLA_HEADS = 4
GLA_GATE_RANK = 16
GLA_GATE_TAU = 16.0
POOL_WINDOWS = (2, 4, 8, 16)
POOL_LOOKBACK = 16
NORM_EPS = 1e-5

SEQ_TILE = 512
GLA_FAST_CHUNK = 256
GLA_CHUNK = 64
WEIGHT_LOAD_ROWS = 256
WEIGHT_STAGE_SLOTS = 4
VMEM_LIMIT_BYTES = 60 * 1024 * 1024
GLA_FACTOR_MAX_DECAY = 50.0

_QK, _V, _G, _U, _Z, _GATE_A, _GATE_B = range(7)

F32 = jnp.float32
BF16 = jnp.bfloat16
F32_SUBLANES = 8


def _dot(a, b):
    return jnp.dot(a, b, preferred_element_type=F32)


def _sigmoid(x):
    return 1.0 / (1.0 + jnp.exp(-x))


def _log_sigmoid(x):
    return jnp.minimum(x, 0.0) - jnp.log(1.0 + jnp.exp(-jnp.abs(x)))


def _hi_lo(x):
    hi = x.astype(BF16)
    return hi, (x - hi.astype(F32)).astype(BF16)


def _row_to_col(row):
    return jnp.transpose(jnp.broadcast_to(row, (F32_SUBLANES, row.shape[1])))[:, 0:1]


def _gla_factor_operands(qs_ref, k_ref, b_ref, rows):
    b = b_ref[rows, :]
    e_pos = jnp.exp(b)
    e_last = e_pos[GLA_FAST_CHUNK - 1:GLA_FAST_CHUNK, :]
    k_neg = k_ref[rows, :] * jnp.exp(-b)
    qd = (qs_ref[rows, :] * e_pos).astype(BF16)
    kn = k_neg.astype(BF16)
    kd = k_neg * e_last
    return qd, kn, kd, e_last


def _gla_chunk_factorised(operands, v_ref, o_ref, rows, state_ref, new_state_ref, *, hdk, hdv, fillers):
    ts = GLA_FAST_CHUNK
    qd, kn, kd, e_last = operands
    heads = range(GLA_HEADS)
    ks = [slice(h * hdk, (h + 1) * hdk) for h in heads]
    vs = [slice(h * hdv, (h + 1) * hdv) for h in heads]
    causal = (lax.broadcasted_iota(jnp.int32, (ts, ts), 0)
              >= lax.broadcasted_iota(jnp.int32, (ts, ts), 1))
    masked = []
    for h in heads:
        scores = lax.dot_general(qd[:, ks[h]], kn[:, ks[h]], (((1,), (1,)), ((), ())),
                                 preferred_element_type=F32)
        fillers[h]()
        masked.append(jnp.where(causal, scores, 0.0).astype(BF16))
    for h in heads:
        kd_t = jnp.transpose(kd[:, ks[h]]).astype(BF16)
        new_state_ref[h] = state_ref[h] * _row_to_col(e_last[:, ks[h]]) + _dot(kd_t, v_ref[rows, vs[h]])
    for h in heads:
        o_ref[rows, vs[h]] = (_dot(qd[:, ks[h]], state_ref[h].astype(BF16))
                              + _dot(masked[h], v_ref[rows, vs[h]]))


def _gla_tile_direct(qs_ref, k_ref, b_ref, v_ref, o_ref, state_ref, new_state_ref, *, hdk, hdv):
    new_state_ref[...] = state_ref[...]
    row_c = lax.broadcasted_iota(jnp.int32, (GLA_CHUNK, 1), 0)
    lane_c = lax.broadcasted_iota(jnp.int32, (GLA_CHUNK, GLA_CHUNK), 1)

    for c in range(SEQ_TILE // GLA_CHUNK):
        r0 = c * GLA_CHUNK
        rows = pl.ds(r0, GLA_CHUNK)
        qs_c = qs_ref[rows, :]
        b_c = b_ref[rows, :]

        def col_body(j, a_heads):
            kj = k_ref[pl.ds(r0 + j, 1), :]
            bj = b_ref[pl.ds(r0 + j, 1), :]
            p = qs_c * kj * jnp.exp(jnp.minimum(b_c - bj, 0.0))
            p = jnp.where(row_c >= j, p, 0.0)
            new = []
            for h in range(GLA_HEADS):
                s = jnp.sum(p[:, h * hdk:(h + 1) * hdk], axis=-1, keepdims=True)
                new.append(jnp.where(lane_c == j, s, a_heads[h]))
            return tuple(new)

        a_heads = lax.fori_loop(
            0, GLA_CHUNK, col_body,
            tuple(jnp.zeros((GLA_CHUNK, GLA_CHUNK), F32) for _ in range(GLA_HEADS)))

        b_last = b_ref[pl.ds(r0 + GLA_CHUNK - 1, 1), :]
        qd = (qs_c * jnp.exp(b_c)).astype(BF16)
        kd = k_ref[rows, :] * jnp.exp(b_last - b_c)
        e_last = jnp.exp(b_last)
        for h in range(GLA_HEADS):
            ks = slice(h * hdk, (h + 1) * hdk)
            vs = slice(h * hdv, (h + 1) * hdv)
            v_ch = v_ref[rows, vs]
            st = new_state_ref[h]
            o_ref[rows, vs] = (_dot(qd[:, ks], st.astype(BF16))
                               + _dot(a_heads[h].astype(BF16), v_ch))
            kd_t = jnp.transpose(kd[:, ks]).astype(BF16)
            new_state_ref[h] = st * _row_to_col(e_last[:, ks]) + _dot(kd_t, v_ch)


def _load_weights(wint_hbm, wa_hbm, wb_hbm, wo_hbm, poolw_hbm,
                  wbig_ref, wal_ref, wa_ref, wb_ref, wo_ref, poolw_ref,
                  stage_ref, gate_stage_ref, pool_stage_ref, sems, *, n_before_gate, rank):
    rows = WEIGHT_LOAD_ROWS
    d_in, d_model = wint_hbm.shape
    assert n_before_gate % rows == 0 and (d_in - rank) % rows == 0 and d_model % rows == 0
    jobs = []
    for j in range((d_in - rank) // rows):
        row0 = j * rows if j * rows < n_before_gate else j * rows + rank
        jobs.append((wint_hbm, row0, wbig_ref.at[:, j * rows:(j + 1) * rows]))
    for src, dst in ((wa_hbm, wa_ref), (wb_hbm, wb_ref), (wo_hbm, wo_ref)):
        for r0 in range(0, d_model, rows):
            jobs.append((src, r0, dst.at[r0:r0 + rows, :]))

    slots = stage_ref.shape[0]

    def staged_copy(i):
        src, row0, _ = jobs[i]
        return pltpu.make_async_copy(src.at[pl.ds(row0, rows), :], stage_ref.at[i % slots], sems.at[i % slots])

    gate_copy = pltpu.make_async_copy(wint_hbm.at[pl.ds(n_before_gate, rank), :], gate_stage_ref, sems.at[slots])
    pool_copy = pltpu.make_async_copy(poolw_hbm, pool_stage_ref, sems.at[slots + 1])
    gate_copy.start()
    pool_copy.start()
    ahead = slots - 1
    for i in range(min(ahead, len(jobs))):
        staged_copy(i).start()
    for i, (src, _, dst) in enumerate(jobs):
        if i + ahead < len(jobs):
            staged_copy(i + ahead).start()
        staged_copy(i).wait()
        block = stage_ref[i % slots]
        dst[...] = (jnp.transpose(block) if src is wint_hbm else block).astype(BF16)
    gate_copy.wait()
    wal_ref[...] = jnp.transpose(gate_stage_ref[...]).astype(BF16)
    pool_copy.wait()
    poolw_ref[...] = pool_stage_ref[...].astype(BF16)


def _block_kernel(x_ref, wint_hbm, wa_hbm, wb_hbm, wo_hbm, poolw_hbm,
                  wgu_ref, bg_ref, gnw_ref, poolb_ref, pools_ref, lnw_ref, lnb_ref,
                  tri_ref, band_ref, bandlb_ref,
                  out_ref,
                  qs_ref, k_ref, b_ref, v_ref, o_ref, state_ref, mid_state_ref, new_state_ref, ulast_ref,
                  ya_scale_ref, gate_a_ref, merged_b_ref, resid_ref, la_ref, factorisable_ref,
                  wbig_ref, wal_ref, wa_ref, wb_ref, wo_ref, poolw_ref,
                  stage_ref, gate_stage_ref, pool_stage_ref, load_sems,
                  *, d_model, dk, dv, alpha, tiles_per_seq, n_tiles):
    ts = SEQ_TILE
    hdk = dk // GLA_HEADS
    hdv = dv // GLA_HEADS
    s = pl.program_id(0)
    t = lax.rem(jnp.minimum(s, n_tiles - 1), tiles_per_seq)
    fast_chunks = [slice(r0, r0 + GLA_FAST_CHUNK) for r0 in range(0, ts, GLA_FAST_CHUNK)]
    assert len(fast_chunks) == 2

    @pl.when(s == 0)
    def _():
        _load_weights(wint_hbm, wa_hbm, wb_hbm, wo_hbm, poolw_hbm,
                      wbig_ref, wal_ref, wa_ref, wb_ref, wo_ref, poolw_ref,
                      stage_ref, gate_stage_ref, pool_stage_ref, load_sems,
                      n_before_gate=2 * dk + 2 * dv, rank=GLA_GATE_RANK)

    @pl.when(t == 0)
    def _():
        state_ref[...] = jnp.zeros_like(state_ref)
        ulast_ref[...] = jnp.zeros_like(ulast_ref)

    def step(do_head, do_tail):
        if do_head:
            xf = x_ref[...]
            xb = xf.astype(BF16)

            def proj(n):
                return _dot(xb, wbig_ref[:, n * d_model:(n + 1) * d_model])

        if do_tail:
            o = o_ref[...]
            y_a_parts = []
            for h in range(GLA_HEADS):
                oh = o[:, h * hdv:(h + 1) * hdv]
                y_a_parts.append(oh * lax.rsqrt(jnp.mean(oh * oh, axis=-1, keepdims=True) + NORM_EPS))
            y_a = (jnp.concatenate(y_a_parts, axis=-1) * ya_scale_ref[...]).astype(BF16)

        if do_head:
            a_low = _dot(xb, wal_ref[...])
            qk = proj(_QK)
            qs_ref[...] = qk[:, :dk] * (hdk ** -0.5)
            k_ref[...] = qk[:, dk:]
            gate_pre = _dot(a_low.astype(BF16), wgu_ref[...].astype(BF16)) + bg_ref[...]
            log_a = _log_sigmoid(gate_pre) * (1.0 / GLA_GATE_TAU)
            la_ref[...] = log_a
            la_hi, la_lo = _hi_lo(log_a)

        if do_tail:
            merged = (gate_a_ref[...] * _dot(y_a, wa_ref[...]) + merged_b_ref[...]).astype(BF16)

        if do_head:
            for rows in fast_chunks:
                b_ref[rows, :] = _dot(tri_ref[...], la_hi[rows, :]) + _dot(tri_ref[...], la_lo[rows, :])
            min_b = functools.reduce(
                jnp.minimum, [jnp.min(b_ref[rows.stop - 1:rows.stop, :]) for rows in fast_chunks])
            factorisable_ref[0] = (min_b >= -GLA_FACTOR_MAX_DECAY).astype(jnp.int32)
            v_ref[...] = proj(_V).astype(BF16)
            gla_operands = [_gla_factor_operands(qs_ref, k_ref, b_ref, rows) for rows in fast_chunks]

        if do_tail:
            out_ref[...] = resid_ref[...] + _dot(merged, wo_ref[...])

        if do_head:
            u = proj(_U)
            ub = u.astype(BF16)
            t_abs = t * ts + lax.broadcasted_iota(jnp.int32, (ts, 1), 0)
            gdim = d_model // len(POOL_WINDOWS)
            gslices = [slice(gi * gdim, (gi + 1) * gdim) for gi in range(len(POOL_WINDOWS))]
            pooled_in = []
            for gi, gs in enumerate(gslices):
                parts = []
                for rows in fast_chunks:
                    look_back = (ulast_ref[:, gs] if rows.start == 0
                                 else ub[rows.start - POOL_LOOKBACK:rows.start, gs])
                    in_chunk = _dot(band_ref[gi], ub[rows, gs])
                    carried = _dot(bandlb_ref[gi], look_back)
                    parts += [in_chunk[:POOL_LOOKBACK] + carried, in_chunk[POOL_LOOKBACK:]]
                count = jnp.minimum(t_abs + 1, POOL_WINDOWS[gi]).astype(F32)
                pooled_in.append((jnp.concatenate(parts, axis=0) / count - u[:, gs]).astype(BF16))
            ulast_ref[...] = ub[ts - POOL_LOOKBACK:, :]

        if do_head:
            def gate_factor_columns(h):
                def issue():
                    cols = slice(h * hdv, (h + 1) * hdv)
                    g = _dot(xb, wbig_ref[:, _G * d_model + h * hdv:_G * d_model + (h + 1) * hdv])
                    ya_scale_ref[:, cols] = gnw_ref[:, cols] * (g * _sigmoid(g))
                return issue

            def merge_gate_columns(h):
                def issue():
                    cols = slice(_GATE_A * d_model + h * hdv, _GATE_A * d_model + (h + 1) * hdv)
                    gate_a_ref[:, h * hdv:(h + 1) * hdv] = _sigmoid(_dot(xb, wbig_ref[:, cols]))
                return issue

            chunk_states = (state_ref, mid_state_ref, new_state_ref)
            chunk_fillers = (gate_factor_columns, merge_gate_columns)
            for c, rows in enumerate(fast_chunks):
                _gla_chunk_factorised(gla_operands[c], v_ref, o_ref, rows, chunk_states[c], chunk_states[c + 1],
                                      hdk=hdk, hdv=hdv,
                                      fillers=[chunk_fillers[c](h) for h in range(GLA_HEADS)])

        if do_head:
            z = proj(_Z)
            silu_z = z * _sigmoid(z)

        if do_tail:
            r = out_ref[...]
            mu = jnp.mean(r, axis=-1, keepdims=True)
            rc = r - mu
            var = jnp.mean(rc * rc, axis=-1, keepdims=True)
            out_ref[...] = (rc * lax.rsqrt(var + NORM_EPS) * lnw_ref[...] + lnb_ref[...]).astype(out_ref.dtype)

        if do_head:
            p_parts =[_dot(pooled_in[gi], poolw_ref[gs, :]) for gi, gs in enumerate(gslices)]
            gate_b = _sigmoid(proj(_GATE_B))
            y_b = (jnp.concatenate(p_parts, axis=-1) + poolb_ref[...]) * pools_ref[...] * silu_z
            resid_ref[...] = alpha * xf
            merged_b_ref[...] = gate_b * _dot(y_b.astype(BF16), wb_ref[...])

    pl.when(s == 0)(functools.partial(step, True, False))
    pl.when(jnp.logical_and(s > 0, s < n_tiles))(functools.partial(step, True, True))
    pl.when(s == n_tiles)(functools.partial(step, False, True))

    @pl.when(s < n_tiles)
    def _():
        @pl.when(factorisable_ref[0] == 0)
        def _():
            span = GLA_FAST_CHUNK
            row = lax.broadcasted_iota(jnp.int32, (span, span), 0)
            col = lax.broadcasted_iota(jnp.int32, (span, span), 1)
            tri_chunk = jnp.where(row // GLA_CHUNK == col // GLA_CHUNK, tri_ref[...], jnp.zeros((), BF16))
            la_hi, la_lo = _hi_lo(la_ref[...])
            for rows in fast_chunks:
                b_ref[rows, :] = _dot(tri_chunk, la_hi[rows, :]) + _dot(tri_chunk, la_lo[rows, :])
            _gla_tile_direct(qs_ref, k_ref, b_ref, v_ref, o_ref, state_ref, new_state_ref, hdk=hdk, hdv=hdv)

        state_ref[...] = new_state_ref[...]


def _const_spec(shape):
    nd = len(shape)
    return pl.BlockSpec(shape, lambda s: (0,) * nd, pipeline_mode=pl.Buffered(1))


def _tri_matrix():
    r = np.arange(GLA_FAST_CHUNK)
    return jnp.asarray((r[None, :] <= r[:, None]).astype(np.float32), dtype=BF16)


def _band_matrices():
    r = np.arange(GLA_FAST_CHUNK)[:, None]
    c = np.arange(GLA_FAST_CHUNK)[None, :]
    c_lb = np.arange(POOL_LOOKBACK)[None, :] - POOL_LOOKBACK
    in_win = lambda d, w: ((d >= 0) & (d < w)).astype(np.float32)
    bands = np.stack([in_win(r - c, w) for w in POOL_WINDOWS])
    bands_lb = np.stack([in_win(r[:POOL_LOOKBACK] - c_lb, w) for w in POOL_WINDOWS])
    return jnp.asarray(bands, dtype=BF16), jnp.asarray(bands_lb, dtype=BF16)


def _layer(x, w_in_t, w_gate_up, b_gate, gn_w, pool_w, pool_b, pool_scale, w_a, w_b, w_o, ln_w, ln_b,
           *, alpha):
    bsz, seq, d_model = x.shape
    rank, dk = w_gate_up.shape
    dv = gn_w.shape[0]
    assert seq % SEQ_TILE == 0 and SEQ_TILE % GLA_FAST_CHUNK == 0 and GLA_FAST_CHUNK % GLA_CHUNK == 0
    assert dv == d_model and 2 * dk == d_model and rank == GLA_GATE_RANK
    assert pool_w.shape[0] == len(POOL_WINDOWS)
    d_in = w_in_t.shape[0]
    gdim = pool_w.shape[1]
    row = lambda a: a.reshape(1, -1)
    hbm_weights = (w_in_t, w_a, w_b, w_o, pool_w.reshape(d_model, gdim))
    small = (w_gate_up, row(b_gate), row(gn_w), row(pool_b), row(pool_scale), row(ln_w), row(ln_b),
             _tri_matrix(), *_band_matrices())
    operands = (x, *hbm_weights, *small)
    tiles_per_seq = seq // SEQ_TILE
    n_tiles = bsz * tiles_per_seq

    def tile_block(tile):
        return (tile // tiles_per_seq, tile % tiles_per_seq, 0)

    in_specs = [pl.BlockSpec((None, SEQ_TILE, d_model), lambda s: tile_block(jnp.minimum(s, n_tiles - 1)))]
    in_specs += [pl.BlockSpec(memory_space=pl.ANY) for _ in hbm_weights]
    in_specs += [_const_spec(a.shape) for a in small]
    kernel = functools.partial(_block_kernel, d_model=d_model, dk=dk, dv=dv, alpha=alpha,
                               tiles_per_seq=tiles_per_seq, n_tiles=n_tiles)
    state_shape = (GLA_HEADS, dk // GLA_HEADS, dv // GLA_HEADS)
    return pl.pallas_call(
        kernel,
        grid=(n_tiles + 1,),
        in_specs=in_specs,
        out_specs=pl.BlockSpec((None, SEQ_TILE, d_model), lambda s: tile_block(jnp.maximum(s - 1, 0))),
        out_shape=jax.ShapeDtypeStruct(x.shape, x.dtype),
        scratch_shapes=[
            pltpu.VMEM((SEQ_TILE, dk), F32),
            pltpu.VMEM((SEQ_TILE, dk), F32),
            pltpu.VMEM((SEQ_TILE, dk), F32),
            pltpu.VMEM((SEQ_TILE, dv), BF16),
            pltpu.VMEM((SEQ_TILE, dv), F32),
            pltpu.VMEM(state_shape, F32),
            pltpu.VMEM(state_shape, F32),
            pltpu.VMEM(state_shape, F32),
            pltpu.VMEM((POOL_LOOKBACK, d_model), BF16),
            pltpu.VMEM((SEQ_TILE, dv), F32),
            pltpu.VMEM((SEQ_TILE, d_model), F32),
            pltpu.VMEM((SEQ_TILE, d_model), F32),
            pltpu.VMEM((SEQ_TILE, d_model), F32),
            pltpu.VMEM((SEQ_TILE, dk), F32),
            pltpu.SMEM((1,), jnp.int32),
            pltpu.VMEM((d_model, d_in - rank), BF16),
            pltpu.VMEM((d_model, rank), BF16),
            pltpu.VMEM(w_a.shape, BF16),
            pltpu.VMEM(w_b.shape, BF16),
            pltpu.VMEM(w_o.shape, BF16),
            pltpu.VMEM((d_model, gdim), BF16),
            pltpu.VMEM((WEIGHT_STAGE_SLOTS, WEIGHT_LOAD_ROWS, d_model), F32),
            pltpu.VMEM((rank, d_model), F32),
            pltpu.VMEM((d_model, gdim), F32),
            pltpu.SemaphoreType.DMA((WEIGHT_STAGE_SLOTS + 2,)),
        ],
        compiler_params=pltpu.CompilerParams(
            dimension_semantics=("arbitrary",),
            vmem_limit_bytes=VMEM_LIMIT_BYTES),
        name="hybrid_gla_pool_layer",
    )(*operands)


def kernel(x, w_in, w_gate_up, b_gate, gn_w, pool_w, pool_b, pool_scale, w_a, w_b, w_o, ln_w, ln_b):
    depth = w_in.shape[0]
    alpha = (2.0 * depth) ** 0.25
    w_in_t = jnp.swapaxes(w_in, 1, 2)
    for l in range(depth):
        x = _layer(x, w_in_t[l], w_gate_up[l], b_gate[l], gn_w[l], pool_w[l], pool_b[l], pool_scale[l],
                   w_a[l], w_b[l], w_o[l], ln_w[l], ln_b[l], alpha=alpha)
    return x
```

```python
import functools

import numpy as np
import jax
import jax.numpy as jnp
from jax import lax
from jax.experimental import pallas as pl
from jax.experimental.pallas import tpu as pltpu

GLA_HEADS = 4
GLA_GATE_RANK = 16
GLA_GATE_TAU = 16.0
POOL_WINDOWS = (2, 4, 8, 16)
POOL_LOOKBACK = 16
NORM_EPS = 1e-5

SEQ_TILE = 512
GLA_FAST_CHUNK = 256
GLA_CHUNK = 64
WEIGHT_LOAD_ROWS = 256
WEIGHT_STAGE_SLOTS = 4
VMEM_LIMIT_BYTES = 60 * 1024 * 1024
GLA_FACTOR_MAX_DECAY = 50.0

_QK, _V, _G, _U, _Z, _GATE_A, _GATE_B = range(7)

F32 = jnp.float32
BF16 = jnp.bfloat16
F32_SUBLANES = 8


def _dot(a, b):
    return jnp.dot(a, b, preferred_element_type=F32)


def _sigmoid(x):
    return 1.0 / (1.0 + jnp.exp(-x))


def _log_sigmoid(x):
    return jnp.minimum(x, 0.0) - jnp.log(1.0 + jnp.exp(-jnp.abs(x)))


def _hi_lo(x):
    hi = x.astype(BF16)
    return hi, (x - hi.astype(F32)).astype(BF16)


def _row_to_col(row):
    return jnp.transpose(jnp.broadcast_to(row, (F32_SUBLANES, row.shape[1])))[:, 0:1]


def _gla_factor_operands(qs_ref, k_ref, b_ref, rows):
    b = b_ref[rows, :]
    e_pos = jnp.exp(b)
    e_last = e_pos[GLA_FAST_CHUNK - 1:GLA_FAST_CHUNK, :]
    k_neg = k_ref[rows, :] * jnp.exp(-b)
    qd = (qs_ref[rows, :] * e_pos).astype(BF16)
    kn = k_neg.astype(BF16)
    kd = k_neg * e_last
    return qd, kn, kd, e_last


def _gla_chunk_factorised(operands, v_ref, o_ref, rows, state_ref, new_state_ref, *, hdk, hdv, fillers):
    ts = GLA_FAST_CHUNK
    qd, kn, kd, e_last = operands
    heads = range(GLA_HEADS)
    ks = [slice(h * hdk, (h + 1) * hdk) for h in heads]
    vs = [slice(h * hdv, (h + 1) * hdv) for h in heads]
    causal = (lax.broadcasted_iota(jnp.int32, (ts, ts), 0)
              >= lax.broadcasted_iota(jnp.int32, (ts, ts), 1))
    masked = []
    for h in heads:
        scores = lax.dot_general(qd[:, ks[h]], kn[:, ks[h]], (((1,), (1,)), ((), ())),
                                 preferred_element_type=F32)
        fillers[h]()
        masked.append(jnp.where(causal, scores, 0.0).astype(BF16))
    for h in heads:
        kd_t = jnp.transpose(kd[:, ks[h]]).astype(BF16)
        new_state_ref[h] = state_ref[h] * _row_to_col(e_last[:, ks[h]]) + _dot(kd_t, v_ref[rows, vs[h]])
    for h in heads:
        o_ref[rows, vs[h]] = (_dot(qd[:, ks[h]], state_ref[h].astype(BF16))
                              + _dot(masked[h], v_ref[rows, vs[h]]))


def _gla_tile_direct(qs_ref, k_ref, b_ref, v_ref, o_ref, state_ref, new_state_ref, *, hdk, hdv):
    new_state_ref[...] = state_ref[...]
    row_c = lax.broadcasted_iota(jnp.int32, (GLA_CHUNK, 1), 0)
    lane_c = lax.broadcasted_iota(jnp.int32, (GLA_CHUNK, GLA_CHUNK), 1)

    for c in range(SEQ_TILE // GLA_CHUNK):
        r0 = c * GLA_CHUNK
        rows = pl.ds(r0, GLA_CHUNK)
        qs_c = qs_ref[rows, :]
        b_c = b_ref[rows, :]

        def col_body(j, a_heads):
            kj = k_ref[pl.ds(r0 + j, 1), :]
            bj = b_ref[pl.ds(r0 + j, 1), :]
            p = qs_c * kj * jnp.exp(jnp.minimum(b_c - bj, 0.0))
            p = jnp.where(row_c >= j, p, 0.0)
            new = []
            for h in range(GLA_HEADS):
                s = jnp.sum(p[:, h * hdk:(h + 1) * hdk], axis=-1, keepdims=True)
                new.append(jnp.where(lane_c == j, s, a_heads[h]))
            return tuple(new)

        a_heads = lax.fori_loop(
            0, GLA_CHUNK, col_body,
            tuple(jnp.zeros((GLA_CHUNK, GLA_CHUNK), F32) for _ in range(GLA_HEADS)))

        b_last = b_ref[pl.ds(r0 + GLA_CHUNK - 1, 1), :]
        qd = (qs_c * jnp.exp(b_c)).astype(BF16)
        kd = k_ref[rows, :] * jnp.exp(b_last - b_c)
        e_last = jnp.exp(b_last)
        for h in range(GLA_HEADS):
            ks = slice(h * hdk, (h + 1) * hdk)
            vs = slice(h * hdv, (h + 1) * hdv)
            v_ch = v_ref[rows, vs]
            st = new_state_ref[h]
            o_ref[rows, vs] = (_dot(qd[:, ks], st.astype(BF16))
                               + _dot(a_heads[h].astype(BF16), v_ch))
            kd_t = jnp.transpose(kd[:, ks]).astype(BF16)
            new_state_ref[h] = st * _row_to_col(e_last[:, ks]) + _dot(kd_t, v_ch)


def _load_weights(wint_hbm, wa_hbm, wb_hbm, wo_hbm, poolw_hbm,
                  wbig_ref, walt_ref, wa_ref, wb_ref, wo_ref, poolw_ref,
                  stage_ref, gate_stage_ref, pool_stage_ref, sems, *, n_before_gate, rank):
    rows = WEIGHT_LOAD_ROWS
    d_in, d_model = wint_hbm.shape
    assert n_before_gate % rows == 0 and (d_in - rank) % rows == 0 and d_model % rows == 0
    jobs = []
    for j in range((d_in - rank) // rows):
        row0 = j * rows if j * rows < n_before_gate else j * rows + rank
        jobs.append((wint_hbm, row0, wbig_ref.at[:, j * rows:(j + 1) * rows]))
    for src, dst in ((wa_hbm, wa_ref), (wb_hbm, wb_ref), (wo_hbm, wo_ref)):
        for r0 in range(0, d_model, rows):
            jobs.append((src, r0, dst.at[r0:r0 + rows, :]))

    slots = stage_ref.shape[0]

    def staged_copy(i):
        src, row0, _ = jobs[i]
        return pltpu.make_async_copy(src.at[pl.ds(row0, rows), :], stage_ref.at[i % slots], sems.at[i % slots])

    gate_copy = pltpu.make_async_copy(wint_hbm.at[pl.ds(n_before_gate, rank), :], gate_stage_ref, sems.at[slots])
    pool_copy = pltpu.make_async_copy(poolw_hbm, pool_stage_ref, sems.at[slots + 1])
    gate_copy.start()
    pool_copy.start()
    ahead = slots - 1
    for i in range(min(ahead, len(jobs))):
        staged_copy(i).start()
    for i, (src, _, dst) in enumerate(jobs):
        if i + ahead < len(jobs):
            staged_copy(i + ahead).start()
        staged_copy(i).wait()
        block = stage_ref[i % slots]
        dst[...] = (jnp.transpose(block) if src is wint_hbm else block).astype(BF16)
    gate_copy.wait()
    walt_ref[...] = gate_stage_ref[...].astype(BF16)
    pool_copy.wait()
    poolw_ref[...] = pool_stage_ref[...].astype(BF16)


def _block_kernel(x_ref, wint_hbm, wa_hbm, wb_hbm, wo_hbm, poolw_hbm,
                  wgu_ref, bg_ref, gnw_ref, poolb_ref, pools_ref, lnw_ref, lnb_ref,
                  tri_ref, band_ref, bandlb_ref,
                  out_ref,
                  qs_ref, k_ref, b_ref, v_ref, o_ref, state_ref, mid_state_ref, new_state_ref, ulast_ref,
                  ya_scale_ref, gate_a_ref, merged_b_ref, resid_ref, la_ref, factorisable_ref,
                  wbig_ref, walt_ref, wa_ref, wb_ref, wo_ref, poolw_ref,
                  stage_ref, gate_stage_ref, pool_stage_ref, load_sems,
                  *, d_model, dk, dv, alpha, tiles_per_seq, n_tiles):
    ts = SEQ_TILE
    hdk = dk // GLA_HEADS
    hdv = dv // GLA_HEADS
    s = pl.program_id(0)
    t = lax.rem(jnp.minimum(s, n_tiles - 1), tiles_per_seq)
    fast_chunks = [slice(r0, r0 + GLA_FAST_CHUNK) for r0 in range(0, ts, GLA_FAST_CHUNK)]
    assert len(fast_chunks) == 2

    @pl.when(s == 0)
    def _():
        _load_weights(wint_hbm, wa_hbm, wb_hbm, wo_hbm, poolw_hbm,
                      wbig_ref, walt_ref, wa_ref, wb_ref, wo_ref, poolw_ref,
                      stage_ref, gate_stage_ref, pool_stage_ref, load_sems,
                      n_before_gate=2 * dk + 2 * dv, rank=GLA_GATE_RANK)

    @pl.when(t == 0)
    def _():
        state_ref[...] = jnp.zeros_like(state_ref)
        ulast_ref[...] = jnp.zeros_like(ulast_ref)

    def step(do_head, do_tail):
        if do_head:
            xf = x_ref[...]
            xb = xf.astype(BF16)

            def proj(n):
                return _dot(xb, wbig_ref[:, n * d_model:(n + 1) * d_model])

        if do_tail:
            o = o_ref[...]
            y_a_parts = []
            for h in range(GLA_HEADS):
                oh = o[:, h * hdv:(h + 1) * hdv]
                y_a_parts.append(oh * lax.rsqrt(jnp.mean(oh * oh, axis=-1, keepdims=True) + NORM_EPS))
            y_a = (jnp.concatenate(y_a_parts, axis=-1) * ya_scale_ref[...]).astype(BF16)

        if do_head:
            a_low = jnp.transpose(lax.dot_general(walt_ref[...], xb, (((1,), (1,)), ((), ())),
                                                  preferred_element_type=F32))
            qk = proj(_QK)
            qs_ref[...] = qk[:, :dk] * (hdk ** -0.5)
            k_ref[...] = qk[:, dk:]
            gate_pre = _dot(a_low.astype(BF16), wgu_ref[...].astype(BF16)) + bg_ref[...]
            log_a = _log_sigmoid(gate_pre) * (1.0 / GLA_GATE_TAU)
            la_ref[...] = log_a
            la_hi, la_lo = _hi_lo(log_a)

        if do_tail:
            merged = (gate_a_ref[...] * _dot(y_a, wa_ref[...]) + merged_b_ref[...]).astype(BF16)

        if do_head:
            for rows in fast_chunks:
                b_ref[rows, :] = _dot(tri_ref[...], la_hi[rows, :]) + _dot(tri_ref[...], la_lo[rows, :])
            min_b = functools.reduce(
                jnp.minimum, [jnp.min(b_ref[rows.stop - 1:rows.stop, :]) for rows in fast_chunks])
            factorisable_ref[0] = (min_b >= -GLA_FACTOR_MAX_DECAY).astype(jnp.int32)
            v_ref[...] = proj(_V).astype(BF16)
            gla_operands = [_gla_factor_operands(qs_ref, k_ref, b_ref, rows) for rows in fast_chunks]

        if do_tail:
            out_ref[...] = resid_ref[...] + _dot(merged, wo_ref[...])

        if do_head:
            u = proj(_U)
            ub = u.astype(BF16)
            t_abs = t * ts + lax.broadcasted_iota(jnp.int32, (ts, 1), 0)
            gdim = d_model // len(POOL_WINDOWS)
            gslices = [slice(gi * gdim, (gi + 1) * gdim) for gi in range(len(POOL_WINDOWS))]
            pooled_in = []
            for gi, gs in enumerate(gslices):
                parts = []
                for rows in fast_chunks:
                    look_back = (ulast_ref[:, gs] if rows.start == 0
                                 else ub[rows.start - POOL_LOOKBACK:rows.start, gs])
                    in_chunk = _dot(band_ref[gi], ub[rows, gs])
                    carried = _dot(bandlb_ref[gi], look_back)
                    parts += [in_chunk[:POOL_LOOKBACK] + carried, in_chunk[POOL_LOOKBACK:]]
                count = jnp.minimum(t_abs + 1, POOL_WINDOWS[gi]).astype(F32)
                pooled_in.append((jnp.concatenate(parts, axis=0) / count - u[:, gs]).astype(BF16))
            ulast_ref[...] = ub[ts - POOL_LOOKBACK:, :]

        if do_head:
            def gate_factor_columns(h):
                def issue():
                    cols = slice(h * hdv, (h + 1) * hdv)
                    g = _dot(xb, wbig_ref[:, _G * d_model + h * hdv:_G * d_model + (h + 1) * hdv])
                    ya_scale_ref[:, cols] = gnw_ref[:, cols] * (g * _sigmoid(g))
                return issue

            def merge_gate_columns(h):
                def issue():
                    cols = slice(_GATE_A * d_model + h * hdv, _GATE_A * d_model + (h + 1) * hdv)
                    gate_a_ref[:, h * hdv:(h + 1) * hdv] = _sigmoid(_dot(xb, wbig_ref[:, cols]))
                return issue

            chunk_states = (state_ref, mid_state_ref, new_state_ref)
            chunk_fillers = (gate_factor_columns, merge_gate_columns)
            for c, rows in enumerate(fast_chunks):
                _gla_chunk_factorised(gla_operands[c], v_ref, o_ref, rows, chunk_states[c], chunk_states[c + 1],
                                      hdk=hdk, hdv=hdv,
                                      fillers=[chunk_fillers[c](h) for h in range(GLA_HEADS)])

        if do_head:
            z = proj(_Z)
            silu_z = z * _sigmoid(z)

        if do_tail:
            r = out_ref[...]
            mu = jnp.mean(r, axis=-1, keepdims=True)
            rc = r - mu
            var = jnp.mean(rc * rc, axis=-1, keepdims=True)
            out_ref[...] = (rc * lax.rsqrt(var + NORM_EPS) * lnw_ref[...] + lnb_ref[...]).astype(out_ref.dtype)

        if do_head:
            p_parts =[_dot(pooled_in[gi], poolw_ref[gs, :]) for gi, gs in enumerate(gslices)]
            gate_b = _sigmoid(proj(_GATE_B))
            y_b = (jnp.concatenate(p_parts, axis=-1) + poolb_ref[...]) * pools_ref[...] * silu_z
            resid_ref[...] = alpha * xf
            merged_b_ref[...] = gate_b * _dot(y_b.astype(BF16), wb_ref[...])

    pl.when(s == 0)(functools.partial(step, True, False))
    pl.when(jnp.logical_and(s > 0, s < n_tiles))(functools.partial(step, True, True))
    pl.when(s == n_tiles)(functools.partial(step, False, True))

    @pl.when(s < n_tiles)
    def _():
        @pl.when(factorisable_ref[0] == 0)
        def _():
            span = GLA_FAST_CHUNK
            row = lax.broadcasted_iota(jnp.int32, (span, span), 0)
            col = lax.broadcasted_iota(jnp.int32, (span, span), 1)
            tri_chunk = jnp.where(row // GLA_CHUNK == col // GLA_CHUNK, tri_ref[...], jnp.zeros((), BF16))
            la_hi, la_lo = _hi_lo(la_ref[...])
            for rows in fast_chunks:
                b_ref[rows, :] = _dot(tri_chunk, la_hi[rows, :]) + _dot(tri_chunk, la_lo[rows, :])
            _gla_tile_direct(qs_ref, k_ref, b_ref, v_ref, o_ref, state_ref, new_state_ref, hdk=hdk, hdv=hdv)

        state_ref[...] = new_state_ref[...]


def _const_spec(shape):
    nd = len(shape)
    return pl.BlockSpec(shape, lambda s: (0,) * nd, pipeline_mode=pl.Buffered(1))


def _tri_matrix():
    r = np.arange(GLA_FAST_CHUNK)
    return jnp.asarray((r[None, :] <= r[:, None]).astype(np.float32), dtype=BF16)


def _band_matrices():
    r = np.arange(GLA_FAST_CHUNK)[:, None]
    c = np.arange(GLA_FAST_CHUNK)[None, :]
    c_lb = np.arange(POOL_LOOKBACK)[None, :] - POOL_LOOKBACK
    in_win = lambda d, w: ((d >= 0) & (d < w)).astype(np.float32)
    bands = np.stack([in_win(r - c, w) for w in POOL_WINDOWS])
    bands_lb = np.stack([in_win(r[:POOL_LOOKBACK] - c_lb, w) for w in POOL_WINDOWS])
    return jnp.asarray(bands, dtype=BF16), jnp.asarray(bands_lb, dtype=BF16)


def _layer(x, w_in_t, w_gate_up, b_gate, gn_w, pool_w, pool_b, pool_scale, w_a, w_b, w_o, ln_w, ln_b,
           *, alpha):
    bsz, seq, d_model = x.shape
    rank, dk = w_gate_up.shape
    dv = gn_w.shape[0]
    assert seq % SEQ_TILE == 0 and SEQ_TILE % GLA_FAST_CHUNK == 0 and GLA_FAST_CHUNK % GLA_CHUNK == 0
    assert dv == d_model and 2 * dk == d_model and rank == GLA_GATE_RANK
    assert pool_w.shape[0] == len(POOL_WINDOWS)
    d_in = w_in_t.shape[0]
    gdim = pool_w.shape[1]
    row = lambda a: a.reshape(1, -1)
    hbm_weights = (w_in_t, w_a, w_b, w_o, pool_w.reshape(d_model, gdim))
    small = (w_gate_up, row(b_gate), row(gn_w), row(pool_b), row(pool_scale), row(ln_w), row(ln_b),
             _tri_matrix(), *_band_matrices())
    operands = (x, *hbm_weights, *small)
    tiles_per_seq = seq // SEQ_TILE
    n_tiles = bsz * tiles_per_seq

    def tile_block(tile):
        return (tile // tiles_per_seq, tile % tiles_per_seq, 0)

    in_specs = [pl.BlockSpec((None, SEQ_TILE, d_model), lambda s: tile_block(jnp.minimum(s, n_tiles - 1)))]
    in_specs += [pl.BlockSpec(memory_space=pl.ANY) for _ in hbm_weights]
    in_specs += [_const_spec(a.shape) for a in small]
    kernel = functools.partial(_block_kernel, d_model=d_model, dk=dk, dv=dv, alpha=alpha,
                               tiles_per_seq=tiles_per_seq, n_tiles=n_tiles)
    state_shape = (GLA_HEADS, dk // GLA_HEADS, dv // GLA_HEADS)
    return pl.pallas_call(
        kernel,
        grid=(n_tiles + 1,),
        in_specs=in_specs,
        out_specs=pl.BlockSpec((None, SEQ_TILE, d_model), lambda s: tile_block(jnp.maximum(s - 1, 0))),
        out_shape=jax.ShapeDtypeStruct(x.shape, x.dtype),
        scratch_shapes=[
            pltpu.VMEM((SEQ_TILE, dk), F32),
            pltpu.VMEM((SEQ_TILE, dk), F32),
            pltpu.VMEM((SEQ_TILE, dk), F32),
            pltpu.VMEM((SEQ_TILE, dv), BF16),
            pltpu.VMEM((SEQ_TILE, dv), F32),
            pltpu.VMEM(state_shape, F32),
            pltpu.VMEM(state_shape, F32),
            pltpu.VMEM(state_shape, F32),
            pltpu.VMEM((POOL_LOOKBACK, d_model), BF16),
            pltpu.VMEM((SEQ_TILE, dv), F32),
            pltpu.VMEM((SEQ_TILE, d_model), F32),
            pltpu.VMEM((SEQ_TILE, d_model), F32),
            pltpu.VMEM((SEQ_TILE, d_model), F32),
            pltpu.VMEM((SEQ_TILE, dk), F32),
            pltpu.SMEM((1,), jnp.int32),
            pltpu.VMEM((d_model, d_in - rank), BF16),
            pltpu.VMEM((rank, d_model), BF16),
            pltpu.VMEM(w_a.shape, BF16),
            pltpu.VMEM(w_b.shape, BF16),
            pltpu.VMEM(w_o.shape, BF16),
            pltpu.VMEM((d_model, gdim), BF16),
            pltpu.VMEM((WEIGHT_STAGE_SLOTS, WEIGHT_LOAD_ROWS, d_model), F32),
            pltpu.VMEM((rank, d_model), F32),
            pltpu.VMEM((d_model, gdim), F32),
            pltpu.SemaphoreType.DMA((WEIGHT_STAGE_SLOTS + 2,)),
        ],
        compiler_params=pltpu.CompilerParams(
            dimension_semantics=("arbitrary",),
            vmem_limit_bytes=VMEM_LIMIT_BYTES),
        name="hybrid_gla_pool_layer",
    )(*operands)


def kernel(x, w_in, w_gate_up, b_gate, gn_w, pool_w, pool_b, pool_scale, w_a, w_b, w_o, ln_w, ln_b):
    depth = w_in.shape[0]
    alpha = (2.0 * depth) ** 0.25
    w_in_t = jnp.swapaxes(w_in, 1, 2)
    for l in range(depth):
        x = _layer(x, w_in_t[l], w_gate_up[l], b_gate[l], gn_w[l], pool_w[l], pool_b[l], pool_scale[l],
                   w_a[l], w_b[l], w_o[l], ln_w[l], ln_b[l], alpha=alpha)
    return x
```

```python
import functools

import numpy as np
import jax
import jax.numpy as jnp
from jax import lax
from jax.experimental import pallas as pl
from jax.experimental.pallas import tpu as pltpu

GLA_HEADS = 4
GLA_GATE_RANK = 16
GLA_GATE_TAU = 16.0
POOL_WINDOWS = (2, 4, 8, 16)
POOL_LOOKBACK = 16
NORM_EPS = 1e-5

SEQ_TILE = 512
GLA_FAST_CHUNK = 256
GLA_CHUNK = 64
WEIGHT_LOAD_ROWS = 256
WEIGHT_STAGE_SLOTS = 4
VMEM_LIMIT_BYTES = 60 * 1024 * 1024
GLA_FACTOR_MAX_DECAY = 50.0

_QK, _V, _G, _U, _Z, _GATE_A, _GATE_B = range(7)

F32 = jnp.float32
BF16 = jnp.bfloat16
F32_SUBLANES = 8


def _dot(a, b):
    return jnp.dot(a, b, preferred_element_type=F32)


def _sigmoid(x):
    return 1.0 / (1.0 + jnp.exp(-x))


def _log_sigmoid(x):
    return jnp.minimum(x, 0.0) - jnp.log(1.0 + jnp.exp(-jnp.abs(x)))


def _hi_lo(x):
    hi = x.astype(BF16)
    return hi, (x - hi.astype(F32)).astype(BF16)


def _row_to_col(row):
    return jnp.transpose(jnp.broadcast_to(row, (F32_SUBLANES, row.shape[1])))[:, 0:1]


def _gla_factor_operands(qs_ref, k_ref, b_ref, rows):
    b = b_ref[rows, :]
    e_pos = jnp.exp(b)
    e_last = e_pos[GLA_FAST_CHUNK - 1:GLA_FAST_CHUNK, :]
    k_neg = k_ref[rows, :] * jnp.exp(-b)
    qd = (qs_ref[rows, :] * e_pos).astype(BF16)
    kn = k_neg.astype(BF16)
    kd = k_neg * e_last
    return qd, kn, kd, e_last


def _gla_chunk_factorised(operands, v_ref, o_ref, rows, state_ref, new_state_ref, *, hdk, hdv, fillers):
    ts = GLA_FAST_CHUNK
    qd, kn, kd, e_last = operands
    heads = range(GLA_HEADS)
    ks = [slice(h * hdk, (h + 1) * hdk) for h in heads]
    vs = [slice(h * hdv, (h + 1) * hdv) for h in heads]
    causal = (lax.broadcasted_iota(jnp.int32, (ts, ts), 0)
              >= lax.broadcasted_iota(jnp.int32, (ts, ts), 1))
    masked = []
    for h in heads:
        scores = lax.dot_general(qd[:, ks[h]], kn[:, ks[h]], (((1,), (1,)), ((), ())),
                                 preferred_element_type=F32)
        fillers[h]()
        masked.append(jnp.where(causal, scores, 0.0).astype(BF16))
    for h in heads:
        kd_t = jnp.transpose(kd[:, ks[h]]).astype(BF16)
        new_state_ref[h] = state_ref[h] * _row_to_col(e_last[:, ks[h]]) + _dot(kd_t, v_ref[rows, vs[h]])
    for h in heads:
        o_ref[rows, vs[h]] = (_dot(qd[:, ks[h]], state_ref[h].astype(BF16))
                              + _dot(masked[h], v_ref[rows, vs[h]]))


def _gla_tile_direct(qs_ref, k_ref, b_ref, v_ref, o_ref, state_ref, new_state_ref, *, hdk, hdv):
    new_state_ref[...] = state_ref[...]
    row_c = lax.broadcasted_iota(jnp.int32, (GLA_CHUNK, 1), 0)
    lane_c = lax.broadcasted_iota(jnp.int32, (GLA_CHUNK, GLA_CHUNK), 1)

    for c in range(SEQ_TILE // GLA_CHUNK):
        r0 = c * GLA_CHUNK
        rows = pl.ds(r0, GLA_CHUNK)
        qs_c = qs_ref[rows, :]
        b_c = b_ref[rows, :]

        def col_body(j, a_heads):
            kj = k_ref[pl.ds(r0 + j, 1), :]
            bj = b_ref[pl.ds(r0 + j, 1), :]
            p = qs_c * kj * jnp.exp(jnp.minimum(b_c - bj, 0.0))
            p = jnp.where(row_c >= j, p, 0.0)
            new = []
            for h in range(GLA_HEADS):
                s = jnp.sum(p[:, h * hdk:(h + 1) * hdk], axis=-1, keepdims=True)
                new.append(jnp.where(lane_c == j, s, a_heads[h]))
            return tuple(new)

        a_heads = lax.fori_loop(
            0, GLA_CHUNK, col_body,
            tuple(jnp.zeros((GLA_CHUNK, GLA_CHUNK), F32) for _ in range(GLA_HEADS)))

        b_last = b_ref[pl.ds(r0 + GLA_CHUNK - 1, 1), :]
        qd = (qs_c * jnp.exp(b_c)).astype(BF16)
        kd = k_ref[rows, :] * jnp.exp(b_last - b_c)
        e_last = jnp.exp(b_last)
        for h in range(GLA_HEADS):
            ks = slice(h * hdk, (h + 1) * hdk)
            vs = slice(h * hdv, (h + 1) * hdv)
            v_ch = v_ref[rows, vs]
            st = new_state_ref[h]
            o_ref[rows, vs] = (_dot(qd[:, ks], st.astype(BF16))
                               + _dot(a_heads[h].astype(BF16), v_ch))
            kd_t = jnp.transpose(kd[:, ks]).astype(BF16)
            new_state_ref[h] = st * _row_to_col(e_last[:, ks]) + _dot(kd_t, v_ch)


def _load_weights(wint_hbm, wa_hbm, wb_hbm, wo_hbm, poolw_hbm,
                  wbig_ref, wal_ref, wa_ref, wb_ref, wo_ref, poolw_ref,
                  stage_ref, gate_stage_ref, pool_stage_ref, sems, *, n_before_gate, rank):
    rows = WEIGHT_LOAD_ROWS
    d_in, d_model = wint_hbm.shape
    assert n_before_gate % rows == 0 and (d_in - rank) % rows == 0 and d_model % rows == 0
    jobs = []
    for j in range((d_in - rank) // rows):
        row0 = j * rows if j * rows < n_before_gate else j * rows + rank
        jobs.append((wint_hbm, row0, wbig_ref.at[:, j * rows:(j + 1) * rows]))
    for src, dst in ((wa_hbm, wa_ref), (wb_hbm, wb_ref), (wo_hbm, wo_ref)):
        for r0 in range(0, d_model, rows):
            jobs.append((src, r0, dst.at[r0:r0 + rows, :]))

    slots = stage_ref.shape[0]

    def staged_copy(i):
        src, row0, _ = jobs[i]
        return pltpu.make_async_copy(src.at[pl.ds(row0, rows), :], stage_ref.at[i % slots], sems.at[i % slots])

    gate_copy = pltpu.make_async_copy(wint_hbm.at[pl.ds(n_before_gate, rank), :], gate_stage_ref, sems.at[slots])
    pool_copy = pltpu.make_async_copy(poolw_hbm, pool_stage_ref, sems.at[slots + 1])
    gate_copy.start()
    pool_copy.start()
    ahead = slots - 1
    for i in range(min(ahead, len(jobs))):
        staged_copy(i).start(priority=i % 2)
    for i, (src, _, dst) in enumerate(jobs):
        if i + ahead < len(jobs):
            staged_copy(i + ahead).start(priority=(i + ahead) % 2)
        staged_copy(i).wait()
        block = stage_ref[i % slots]
        dst[...] = (jnp.transpose(block) if src is wint_hbm else block).astype(BF16)
    gate_copy.wait()
    wal_ref[...] = jnp.transpose(gate_stage_ref[...]).astype(BF16)
    pool_copy.wait()
    poolw_ref[...] = pool_stage_ref[...].astype(BF16)


def _block_kernel(x_ref, wint_hbm, wa_hbm, wb_hbm, wo_hbm, poolw_hbm,
                  wgu_ref, bg_ref, gnw_ref, poolb_ref, pools_ref, lnw_ref, lnb_ref,
                  tri_ref, band_ref, bandlb_ref,
                  out_ref,
                  qs_ref, k_ref, b_ref, v_ref, o_ref, state_ref, mid_state_ref, new_state_ref, ulast_ref,
                  ya_scale_ref, gate_a_ref, merged_b_ref, resid_ref, la_ref, factorisable_ref,
                  wbig_ref, wal_ref, wa_ref, wb_ref, wo_ref, poolw_ref,
                  stage_ref, gate_stage_ref, pool_stage_ref, load_sems,
                  *, d_model, dk, dv, alpha, tiles_per_seq, n_tiles):
    ts = SEQ_TILE
    hdk = dk // GLA_HEADS
    hdv = dv // GLA_HEADS
    s = pl.program_id(0)
    t = lax.rem(jnp.minimum(s, n_tiles - 1), tiles_per_seq)
    fast_chunks = [slice(r0, r0 + GLA_FAST_CHUNK) for r0 in range(0, ts, GLA_FAST_CHUNK)]
    assert len(fast_chunks) == 2

    @pl.when(s == 0)
    def _():
        _load_weights(wint_hbm, wa_hbm, wb_hbm, wo_hbm, poolw_hbm,
                      wbig_ref, wal_ref, wa_ref, wb_ref, wo_ref, poolw_ref,
                      stage_ref, gate_stage_ref, pool_stage_ref, load_sems,
                      n_before_gate=2 * dk + 2 * dv, rank=GLA_GATE_RANK)

    @pl.when(t == 0)
    def _():
        state_ref[...] = jnp.zeros_like(state_ref)
        ulast_ref[...] = jnp.zeros_like(ulast_ref)

    def step(do_head, do_tail):
        if do_head:
            xf = x_ref[...]
            xb = xf.astype(BF16)

            def proj(n):
                return _dot(xb, wbig_ref[:, n * d_model:(n + 1) * d_model])

        if do_tail:
            o = o_ref[...]
            y_a_parts = []
            for h in range(GLA_HEADS):
                oh = o[:, h * hdv:(h + 1) * hdv]
                y_a_parts.append(oh * lax.rsqrt(jnp.mean(oh * oh, axis=-1, keepdims=True) + NORM_EPS))
            y_a = (jnp.concatenate(y_a_parts, axis=-1) * ya_scale_ref[...]).astype(BF16)

        if do_head:
            a_low = _dot(xb, wal_ref[...])
            qk = proj(_QK)
            qs_ref[...] = qk[:, :dk] * (hdk ** -0.5)
            k_ref[...] = qk[:, dk:]
            gate_pre = _dot(a_low.astype(BF16), wgu_ref[...].astype(BF16)) + bg_ref[...]
            log_a = _log_sigmoid(gate_pre) * (1.0 / GLA_GATE_TAU)
            la_ref[...] = log_a
            la_hi, la_lo = _hi_lo(log_a)

        if do_tail:
            merged = (gate_a_ref[...] * _dot(y_a, wa_ref[...]) + merged_b_ref[...]).astype(BF16)

        if do_head:
            for rows in fast_chunks:
                b_ref[rows, :] = _dot(tri_ref[...], la_hi[rows, :]) + _dot(tri_ref[...], la_lo[rows, :])
            min_b = functools.reduce(
                jnp.minimum, [jnp.min(b_ref[rows.stop - 1:rows.stop, :]) for rows in fast_chunks])
            factorisable_ref[0] = (min_b >= -GLA_FACTOR_MAX_DECAY).astype(jnp.int32)
            v_ref[...] = proj(_V).astype(BF16)
            gla_operands = [_gla_factor_operands(qs_ref, k_ref, b_ref, rows) for rows in fast_chunks]

        if do_tail:
            out_ref[...] = resid_ref[...] + _dot(merged, wo_ref[...])

        if do_head:
            u = proj(_U)
            ub = u.astype(BF16)
            t_abs = t * ts + lax.broadcasted_iota(jnp.int32, (ts, 1), 0)
            gdim = d_model // len(POOL_WINDOWS)
            gslices = [slice(gi * gdim, (gi + 1) * gdim) for gi in range(len(POOL_WINDOWS))]
            pooled_in = []
            for gi, gs in enumerate(gslices):
                parts = []
                for rows in fast_chunks:
                    look_back = (ulast_ref[:, gs] if rows.start == 0
                                 else ub[rows.start - POOL_LOOKBACK:rows.start, gs])
                    in_chunk = _dot(band_ref[gi], ub[rows, gs])
                    carried = _dot(bandlb_ref[gi], look_back)
                    parts += [in_chunk[:POOL_LOOKBACK] + carried, in_chunk[POOL_LOOKBACK:]]
                count = jnp.minimum(t_abs + 1, POOL_WINDOWS[gi]).astype(F32)
                pooled_in.append((jnp.concatenate(parts, axis=0) / count - u[:, gs]).astype(BF16))
            ulast_ref[...] = ub[ts - POOL_LOOKBACK:, :]

        if do_head:
            def gate_factor_columns(h):
                def issue():
                    cols = slice(h * hdv, (h + 1) * hdv)
                    g = _dot(xb, wbig_ref[:, _G * d_model + h * hdv:_G * d_model + (h + 1) * hdv])
                    ya_scale_ref[:, cols] = gnw_ref[:, cols] * (g * _sigmoid(g))
                return issue

            def merge_gate_columns(h):
                def issue():
                    cols = slice(_GATE_A * d_model + h * hdv, _GATE_A * d_model + (h + 1) * hdv)
                    gate_a_ref[:, h * hdv:(h + 1) * hdv] = _sigmoid(_dot(xb, wbig_ref[:, cols]))
                return issue

            chunk_states = (state_ref, mid_state_ref, new_state_ref)
            chunk_fillers = (gate_factor_columns, merge_gate_columns)
            for c, rows in enumerate(fast_chunks):
                _gla_chunk_factorised(gla_operands[c], v_ref, o_ref, rows, chunk_states[c], chunk_states[c + 1],
                                      hdk=hdk, hdv=hdv,
                                      fillers=[chunk_fillers[c](h) for h in range(GLA_HEADS)])

        if do_head:
            z = proj(_Z)
            silu_z = z * _sigmoid(z)

        if do_tail:
            r = out_ref[...]
            mu = jnp.mean(r, axis=-1, keepdims=True)
            rc = r - mu
            var = jnp.mean(rc * rc, axis=-1, keepdims=True)
            out_ref[...] = (rc * lax.rsqrt(var + NORM_EPS) * lnw_ref[...] + lnb_ref[...]).astype(out_ref.dtype)

        if do_head:
            p_parts =[_dot(pooled_in[gi], poolw_ref[gs, :]) for gi, gs in enumerate(gslices)]
            gate_b = _sigmoid(proj(_GATE_B))
            y_b = (jnp.concatenate(p_parts, axis=-1) + poolb_ref[...]) * pools_ref[...] * silu_z
            resid_ref[...] = alpha * xf
            merged_b_ref[...] = gate_b * _dot(y_b.astype(BF16), wb_ref[...])

    pl.when(s == 0)(functools.partial(step, True, False))
    pl.when(jnp.logical_and(s > 0, s < n_tiles))(functools.partial(step, True, True))
    pl.when(s == n_tiles)(functools.partial(step, False, True))

    @pl.when(s < n_tiles)
    def _():
        @pl.when(factorisable_ref[0] == 0)
        def _():
            span = GLA_FAST_CHUNK
            row = lax.broadcasted_iota(jnp.int32, (span, span), 0)
            col = lax.broadcasted_iota(jnp.int32, (span, span), 1)
            tri_chunk = jnp.where(row // GLA_CHUNK == col // GLA_CHUNK, tri_ref[...], jnp.zeros((), BF16))
            la_hi, la_lo = _hi_lo(la_ref[...])
            for rows in fast_chunks:
                b_ref[rows, :] = _dot(tri_chunk, la_hi[rows, :]) + _dot(tri_chunk, la_lo[rows, :])
            _gla_tile_direct(qs_ref, k_ref, b_ref, v_ref, o_ref, state_ref, new_state_ref, hdk=hdk, hdv=hdv)

        state_ref[...] = new_state_ref[...]


def _const_spec(shape):
    nd = len(shape)
    return pl.BlockSpec(shape, lambda s: (0,) * nd, pipeline_mode=pl.Buffered(1))


def _tri_matrix():
    r = np.arange(GLA_FAST_CHUNK)
    return jnp.asarray((r[None, :] <= r[:, None]).astype(np.float32), dtype=BF16)


def _band_matrices():
    r = np.arange(GLA_FAST_CHUNK)[:, None]
    c = np.arange(GLA_FAST_CHUNK)[None, :]
    c_lb = np.arange(POOL_LOOKBACK)[None, :] - POOL_LOOKBACK
    in_win = lambda d, w: ((d >= 0) & (d < w)).astype(np.float32)
    bands = np.stack([in_win(r - c, w) for w in POOL_WINDOWS])
    bands_lb = np.stack([in_win(r[:POOL_LOOKBACK] - c_lb, w) for w in POOL_WINDOWS])
    return jnp.asarray(bands, dtype=BF16), jnp.asarray(bands_lb, dtype=BF16)


def _layer(x, w_in_t, w_gate_up, b_gate, gn_w, pool_w, pool_b, pool_scale, w_a, w_b, w_o, ln_w, ln_b,
           *, alpha):
    bsz, seq, d_model = x.shape
    rank, dk = w_gate_up.shape
    dv = gn_w.shape[0]
    assert seq % SEQ_TILE == 0 and SEQ_TILE % GLA_FAST_CHUNK == 0 and GLA_FAST_CHUNK % GLA_CHUNK == 0
    assert dv == d_model and 2 * dk == d_model and rank == GLA_GATE_RANK
    assert pool_w.shape[0] == len(POOL_WINDOWS)
    d_in = w_in_t.shape[0]
    gdim = pool_w.shape[1]
    row = lambda a: a.reshape(1, -1)
    hbm_weights = (w_in_t, w_a, w_b, w_o, pool_w.reshape(d_model, gdim))
    small = (w_gate_up, row(b_gate), row(gn_w), row(pool_b), row(pool_scale), row(ln_w), row(ln_b),
             _tri_matrix(), *_band_matrices())
    operands = (x, *hbm_weights, *small)
    tiles_per_seq = seq // SEQ_TILE
    n_tiles = bsz * tiles_per_seq

    def tile_block(tile):
        return (tile // tiles_per_seq, tile % tiles_per_seq, 0)

    in_specs = [pl.BlockSpec((None, SEQ_TILE, d_model), lambda s: tile_block(jnp.minimum(s, n_tiles - 1)))]
    in_specs += [pl.BlockSpec(memory_space=pl.ANY) for _ in hbm_weights]
    in_specs += [_const_spec(a.shape) for a in small]
    kernel = functools.partial(_block_kernel, d_model=d_model, dk=dk, dv=dv, alpha=alpha,
                               tiles_per_seq=tiles_per_seq, n_tiles=n_tiles)
    state_shape = (GLA_HEADS, dk // GLA_HEADS, dv // GLA_HEADS)
    return pl.pallas_call(
        kernel,
        grid=(n_tiles + 1,),
        in_specs=in_specs,
        out_specs=pl.BlockSpec((None, SEQ_TILE, d_model), lambda s: tile_block(jnp.maximum(s - 1, 0))),
        out_shape=jax.ShapeDtypeStruct(x.shape, x.dtype),
        scratch_shapes=[
            pltpu.VMEM((SEQ_TILE, dk), F32),
            pltpu.VMEM((SEQ_TILE, dk), F32),
            pltpu.VMEM((SEQ_TILE, dk), F32),
            pltpu.VMEM((SEQ_TILE, dv), BF16),
            pltpu.VMEM((SEQ_TILE, dv), F32),
            pltpu.VMEM(state_shape, F32),
            pltpu.VMEM(state_shape, F32),
            pltpu.VMEM(state_shape, F32),
            pltpu.VMEM((POOL_LOOKBACK, d_model), BF16),
            pltpu.VMEM((SEQ_TILE, dv), F32),
            pltpu.VMEM((SEQ_TILE, d_model), F32),
            pltpu.VMEM((SEQ_TILE, d_model), F32),
            pltpu.VMEM((SEQ_TILE, d_model), F32),
            pltpu.VMEM((SEQ_TILE, dk), F32),
            pltpu.SMEM((1,), jnp.int32),
            pltpu.VMEM((d_model, d_in - rank), BF16),
            pltpu.VMEM((d_model, rank), BF16),
            pltpu.VMEM(w_a.shape, BF16),
            pltpu.VMEM(w_b.shape, BF16),
            pltpu.VMEM(w_o.shape, BF16),
            pltpu.VMEM((d_model, gdim), BF16),
            pltpu.VMEM((WEIGHT_STAGE_SLOTS, WEIGHT_LOAD_ROWS, d_model), F32),
            pltpu.VMEM((rank, d_model), F32),
            pltpu.VMEM((d_model, gdim), F32),
            pltpu.SemaphoreType.DMA((WEIGHT_STAGE_SLOTS + 2,)),
        ],
        compiler_params=pltpu.CompilerParams(
            dimension_semantics=("arbitrary",),
            vmem_limit_bytes=VMEM_LIMIT_BYTES),
        name="hybrid_gla_pool_layer",
    )(*operands)


def kernel(x, w_in, w_gate_up, b_gate, gn_w, pool_w, pool_b, pool_scale, w_a, w_b, w_o, ln_w, ln_b):
    depth = w_in.shape[0]
    alpha = (2.0 * depth) ** 0.25
    w_in_t = jnp.swapaxes(w_in, 1, 2)
    for l in range(depth):
        x = _layer(x, w_in_t[l], w_gate_up[l], b_gate[l], gn_w[l], pool_w[l], pool_b[l], pool_scale[l],
                   w_a[l], w_b[l], w_o[l], ln_w[l], ln_b[l], alpha=alpha)
    return x
```

```python
import functools

import numpy as np
import jax
import jax.numpy as jnp
from jax import lax
from jax.experimental import pallas as pl
from jax.experimental.pallas import tpu as pltpu

GLA_HEADS = 4
GLA_GATE_RANK = 16
GLA_GATE_TAU = 16.0
POOL_WINDOWS = (2, 4, 8, 16)
POOL_LOOKBACK = 16
NORM_EPS = 1e-5

SEQ_TILE = 512
GLA_FAST_CHUNK = 256
GLA_CHUNK = 64
WEIGHT_LOAD_ROWS = 256
WEIGHT_STAGE_SLOTS = 4
VMEM_LIMIT_BYTES = 60 * 1024 * 1024
GLA_FACTOR_MAX_DECAY = 50.0

_QK, _V, _G, _U, _Z, _GATE_A, _GATE_B = range(7)

F32 = jnp.float32
BF16 = jnp.bfloat16
F32_SUBLANES = 8


def _dot(a, b):
    return jnp.dot(a, b, preferred_element_type=F32)


def _sigmoid(x):
    return 1.0 / (1.0 + jnp.exp(-x))


def _log_sigmoid(x):
    return jnp.minimum(x, 0.0) - jnp.log(1.0 + jnp.exp(-jnp.abs(x)))


def _hi_lo(x):
    hi = x.astype(BF16)
    return hi, (x - hi.astype(F32)).astype(BF16)


def _row_to_col(row):
    return jnp.transpose(jnp.broadcast_to(row, (F32_SUBLANES, row.shape[1])))[:, 0:1]


def _gla_factor_operands(qs_ref, k_ref, b_ref, rows):
    b = b_ref[rows, :]
    e_pos = jnp.exp(b)
    e_last = e_pos[GLA_FAST_CHUNK - 1:GLA_FAST_CHUNK, :]
    k_neg = k_ref[rows, :] * jnp.exp(-b)
    qd = (qs_ref[rows, :] * e_pos).astype(BF16)
    kn = k_neg.astype(BF16)
    kd = k_neg * e_last
    return qd, kn, kd, e_last


def _gla_chunk_factorised(operands, v_ref, o_ref, rows, state_ref, new_state_ref, *, hdk, hdv, fillers):
    ts = GLA_FAST_CHUNK
    qd, kn, kd, e_last = operands
    heads = range(GLA_HEADS)
    ks = [slice(h * hdk, (h + 1) * hdk) for h in heads]
    vs = [slice(h * hdv, (h + 1) * hdv) for h in heads]
    causal = (lax.broadcasted_iota(jnp.int32, (ts, ts), 0)
              >= lax.broadcasted_iota(jnp.int32, (ts, ts), 1))
    masked = []
    for h in heads:
        scores = lax.dot_general(qd[:, ks[h]], kn[:, ks[h]], (((1,), (1,)), ((), ())),
                                 preferred_element_type=F32)
        fillers[h]()
        masked.append(jnp.where(causal, scores, 0.0).astype(BF16))
    for h in heads:
        kd_t = jnp.transpose(kd[:, ks[h]]).astype(BF16)
        new_state_ref[h] = state_ref[h] * _row_to_col(e_last[:, ks[h]]) + _dot(kd_t, v_ref[rows, vs[h]])
    for h in heads:
        o_ref[rows, vs[h]] = (_dot(qd[:, ks[h]], state_ref[h].astype(BF16))
                              + _dot(masked[h], v_ref[rows, vs[h]]))


def _gla_tile_direct(qs_ref, k_ref, b_ref, v_ref, o_ref, state_ref, new_state_ref, *, hdk, hdv):
    new_state_ref[...] = state_ref[...]
    row_c = lax.broadcasted_iota(jnp.int32, (GLA_CHUNK, 1), 0)
    lane_c = lax.broadcasted_iota(jnp.int32, (GLA_CHUNK, GLA_CHUNK), 1)

    @pl.loop(0, SEQ_TILE // GLA_CHUNK)
    def _(c):
        r0 = pl.multiple_of(c * GLA_CHUNK, GLA_CHUNK)
        rows = pl.ds(r0, GLA_CHUNK)
        qs_c = qs_ref[rows, :]
        b_c = b_ref[rows, :]

        def col_body(j, a_heads):
            kj = k_ref[pl.ds(r0 + j, 1), :]
            bj = b_ref[pl.ds(r0 + j, 1), :]
            p = qs_c * kj * jnp.exp(jnp.minimum(b_c - bj, 0.0))
            p = jnp.where(row_c >= j, p, 0.0)
            new = []
            for h in range(GLA_HEADS):
                s = jnp.sum(p[:, h * hdk:(h + 1) * hdk], axis=-1, keepdims=True)
                new.append(jnp.where(lane_c == j, s, a_heads[h]))
            return tuple(new)

        a_heads = lax.fori_loop(
            0, GLA_CHUNK, col_body,
            tuple(jnp.zeros((GLA_CHUNK, GLA_CHUNK), F32) for _ in range(GLA_HEADS)))

        b_last = b_ref[pl.ds(r0 + GLA_CHUNK - 1, 1), :]
        qd = (qs_c * jnp.exp(b_c)).astype(BF16)
        kd = k_ref[rows, :] * jnp.exp(b_last - b_c)
        e_last = jnp.exp(b_last)
        for h in range(GLA_HEADS):
            ks = slice(h * hdk, (h + 1) * hdk)
            vs = slice(h * hdv, (h + 1) * hdv)
            v_ch = v_ref[rows, vs]
            st = new_state_ref[h]
            o_ref[rows, vs] = (_dot(qd[:, ks], st.astype(BF16))
                               + _dot(a_heads[h].astype(BF16), v_ch))
            kd_t = jnp.transpose(kd[:, ks]).astype(BF16)
            new_state_ref[h] = st * _row_to_col(e_last[:, ks]) + _dot(kd_t, v_ch)


def _load_weights(wint_hbm, wa_hbm, wb_hbm, wo_hbm, poolw_hbm,
                  wbig_ref, wal_ref, wa_ref, wb_ref, wo_ref, poolw_ref,
                  stage_ref, gate_stage_ref, pool_stage_ref, sems, *, n_before_gate, rank):
    rows = WEIGHT_LOAD_ROWS
    d_in, d_model = wint_hbm.shape
    assert n_before_gate % rows == 0 and (d_in - rank) % rows == 0 and d_model % rows == 0
    jobs = []
    for j in range((d_in - rank) // rows):
        row0 = j * rows if j * rows < n_before_gate else j * rows + rank
        jobs.append((wint_hbm, row0, wbig_ref.at[:, j * rows:(j + 1) * rows]))
    for src, dst in ((wa_hbm, wa_ref), (wb_hbm, wb_ref), (wo_hbm, wo_ref)):
        for r0 in range(0, d_model, rows):
            jobs.append((src, r0, dst.at[r0:r0 + rows, :]))

    slots = stage_ref.shape[0]

    def staged_copy(i):
        src, row0, _ = jobs[i]
        return pltpu.make_async_copy(src.at[pl.ds(row0, rows), :], stage_ref.at[i % slots], sems.at[i % slots])

    gate_copy = pltpu.make_async_copy(wint_hbm.at[pl.ds(n_before_gate, rank), :], gate_stage_ref, sems.at[slots])
    pool_copy = pltpu.make_async_copy(poolw_hbm, pool_stage_ref, sems.at[slots + 1])
    gate_copy.start()
    pool_copy.start()
    ahead = slots - 1
    for i in range(min(ahead, len(jobs))):
        staged_copy(i).start()
    for i, (src, _, dst) in enumerate(jobs):
        if i + ahead < len(jobs):
            staged_copy(i + ahead).start()
        staged_copy(i).wait()
        block = stage_ref[i % slots]
        dst[...] = (jnp.transpose(block) if src is wint_hbm else block).astype(BF16)
    gate_copy.wait()
    wal_ref[...] = jnp.transpose(gate_stage_ref[...]).astype(BF16)
    pool_copy.wait()
    poolw_ref[...] = pool_stage_ref[...].astype(BF16)


def _block_kernel(x_ref, wint_hbm, wa_hbm, wb_hbm, wo_hbm, poolw_hbm,
                  wgu_ref, bg_ref, gnw_ref, poolb_ref, pools_ref, lnw_ref, lnb_ref,
                  tri_ref, band_ref, bandlb_ref,
                  out_ref,
                  qs_ref, k_ref, b_ref, v_ref, o_ref, state_ref, mid_state_ref, new_state_ref, ulast_ref,
                  ya_scale_ref, gate_a_ref, merged_b_ref, resid_ref, la_ref, factorisable_ref,
                  wbig_ref, wal_ref, wa_ref, wb_ref, wo_ref, poolw_ref,
                  stage_ref, gate_stage_ref, pool_stage_ref, load_sems,
                  *, d_model, dk, dv, alpha, tiles_per_seq, n_tiles):
    ts = SEQ_TILE
    hdk = dk // GLA_HEADS
    hdv = dv // GLA_HEADS
    s = pl.program_id(0)
    t = lax.rem(jnp.minimum(s, n_tiles - 1), tiles_per_seq)
    fast_chunks = [slice(r0, r0 + GLA_FAST_CHUNK) for r0 in range(0, ts, GLA_FAST_CHUNK)]
    assert len(fast_chunks) == 2

    @pl.when(s == 0)
    def _():
        _load_weights(wint_hbm, wa_hbm, wb_hbm, wo_hbm, poolw_hbm,
                      wbig_ref, wal_ref, wa_ref, wb_ref, wo_ref, poolw_ref,
                      stage_ref, gate_stage_ref, pool_stage_ref, load_sems,
                      n_before_gate=2 * dk + 2 * dv, rank=GLA_GATE_RANK)
        for ref in (o_ref, ya_scale_ref, gate_a_ref, merged_b_ref, resid_ref):
            @pl.loop(0, ts // GLA_CHUNK)
            def _(i, ref=ref):
                ref[pl.ds(pl.multiple_of(i * GLA_CHUNK, GLA_CHUNK), GLA_CHUNK), :] = jnp.zeros(
                    (GLA_CHUNK, ref.shape[1]), ref.dtype)

    @pl.when(t == 0)
    def _():
        state_ref[...] = jnp.zeros_like(state_ref)
        ulast_ref[...] = jnp.zeros_like(ulast_ref)

    def step(do_head, do_tail):
        if do_head:
            xf = x_ref[...]
            xb = xf.astype(BF16)

            def proj(n):
                return _dot(xb, wbig_ref[:, n * d_model:(n + 1) * d_model])

        if do_tail:
            o = o_ref[...]
            y_a_parts = []
            for h in range(GLA_HEADS):
                oh = o[:, h * hdv:(h + 1) * hdv]
                y_a_parts.append(oh * lax.rsqrt(jnp.mean(oh * oh, axis=-1, keepdims=True) + NORM_EPS))
            y_a = (jnp.concatenate(y_a_parts, axis=-1) * ya_scale_ref[...]).astype(BF16)

        if do_head:
            a_low = _dot(xb, wal_ref[...])
            qk = proj(_QK)
            qs_ref[...] = qk[:, :dk] * (hdk ** -0.5)
            k_ref[...] = qk[:, dk:]
            gate_pre = _dot(a_low.astype(BF16), wgu_ref[...].astype(BF16)) + bg_ref[...]
            log_a = _log_sigmoid(gate_pre) * (1.0 / GLA_GATE_TAU)
            la_ref[...] = log_a
            la_hi, la_lo = _hi_lo(log_a)

        if do_tail:
            merged = (gate_a_ref[...] * _dot(y_a, wa_ref[...]) + merged_b_ref[...]).astype(BF16)

        if do_head:
            for rows in fast_chunks:
                b_ref[rows, :] = _dot(tri_ref[...], la_hi[rows, :]) + _dot(tri_ref[...], la_lo[rows, :])
            min_b = functools.reduce(
                jnp.minimum, [jnp.min(b_ref[rows.stop - 1:rows.stop, :]) for rows in fast_chunks])
            factorisable_ref[0] = (min_b >= -GLA_FACTOR_MAX_DECAY).astype(jnp.int32)
            v_ref[...] = proj(_V).astype(BF16)
            gla_operands = [_gla_factor_operands(qs_ref, k_ref, b_ref, rows) for rows in fast_chunks]

        if do_tail:
            out_ref[...] = resid_ref[...] + _dot(merged, wo_ref[...])

        if do_head:
            u = proj(_U)
            ub = u.astype(BF16)
            t_abs = t * ts + lax.broadcasted_iota(jnp.int32, (ts, 1), 0)
            gdim = d_model // len(POOL_WINDOWS)
            gslices = [slice(gi * gdim, (gi + 1) * gdim) for gi in range(len(POOL_WINDOWS))]
            pooled_in = []
            for gi, gs in enumerate(gslices):
                parts = []
                for rows in fast_chunks:
                    look_back = (ulast_ref[:, gs] if rows.start == 0
                                 else ub[rows.start - POOL_LOOKBACK:rows.start, gs])
                    in_chunk = _dot(band_ref[gi], ub[rows, gs])
                    carried = _dot(bandlb_ref[gi], look_back)
                    parts += [in_chunk[:POOL_LOOKBACK] + carried, in_chunk[POOL_LOOKBACK:]]
                count = jnp.minimum(t_abs + 1, POOL_WINDOWS[gi]).astype(F32)
                pooled_in.append((jnp.concatenate(parts, axis=0) / count - u[:, gs]).astype(BF16))
            ulast_ref[...] = ub[ts - POOL_LOOKBACK:, :]

        if do_head:
            def gate_factor_columns(h):
                def issue():
                    cols = slice(h * hdv, (h + 1) * hdv)
                    g = _dot(xb, wbig_ref[:, _G * d_model + h * hdv:_G * d_model + (h + 1) * hdv])
                    ya_scale_ref[:, cols] = gnw_ref[:, cols] * (g * _sigmoid(g))
                return issue

            def merge_gate_columns(h):
                def issue():
                    cols = slice(_GATE_A * d_model + h * hdv, _GATE_A * d_model + (h + 1) * hdv)
                    gate_a_ref[:, h * hdv:(h + 1) * hdv] = _sigmoid(_dot(xb, wbig_ref[:, cols]))
                return issue

            chunk_states = (state_ref, mid_state_ref, new_state_ref)
            chunk_fillers = (gate_factor_columns, merge_gate_columns)
            for c, rows in enumerate(fast_chunks):
                _gla_chunk_factorised(gla_operands[c], v_ref, o_ref, rows, chunk_states[c], chunk_states[c + 1],
                                      hdk=hdk, hdv=hdv,
                                      fillers=[chunk_fillers[c](h) for h in range(GLA_HEADS)])

        if do_head:
            z = proj(_Z)
            silu_z = z * _sigmoid(z)

        if do_tail:
            r = out_ref[...]
            mu = jnp.mean(r, axis=-1, keepdims=True)
            rc = r - mu
            var = jnp.mean(rc * rc, axis=-1, keepdims=True)
            out_ref[...] = (rc * lax.rsqrt(var + NORM_EPS) * lnw_ref[...] + lnb_ref[...]).astype(out_ref.dtype)

        if do_head:
            p_parts =[_dot(pooled_in[gi], poolw_ref[gs, :]) for gi, gs in enumerate(gslices)]
            gate_b = _sigmoid(proj(_GATE_B))
            y_b = (jnp.concatenate(p_parts, axis=-1) + poolb_ref[...]) * pools_ref[...] * silu_z
            resid_ref[...] = alpha * xf
            merged_b_ref[...] = gate_b * _dot(y_b.astype(BF16), wb_ref[...])

    pl.when(s < n_tiles)(functools.partial(step, True, True))
    pl.when(s == n_tiles)(functools.partial(step, False, True))

    @pl.when(s < n_tiles)
    def _():
        @pl.when(factorisable_ref[0] == 0)
        def _():
            span = GLA_FAST_CHUNK
            row = lax.broadcasted_iota(jnp.int32, (span, span), 0)
            col = lax.broadcasted_iota(jnp.int32, (span, span), 1)
            tri_chunk = jnp.where(row // GLA_CHUNK == col // GLA_CHUNK, tri_ref[...], jnp.zeros((), BF16))
            la_hi, la_lo = _hi_lo(la_ref[...])
            for rows in fast_chunks:
                b_ref[rows, :] = _dot(tri_chunk, la_hi[rows, :]) + _dot(tri_chunk, la_lo[rows, :])
            _gla_tile_direct(qs_ref, k_ref, b_ref, v_ref, o_ref, state_ref, new_state_ref, hdk=hdk, hdv=hdv)

        state_ref[...] = new_state_ref[...]


def _const_spec(shape):
    nd = len(shape)
    return pl.BlockSpec(shape, lambda s: (0,) * nd, pipeline_mode=pl.Buffered(1))


def _tri_matrix():
    r = np.arange(GLA_FAST_CHUNK)
    return jnp.asarray((r[None, :] <= r[:, None]).astype(np.float32), dtype=BF16)


def _band_matrices():
    r = np.arange(GLA_FAST_CHUNK)[:, None]
    c = np.arange(GLA_FAST_CHUNK)[None, :]
    c_lb = np.arange(POOL_LOOKBACK)[None, :] - POOL_LOOKBACK
    in_win = lambda d, w: ((d >= 0) & (d < w)).astype(np.float32)
    bands = np.stack([in_win(r - c, w) for w in POOL_WINDOWS])
    bands_lb = np.stack([in_win(r[:POOL_LOOKBACK] - c_lb, w) for w in POOL_WINDOWS])
    return jnp.asarray(bands, dtype=BF16), jnp.asarray(bands_lb, dtype=BF16)


def _layer(x, w_in_t, w_gate_up, b_gate, gn_w, pool_w, pool_b, pool_scale, w_a, w_b, w_o, ln_w, ln_b,
           *, alpha):
    bsz, seq, d_model = x.shape
    rank, dk = w_gate_up.shape
    dv = gn_w.shape[0]
    assert seq % SEQ_TILE == 0 and SEQ_TILE % GLA_FAST_CHUNK == 0 and GLA_FAST_CHUNK % GLA_CHUNK == 0
    assert dv == d_model and 2 * dk == d_model and rank == GLA_GATE_RANK
    assert pool_w.shape[0] == len(POOL_WINDOWS)
    d_in = w_in_t.shape[0]
    gdim = pool_w.shape[1]
    row = lambda a: a.reshape(1, -1)
    hbm_weights = (w_in_t, w_a, w_b, w_o, pool_w.reshape(d_model, gdim))
    small = (w_gate_up, row(b_gate), row(gn_w), row(pool_b), row(pool_scale), row(ln_w), row(ln_b),
             _tri_matrix(), *_band_matrices())
    operands = (x, *hbm_weights, *small)
    tiles_per_seq = seq // SEQ_TILE
    n_tiles = bsz * tiles_per_seq

    def tile_block(tile):
        return (tile // tiles_per_seq, tile % tiles_per_seq, 0)

    in_specs = [pl.BlockSpec((None, SEQ_TILE, d_model), lambda s: tile_block(jnp.minimum(s, n_tiles - 1)))]
    in_specs += [pl.BlockSpec(memory_space=pl.ANY) for _ in hbm_weights]
    in_specs += [_const_spec(a.shape) for a in small]
    kernel = functools.partial(_block_kernel, d_model=d_model, dk=dk, dv=dv, alpha=alpha,
                               tiles_per_seq=tiles_per_seq, n_tiles=n_tiles)
    state_shape = (GLA_HEADS, dk // GLA_HEADS, dv // GLA_HEADS)
    return pl.pallas_call(
        kernel,
        grid=(n_tiles + 1,),
        in_specs=in_specs,
        out_specs=pl.BlockSpec((None, SEQ_TILE, d_model), lambda s: tile_block(jnp.maximum(s - 1, 0))),
        out_shape=jax.ShapeDtypeStruct(x.shape, x.dtype),
        scratch_shapes=[
            pltpu.VMEM((SEQ_TILE, dk), F32),
            pltpu.VMEM((SEQ_TILE, dk), F32),
            pltpu.VMEM((SEQ_TILE, dk), F32),
            pltpu.VMEM((SEQ_TILE, dv), BF16),
            pltpu.VMEM((SEQ_TILE, dv), F32),
            pltpu.VMEM(state_shape, F32),
            pltpu.VMEM(state_shape, F32),
            pltpu.VMEM(state_shape, F32),
            pltpu.VMEM((POOL_LOOKBACK, d_model), BF16),
            pltpu.VMEM((SEQ_TILE, dv), F32),
            pltpu.VMEM((SEQ_TILE, d_model), F32),
            pltpu.VMEM((SEQ_TILE, d_model), F32),
            pltpu.VMEM((SEQ_TILE, d_model), F32),
            pltpu.VMEM((SEQ_TILE, dk), F32),
            pltpu.SMEM((1,), jnp.int32),
            pltpu.VMEM((d_model, d_in - rank), BF16),
            pltpu.VMEM((d_model, rank), BF16),
            pltpu.VMEM(w_a.shape, BF16),
            pltpu.VMEM(w_b.shape, BF16),
            pltpu.VMEM(w_o.shape, BF16),
            pltpu.VMEM((d_model, gdim), BF16),
            pltpu.VMEM((WEIGHT_STAGE_SLOTS, WEIGHT_LOAD_ROWS, d_model), F32),
            pltpu.VMEM((rank, d_model), F32),
            pltpu.VMEM((d_model, gdim), F32),
            pltpu.SemaphoreType.DMA((WEIGHT_STAGE_SLOTS + 2,)),
        ],
        compiler_params=pltpu.CompilerParams(
            dimension_semantics=("arbitrary",),
            vmem_limit_bytes=VMEM_LIMIT_BYTES),
        name="hybrid_gla_pool_layer",
    )(*operands)


def kernel(x, w_in, w_gate_up, b_gate, gn_w, pool_w, pool_b, pool_scale, w_a, w_b, w_o, ln_w, ln_b):
    depth = w_in.shape[0]
    alpha = (2.0 * depth) ** 0.25
    w_in_t = jnp.swapaxes(w_in, 1, 2)
    for l in range(depth):
        x = _layer(x, w_in_t[l], w_gate_up[l], b_gate[l], gn_w[l], pool_w[l], pool_b[l], pool_scale[l],
                   w_a[l], w_b[l], w_o[l], ln_w[l], ln_b[l], alpha=alpha)
    return x
```

```python
import functools

import numpy as np
import jax
import jax.numpy as jnp
from jax import lax
from jax.experimental import pallas as pl
from jax.experimental.pallas import tpu as pltpu

GLA_HEADS = 4
GLA_GATE_RANK = 16
GLA_GATE_TAU = 16.0
POOL_WINDOWS = (2, 4, 8, 16)
POOL_LOOKBACK = 16
NORM_EPS = 1e-5

SEQ_TILE = 512
GLA_FAST_CHUNK = 256
GLA_CHUNK = 64
WEIGHT_LOAD_ROWS = 256
WEIGHT_STAGE_SLOTS = 6
VMEM_LIMIT_BYTES = 60 * 1024 * 1024
GLA_FACTOR_MAX_DECAY = 50.0

_QK, _V, _G, _U, _Z, _GATE_A, _GATE_B = range(7)

F32 = jnp.float32
BF16 = jnp.bfloat16
F32_SUBLANES = 8


def _dot(a, b):
    return jnp.dot(a, b, preferred_element_type=F32)


def _sigmoid(x):
    return 1.0 / (1.0 + jnp.exp(-x))


def _log_sigmoid(x):
    return jnp.minimum(x, 0.0) - jnp.log(1.0 + jnp.exp(-jnp.abs(x)))


def _hi_lo(x):
    hi = x.astype(BF16)
    return hi, (x - hi.astype(F32)).astype(BF16)


def _row_to_col(row):
    return jnp.transpose(jnp.broadcast_to(row, (F32_SUBLANES, row.shape[1])))[:, 0:1]


def _gla_factor_operands(qs_ref, k_ref, b_ref, rows):
    b = b_ref[rows, :]
    e_pos = jnp.exp(b)
    e_last = e_pos[GLA_FAST_CHUNK - 1:GLA_FAST_CHUNK, :]
    k_neg = k_ref[rows, :] * jnp.exp(-b)
    qd = (qs_ref[rows, :] * e_pos).astype(BF16)
    kn = k_neg.astype(BF16)
    kd = k_neg * e_last
    return qd, kn, kd, e_last


def _gla_chunk_factorised(operands, v_ref, o_ref, rows, state_ref, new_state_ref, *, hdk, hdv, fillers):
    ts = GLA_FAST_CHUNK
    qd, kn, kd, e_last = operands
    heads = range(GLA_HEADS)
    ks = [slice(h * hdk, (h + 1) * hdk) for h in heads]
    vs = [slice(h * hdv, (h + 1) * hdv) for h in heads]
    causal = (lax.broadcasted_iota(jnp.int32, (ts, ts), 0)
              >= lax.broadcasted_iota(jnp.int32, (ts, ts), 1))
    masked = []
    for h in heads:
        scores = lax.dot_general(qd[:, ks[h]], kn[:, ks[h]], (((1,), (1,)), ((), ())),
                                 preferred_element_type=F32)
        fillers[h]()
        masked.append(jnp.where(causal, scores, 0.0).astype(BF16))
    for h in heads:
        kd_t = jnp.transpose(kd[:, ks[h]]).astype(BF16)
        new_state_ref[h] = state_ref[h] * _row_to_col(e_last[:, ks[h]]) + _dot(kd_t, v_ref[rows, vs[h]])
    for h in heads:
        o_ref[rows, vs[h]] = (_dot(qd[:, ks[h]], state_ref[h].astype(BF16))
                              + _dot(masked[h], v_ref[rows, vs[h]]))


def _gla_tile_direct(qs_ref, k_ref, b_ref, v_ref, o_ref, state_ref, new_state_ref, *, hdk, hdv):
    new_state_ref[...] = state_ref[...]
    row_c = lax.broadcasted_iota(jnp.int32, (GLA_CHUNK, 1), 0)
    lane_c = lax.broadcasted_iota(jnp.int32, (GLA_CHUNK, GLA_CHUNK), 1)

    @pl.loop(0, SEQ_TILE // GLA_CHUNK)
    def _(c):
        r0 = pl.multiple_of(c * GLA_CHUNK, GLA_CHUNK)
        rows = pl.ds(r0, GLA_CHUNK)
        qs_c = qs_ref[rows, :]
        b_c = b_ref[rows, :]

        def col_body(j, a_heads):
            kj = k_ref[pl.ds(r0 + j, 1), :]
            bj = b_ref[pl.ds(r0 + j, 1), :]
            p = qs_c * kj * jnp.exp(jnp.minimum(b_c - bj, 0.0))
            p = jnp.where(row_c >= j, p, 0.0)
            new = []
            for h in range(GLA_HEADS):
                s = jnp.sum(p[:, h * hdk:(h + 1) * hdk], axis=-1, keepdims=True)
                new.append(jnp.where(lane_c == j, s, a_heads[h]))
            return tuple(new)

        a_heads = lax.fori_loop(
            0, GLA_CHUNK, col_body,
            tuple(jnp.zeros((GLA_CHUNK, GLA_CHUNK), F32) for _ in range(GLA_HEADS)))

        b_last = b_ref[pl.ds(r0 + GLA_CHUNK - 1, 1), :]
        qd = (qs_c * jnp.exp(b_c)).astype(BF16)
        kd = k_ref[rows, :] * jnp.exp(b_last - b_c)
        e_last = jnp.exp(b_last)
        for h in range(GLA_HEADS):
            ks = slice(h * hdk, (h + 1) * hdk)
            vs = slice(h * hdv, (h + 1) * hdv)
            v_ch = v_ref[rows, vs]
            st = new_state_ref[h]
            o_ref[rows, vs] = (_dot(qd[:, ks], st.astype(BF16))
                               + _dot(a_heads[h].astype(BF16), v_ch))
            kd_t = jnp.transpose(kd[:, ks]).astype(BF16)
            new_state_ref[h] = st * _row_to_col(e_last[:, ks]) + _dot(kd_t, v_ch)


def _load_weights(wint_hbm, wa_hbm, wb_hbm, wo_hbm, poolw_hbm,
                  wbig_ref, wal_ref, wa_ref, wb_ref, wo_ref, poolw_ref,
                  stage_ref, gate_stage_ref, pool_stage_ref, sems, *, n_before_gate, rank):
    rows = WEIGHT_LOAD_ROWS
    d_in, d_model = wint_hbm.shape
    assert n_before_gate % rows == 0 and (d_in - rank) % rows == 0 and d_model % rows == 0
    jobs = []
    for j in range((d_in - rank) // rows):
        row0 = j * rows if j * rows < n_before_gate else j * rows + rank
        jobs.append((wint_hbm, row0, wbig_ref.at[:, j * rows:(j + 1) * rows]))
    for src, dst in ((wa_hbm, wa_ref), (wb_hbm, wb_ref), (wo_hbm, wo_ref)):
        for r0 in range(0, d_model, rows):
            jobs.append((src, r0, dst.at[r0:r0 + rows, :]))

    slots = stage_ref.shape[0]

    def staged_copy(i):
        src, row0, _ = jobs[i]
        return pltpu.make_async_copy(src.at[pl.ds(row0, rows), :], stage_ref.at[i % slots], sems.at[i % slots])

    gate_copy = pltpu.make_async_copy(wint_hbm.at[pl.ds(n_before_gate, rank), :], gate_stage_ref, sems.at[slots])
    pool_copy = pltpu.make_async_copy(poolw_hbm, pool_stage_ref, sems.at[slots + 1])
    gate_copy.start()
    pool_copy.start()
    ahead = slots - 1
    for i in range(min(ahead, len(jobs))):
        staged_copy(i).start()
    for i, (src, _, dst) in enumerate(jobs):
        if i + ahead < len(jobs):
            staged_copy(i + ahead).start()
        staged_copy(i).wait()
        block = stage_ref[i % slots]
        dst[...] = (jnp.transpose(block) if src is wint_hbm else block).astype(BF16)
    gate_copy.wait()
    wal_ref[...] = jnp.transpose(gate_stage_ref[...]).astype(BF16)
    pool_copy.wait()
    poolw_ref[...] = pool_stage_ref[...].astype(BF16)


def _block_kernel(x_ref, wint_hbm, wa_hbm, wb_hbm, wo_hbm, poolw_hbm,
                  wgu_ref, bg_ref, gnw_ref, poolb_ref, pools_ref, lnw_ref, lnb_ref,
                  tri_ref, band_ref, bandlb_ref,
                  out_ref,
                  qs_ref, k_ref, b_ref, v_ref, o_ref, state_ref, mid_state_ref, new_state_ref, ulast_ref,
                  ya_scale_ref, gate_a_ref, merged_b_ref, resid_ref, la_ref, factorisable_ref,
                  wbig_ref, wal_ref, wa_ref, wb_ref, wo_ref, poolw_ref,
                  stage_ref, gate_stage_ref, pool_stage_ref, load_sems,
                  *, d_model, dk, dv, alpha, tiles_per_seq, n_tiles):
    ts = SEQ_TILE
    hdk = dk // GLA_HEADS
    hdv = dv // GLA_HEADS
    s = pl.program_id(0)
    t = lax.rem(jnp.minimum(s, n_tiles - 1), tiles_per_seq)
    fast_chunks = [slice(r0, r0 + GLA_FAST_CHUNK) for r0 in range(0, ts, GLA_FAST_CHUNK)]
    assert len(fast_chunks) == 2

    @pl.when(s == 0)
    def _():
        _load_weights(wint_hbm, wa_hbm, wb_hbm, wo_hbm, poolw_hbm,
                      wbig_ref, wal_ref, wa_ref, wb_ref, wo_ref, poolw_ref,
                      stage_ref, gate_stage_ref, pool_stage_ref, load_sems,
                      n_before_gate=2 * dk + 2 * dv, rank=GLA_GATE_RANK)
        for ref in (o_ref, ya_scale_ref, gate_a_ref, merged_b_ref, resid_ref):
            @pl.loop(0, ts // GLA_CHUNK)
            def _(i, ref=ref):
                ref[pl.ds(pl.multiple_of(i * GLA_CHUNK, GLA_CHUNK), GLA_CHUNK), :] = jnp.zeros(
                    (GLA_CHUNK, ref.shape[1]), ref.dtype)

    @pl.when(t == 0)
    def _():
        state_ref[...] = jnp.zeros_like(state_ref)
        ulast_ref[...] = jnp.zeros_like(ulast_ref)

    def step(do_head, do_tail):
        if do_head:
            xf = x_ref[...]
            xb = xf.astype(BF16)

            def proj(n):
                return _dot(xb, wbig_ref[:, n * d_model:(n + 1) * d_model])

        if do_tail:
            o = o_ref[...]
            y_a_parts = []
            for h in range(GLA_HEADS):
                oh = o[:, h * hdv:(h + 1) * hdv]
                y_a_parts.append(oh * lax.rsqrt(jnp.mean(oh * oh, axis=-1, keepdims=True) + NORM_EPS))
            y_a = (jnp.concatenate(y_a_parts, axis=-1) * ya_scale_ref[...]).astype(BF16)

        if do_head:
            a_low = _dot(xb, wal_ref[...])
            qk = proj(_QK)
            qs_ref[...] = qk[:, :dk] * (hdk ** -0.5)
            k_ref[...] = qk[:, dk:]
            gate_pre = _dot(a_low.astype(BF16), wgu_ref[...].astype(BF16)) + bg_ref[...]
            log_a = _log_sigmoid(gate_pre) * (1.0 / GLA_GATE_TAU)
            la_ref[...] = log_a
            la_hi, la_lo = _hi_lo(log_a)

        if do_tail:
            merged = (gate_a_ref[...] * _dot(y_a, wa_ref[...]) + merged_b_ref[...]).astype(BF16)

        if do_head:
            for rows in fast_chunks:
                b_ref[rows, :] = _dot(tri_ref[...], la_hi[rows, :]) + _dot(tri_ref[...], la_lo[rows, :])
            min_b = functools.reduce(
                jnp.minimum, [jnp.min(b_ref[rows.stop - 1:rows.stop, :]) for rows in fast_chunks])
            factorisable_ref[0] = (min_b >= -GLA_FACTOR_MAX_DECAY).astype(jnp.int32)
            v_ref[...] = proj(_V).astype(BF16)
            gla_operands = [_gla_factor_operands(qs_ref, k_ref, b_ref, rows) for rows in fast_chunks]

        if do_tail:
            out_ref[...] = resid_ref[...] + _dot(merged, wo_ref[...])

        if do_head:
            u = proj(_U)
            ub = u.astype(BF16)
            t_abs = t * ts + lax.broadcasted_iota(jnp.int32, (ts, 1), 0)
            gdim = d_model // len(POOL_WINDOWS)
            gslices = [slice(gi * gdim, (gi + 1) * gdim) for gi in range(len(POOL_WINDOWS))]
            pooled_in = []
            for gi, gs in enumerate(gslices):
                parts = []
                for rows in fast_chunks:
                    look_back = (ulast_ref[:, gs] if rows.start == 0
                                 else ub[rows.start - POOL_LOOKBACK:rows.start, gs])
                    in_chunk = _dot(band_ref[gi], ub[rows, gs])
                    carried = _dot(bandlb_ref[gi], look_back)
                    parts += [in_chunk[:POOL_LOOKBACK] + carried, in_chunk[POOL_LOOKBACK:]]
                count = jnp.minimum(t_abs + 1, POOL_WINDOWS[gi]).astype(F32)
                pooled_in.append((jnp.concatenate(parts, axis=0) / count - u[:, gs]).astype(BF16))
            ulast_ref[...] = ub[ts - POOL_LOOKBACK:, :]

        if do_head:
            def gate_factor_columns(h):
                def issue():
                    cols = slice(h * hdv, (h + 1) * hdv)
                    g = _dot(xb, wbig_ref[:, _G * d_model + h * hdv:_G * d_model + (h + 1) * hdv])
                    ya_scale_ref[:, cols] = gnw_ref[:, cols] * (g * _sigmoid(g))
                return issue

            def merge_gate_columns(h):
                def issue():
                    cols = slice(_GATE_A * d_model + h * hdv, _GATE_A * d_model + (h + 1) * hdv)
                    gate_a_ref[:, h * hdv:(h + 1) * hdv] = _sigmoid(_dot(xb, wbig_ref[:, cols]))
                return issue

            chunk_states = (state_ref, mid_state_ref, new_state_ref)
            chunk_fillers = (gate_factor_columns, merge_gate_columns)
            for c, rows in enumerate(fast_chunks):
                _gla_chunk_factorised(gla_operands[c], v_ref, o_ref, rows, chunk_states[c], chunk_states[c + 1],
                                      hdk=hdk, hdv=hdv,
                                      fillers=[chunk_fillers[c](h) for h in range(GLA_HEADS)])

        if do_head:
            z = proj(_Z)
            silu_z = z * _sigmoid(z)

        if do_tail:
            r = out_ref[...]
            mu = jnp.mean(r, axis=-1, keepdims=True)
            rc = r - mu
            var = jnp.mean(rc * rc, axis=-1, keepdims=True)
            out_ref[...] = (rc * lax.rsqrt(var + NORM_EPS) * lnw_ref[...] + lnb_ref[...]).astype(out_ref.dtype)

        if do_head:
            p_parts =[_dot(pooled_in[gi], poolw_ref[gs, :]) for gi, gs in enumerate(gslices)]
            gate_b = _sigmoid(proj(_GATE_B))
            y_b = (jnp.concatenate(p_parts, axis=-1) + poolb_ref[...]) * pools_ref[...] * silu_z
            resid_ref[...] = alpha * xf
            merged_b_ref[...] = gate_b * _dot(y_b.astype(BF16), wb_ref[...])

    pl.when(s < n_tiles)(functools.partial(step, True, True))
    pl.when(s == n_tiles)(functools.partial(step, False, True))

    @pl.when(s < n_tiles)
    def _():
        @pl.when(factorisable_ref[0] == 0)
        def _():
            span = GLA_FAST_CHUNK
            row = lax.broadcasted_iota(jnp.int32, (span, span), 0)
            col = lax.broadcasted_iota(jnp.int32, (span, span), 1)
            tri_chunk = jnp.where(row // GLA_CHUNK == col // GLA_CHUNK, tri_ref[...], jnp.zeros((), BF16))
            la_hi, la_lo = _hi_lo(la_ref[...])
            for rows in fast_chunks:
                b_ref[rows, :] = _dot(tri_chunk, la_hi[rows, :]) + _dot(tri_chunk, la_lo[rows, :])
            _gla_tile_direct(qs_ref, k_ref, b_ref, v_ref, o_ref, state_ref, new_state_ref, hdk=hdk, hdv=hdv)

        state_ref[...] = new_state_ref[...]


def _const_spec(shape):
    nd = len(shape)
    return pl.BlockSpec(shape, lambda s: (0,) * nd, pipeline_mode=pl.Buffered(1))


def _tri_matrix():
    r = np.arange(GLA_FAST_CHUNK)
    return jnp.asarray((r[None, :] <= r[:, None]).astype(np.float32), dtype=BF16)


def _band_matrices():
    r = np.arange(GLA_FAST_CHUNK)[:, None]
    c = np.arange(GLA_FAST_CHUNK)[None, :]
    c_lb = np.arange(POOL_LOOKBACK)[None, :] - POOL_LOOKBACK
    in_win = lambda d, w: ((d >= 0) & (d < w)).astype(np.float32)
    bands = np.stack([in_win(r - c, w) for w in POOL_WINDOWS])
    bands_lb = np.stack([in_win(r[:POOL_LOOKBACK] - c_lb, w) for w in POOL_WINDOWS])
    return jnp.asarray(bands, dtype=BF16), jnp.asarray(bands_lb, dtype=BF16)


def _layer(x, w_in_t, w_gate_up, b_gate, gn_w, pool_w, pool_b, pool_scale, w_a, w_b, w_o, ln_w, ln_b,
           *, alpha):
    bsz, seq, d_model = x.shape
    rank, dk = w_gate_up.shape
    dv = gn_w.shape[0]
    assert seq % SEQ_TILE == 0 and SEQ_TILE % GLA_FAST_CHUNK == 0 and GLA_FAST_CHUNK % GLA_CHUNK == 0
    assert dv == d_model and 2 * dk == d_model and rank == GLA_GATE_RANK
    assert pool_w.shape[0] == len(POOL_WINDOWS)
    d_in = w_in_t.shape[0]
    gdim = pool_w.shape[1]
    row = lambda a: a.reshape(1, -1)
    hbm_weights = (w_in_t, w_a, w_b, w_o, pool_w.reshape(d_model, gdim))
    small = (w_gate_up, row(b_gate), row(gn_w), row(pool_b), row(pool_scale), row(ln_w), row(ln_b),
             _tri_matrix(), *_band_matrices())
    operands = (x, *hbm_weights, *small)
    tiles_per_seq = seq // SEQ_TILE
    n_tiles = bsz * tiles_per_seq

    def tile_block(tile):
        return (tile // tiles_per_seq, tile % tiles_per_seq, 0)

    in_specs = [pl.BlockSpec((None, SEQ_TILE, d_model), lambda s: tile_block(jnp.minimum(s, n_tiles - 1)))]
    in_specs += [pl.BlockSpec(memory_space=pl.ANY) for _ in hbm_weights]
    in_specs += [_const_spec(a.shape) for a in small]
    kernel = functools.partial(_block_kernel, d_model=d_model, dk=dk, dv=dv, alpha=alpha,
                               tiles_per_seq=tiles_per_seq, n_tiles=n_tiles)
    state_shape = (GLA_HEADS, dk // GLA_HEADS, dv // GLA_HEADS)
    return pl.pallas_call(
        kernel,
        grid=(n_tiles + 1,),
        in_specs=in_specs,
        out_specs=pl.BlockSpec((None, SEQ_TILE, d_model), lambda s: tile_block(jnp.maximum(s - 1, 0))),
        out_shape=jax.ShapeDtypeStruct(x.shape, x.dtype),
        scratch_shapes=[
            pltpu.VMEM((SEQ_TILE, dk), F32),
            pltpu.VMEM((SEQ_TILE, dk), F32),
            pltpu.VMEM((SEQ_TILE, dk), F32),
            pltpu.VMEM((SEQ_TILE, dv), BF16),
            pltpu.VMEM((SEQ_TILE, dv), F32),
            pltpu.VMEM(state_shape, F32),
            pltpu.VMEM(state_shape, F32),
            pltpu.VMEM(state_shape, F32),
            pltpu.VMEM((POOL_LOOKBACK, d_model), BF16),
            pltpu.VMEM((SEQ_TILE, dv), F32),
            pltpu.VMEM((SEQ_TILE, d_model), F32),
            pltpu.VMEM((SEQ_TILE, d_model), F32),
            pltpu.VMEM((SEQ_TILE, d_model), F32),
            pltpu.VMEM((SEQ_TILE, dk), F32),
            pltpu.SMEM((1,), jnp.int32),
            pltpu.VMEM((d_model, d_in - rank), BF16),
            pltpu.VMEM((d_model, rank), BF16),
            pltpu.VMEM(w_a.shape, BF16),
            pltpu.VMEM(w_b.shape, BF16),
            pltpu.VMEM(w_o.shape, BF16),
            pltpu.VMEM((d_model, gdim), BF16),
            pltpu.VMEM((WEIGHT_STAGE_SLOTS, WEIGHT_LOAD_ROWS, d_model), F32),
            pltpu.VMEM((rank, d_model), F32),
            pltpu.VMEM((d_model, gdim), F32),
            pltpu.SemaphoreType.DMA((WEIGHT_STAGE_SLOTS + 2,)),
        ],
        compiler_params=pltpu.CompilerParams(
            dimension_semantics=("arbitrary",),
            vmem_limit_bytes=VMEM_LIMIT_BYTES),
        name="hybrid_gla_pool_layer",
    )(*operands)


def kernel(x, w_in, w_gate_up, b_gate, gn_w, pool_w, pool_b, pool_scale, w_a, w_b, w_o, ln_w, ln_b):
    depth = w_in.shape[0]
    alpha = (2.0 * depth) ** 0.25
    w_in_t = jnp.swapaxes(w_in, 1, 2)
    for l in range(depth):
        x = _layer(x, w_in_t[l], w_gate_up[l], b_gate[l], gn_w[l], pool_w[l], pool_b[l], pool_scale[l],
                   w_a[l], w_b[l], w_o[l], ln_w[l], ln_b[l], alpha=alpha)
    return x
```

```python
import functools

import numpy as np
import jax
import jax.numpy as jnp
from jax import lax
from jax.experimental import pallas as pl
from jax.experimental.pallas import tpu as pltpu

GLA_HEADS = 4
GLA_GATE_RANK = 16
GLA_GATE_TAU = 16.0
POOL_WINDOWS = (2, 4, 8, 16)
POOL_LOOKBACK = 16
NORM_EPS = 1e-5

SEQ_TILE = 512
GLA_FAST_CHUNK = 256
GLA_CHUNK = 64
WEIGHT_LOAD_ROWS = 128
WEIGHT_STAGE_SLOTS = 12
VMEM_LIMIT_BYTES = 60 * 1024 * 1024
GLA_FACTOR_MAX_DECAY = 50.0

_QK, _V, _G, _U, _Z, _GATE_A, _GATE_B = range(7)

F32 = jnp.float32
BF16 = jnp.bfloat16
F32_SUBLANES = 8


def _dot(a, b):
    return jnp.dot(a, b, preferred_element_type=F32)


def _sigmoid(x):
    return 1.0 / (1.0 + jnp.exp(-x))


def _log_sigmoid(x):
    return jnp.minimum(x, 0.0) - jnp.log(1.0 + jnp.exp(-jnp.abs(x)))


def _hi_lo(x):
    hi = x.astype(BF16)
    return hi, (x - hi.astype(F32)).astype(BF16)


def _row_to_col(row):
    return jnp.transpose(jnp.broadcast_to(row, (F32_SUBLANES, row.shape[1])))[:, 0:1]


def _gla_factor_operands(qs_ref, k_ref, b_ref, rows):
    b = b_ref[rows, :]
    e_pos = jnp.exp(b)
    e_last = e_pos[GLA_FAST_CHUNK - 1:GLA_FAST_CHUNK, :]
    k_neg = k_ref[rows, :] * jnp.exp(-b)
    qd = (qs_ref[rows, :] * e_pos).astype(BF16)
    kn = k_neg.astype(BF16)
    kd = k_neg * e_last
    return qd, kn, kd, e_last


def _gla_chunk_factorised(operands, v_ref, o_ref, rows, state_ref, new_state_ref, *, hdk, hdv, fillers):
    ts = GLA_FAST_CHUNK
    qd, kn, kd, e_last = operands
    heads = range(GLA_HEADS)
    ks = [slice(h * hdk, (h + 1) * hdk) for h in heads]
    vs = [slice(h * hdv, (h + 1) * hdv) for h in heads]
    causal = (lax.broadcasted_iota(jnp.int32, (ts, ts), 0)
              >= lax.broadcasted_iota(jnp.int32, (ts, ts), 1))
    masked = []
    for h in heads:
        scores = lax.dot_general(qd[:, ks[h]], kn[:, ks[h]], (((1,), (1,)), ((), ())),
                                 preferred_element_type=F32)
        fillers[h]()
        masked.append(jnp.where(causal, scores, 0.0).astype(BF16))
    for h in heads:
        kd_t = jnp.transpose(kd[:, ks[h]]).astype(BF16)
        new_state_ref[h] = state_ref[h] * _row_to_col(e_last[:, ks[h]]) + _dot(kd_t, v_ref[rows, vs[h]])
    for h in heads:
        o_ref[rows, vs[h]] = (_dot(qd[:, ks[h]], state_ref[h].astype(BF16))
                              + _dot(masked[h], v_ref[rows, vs[h]]))


def _gla_tile_direct(qs_ref, k_ref, b_ref, v_ref, o_ref, state_ref, new_state_ref, *, hdk, hdv):
    new_state_ref[...] = state_ref[...]
    row_c = lax.broadcasted_iota(jnp.int32, (GLA_CHUNK, 1), 0)
    lane_c = lax.broadcasted_iota(jnp.int32, (GLA_CHUNK, GLA_CHUNK), 1)

    @pl.loop(0, SEQ_TILE // GLA_CHUNK)
    def _(c):
        r0 = pl.multiple_of(c * GLA_CHUNK, GLA_CHUNK)
        rows = pl.ds(r0, GLA_CHUNK)
        qs_c = qs_ref[rows, :]
        b_c = b_ref[rows, :]

        def col_body(j, a_heads):
            kj = k_ref[pl.ds(r0 + j, 1), :]
            bj = b_ref[pl.ds(r0 + j, 1), :]
            p = qs_c * kj * jnp.exp(jnp.minimum(b_c - bj, 0.0))
            p = jnp.where(row_c >= j, p, 0.0)
            new = []
            for h in range(GLA_HEADS):
                s = jnp.sum(p[:, h * hdk:(h + 1) * hdk], axis=-1, keepdims=True)
                new.append(jnp.where(lane_c == j, s, a_heads[h]))
            return tuple(new)

        a_heads = lax.fori_loop(
            0, GLA_CHUNK, col_body,
            tuple(jnp.zeros((GLA_CHUNK, GLA_CHUNK), F32) for _ in range(GLA_HEADS)))

        b_last = b_ref[pl.ds(r0 + GLA_CHUNK - 1, 1), :]
        qd = (qs_c * jnp.exp(b_c)).astype(BF16)
        kd = k_ref[rows, :] * jnp.exp(b_last - b_c)
        e_last = jnp.exp(b_last)
        for h in range(GLA_HEADS):
            ks = slice(h * hdk, (h + 1) * hdk)
            vs = slice(h * hdv, (h + 1) * hdv)
            v_ch = v_ref[rows, vs]
            st = new_state_ref[h]
            o_ref[rows, vs] = (_dot(qd[:, ks], st.astype(BF16))
                               + _dot(a_heads[h].astype(BF16), v_ch))
            kd_t = jnp.transpose(kd[:, ks]).astype(BF16)
            new_state_ref[h] = st * _row_to_col(e_last[:, ks]) + _dot(kd_t, v_ch)


def _load_weights(wint_hbm, wa_hbm, wb_hbm, wo_hbm, poolw_hbm,
                  wbig_ref, wal_ref, wa_ref, wb_ref, wo_ref, poolw_ref,
                  stage_ref, gate_stage_ref, pool_stage_ref, sems, *, n_before_gate, rank):
    rows = WEIGHT_LOAD_ROWS
    d_in, d_model = wint_hbm.shape
    assert n_before_gate % rows == 0 and (d_in - rank) % rows == 0 and d_model % rows == 0
    jobs = []
    for j in range((d_in - rank) // rows):
        row0 = j * rows if j * rows < n_before_gate else j * rows + rank
        jobs.append((wint_hbm, row0, wbig_ref.at[:, j * rows:(j + 1) * rows]))
    for src, dst in ((wa_hbm, wa_ref), (wb_hbm, wb_ref), (wo_hbm, wo_ref)):
        for r0 in range(0, d_model, rows):
            jobs.append((src, r0, dst.at[r0:r0 + rows, :]))

    slots = stage_ref.shape[0]

    def staged_copy(i):
        src, row0, _ = jobs[i]
        return pltpu.make_async_copy(src.at[pl.ds(row0, rows), :], stage_ref.at[i % slots], sems.at[i % slots])

    gate_copy = pltpu.make_async_copy(wint_hbm.at[pl.ds(n_before_gate, rank), :], gate_stage_ref, sems.at[slots])
    pool_copy = pltpu.make_async_copy(poolw_hbm, pool_stage_ref, sems.at[slots + 1])
    gate_copy.start()
    pool_copy.start()
    ahead = slots - 1
    for i in range(min(ahead, len(jobs))):
        staged_copy(i).start()
    for i, (src, _, dst) in enumerate(jobs):
        if i + ahead < len(jobs):
            staged_copy(i + ahead).start()
        staged_copy(i).wait()
        block = stage_ref[i % slots]
        dst[...] = (jnp.transpose(block) if src is wint_hbm else block).astype(BF16)
    gate_copy.wait()
    wal_ref[...] = jnp.transpose(gate_stage_ref[...]).astype(BF16)
    pool_copy.wait()
    poolw_ref[...] = pool_stage_ref[...].astype(BF16)


def _block_kernel(x_ref, wint_hbm, wa_hbm, wb_hbm, wo_hbm, poolw_hbm,
                  wgu_ref, bg_ref, gnw_ref, poolb_ref, pools_ref, lnw_ref, lnb_ref,
                  tri_ref, band_ref, bandlb_ref,
                  out_ref,
                  qs_ref, k_ref, b_ref, v_ref, o_ref, state_ref, mid_state_ref, new_state_ref, ulast_ref,
                  ya_scale_ref, gate_a_ref, merged_b_ref, resid_ref, la_ref, factorisable_ref,
                  wbig_ref, wal_ref, wa_ref, wb_ref, wo_ref, poolw_ref,
                  stage_ref, gate_stage_ref, pool_stage_ref, load_sems,
                  *, d_model, dk, dv, alpha, tiles_per_seq, n_tiles):
    ts = SEQ_TILE
    hdk = dk // GLA_HEADS
    hdv = dv // GLA_HEADS
    s = pl.program_id(0)
    t = lax.rem(jnp.minimum(s, n_tiles - 1), tiles_per_seq)
    fast_chunks = [slice(r0, r0 + GLA_FAST_CHUNK) for r0 in range(0, ts, GLA_FAST_CHUNK)]
    assert len(fast_chunks) == 2

    @pl.when(s == 0)
    def _():
        _load_weights(wint_hbm, wa_hbm, wb_hbm, wo_hbm, poolw_hbm,
                      wbig_ref, wal_ref, wa_ref, wb_ref, wo_ref, poolw_ref,
                      stage_ref, gate_stage_ref, pool_stage_ref, load_sems,
                      n_before_gate=2 * dk + 2 * dv, rank=GLA_GATE_RANK)
        for ref in (o_ref, ya_scale_ref, gate_a_ref, merged_b_ref, resid_ref):
            @pl.loop(0, ts // GLA_CHUNK)
            def _(i, ref=ref):
                ref[pl.ds(pl.multiple_of(i * GLA_CHUNK, GLA_CHUNK), GLA_CHUNK), :] = jnp.zeros(
                    (GLA_CHUNK, ref.shape[1]), ref.dtype)

    @pl.when(t == 0)
    def _():
        state_ref[...] = jnp.zeros_like(state_ref)
        ulast_ref[...] = jnp.zeros_like(ulast_ref)

    def step(do_head, do_tail):
        if do_head:
            xf = x_ref[...]
            xb = xf.astype(BF16)

            def proj(n):
                return _dot(xb, wbig_ref[:, n * d_model:(n + 1) * d_model])

        if do_tail:
            o = o_ref[...]
            y_a_parts = []
            for h in range(GLA_HEADS):
                oh = o[:, h * hdv:(h + 1) * hdv]
                y_a_parts.append(oh * lax.rsqrt(jnp.mean(oh * oh, axis=-1, keepdims=True) + NORM_EPS))
            y_a = (jnp.concatenate(y_a_parts, axis=-1) * ya_scale_ref[...]).astype(BF16)

        if do_head:
            a_low = _dot(xb, wal_ref[...])
            qk = proj(_QK)
            qs_ref[...] = qk[:, :dk] * (hdk ** -0.5)
            k_ref[...] = qk[:, dk:]
            gate_pre = _dot(a_low.astype(BF16), wgu_ref[...].astype(BF16)) + bg_ref[...]
            log_a = _log_sigmoid(gate_pre) * (1.0 / GLA_GATE_TAU)
            la_ref[...] = log_a
            la_hi, la_lo = _hi_lo(log_a)

        if do_tail:
            merged = (gate_a_ref[...] * _dot(y_a, wa_ref[...]) + merged_b_ref[...]).astype(BF16)

        if do_head:
            for rows in fast_chunks:
                b_ref[rows, :] = _dot(tri_ref[...], la_hi[rows, :]) + _dot(tri_ref[...], la_lo[rows, :])
            min_b = functools.reduce(
                jnp.minimum, [jnp.min(b_ref[rows.stop - 1:rows.stop, :]) for rows in fast_chunks])
            factorisable_ref[0] = (min_b >= -GLA_FACTOR_MAX_DECAY).astype(jnp.int32)
            v_ref[...] = proj(_V).astype(BF16)
            gla_operands = [_gla_factor_operands(qs_ref, k_ref, b_ref, rows) for rows in fast_chunks]

        if do_tail:
            out_ref[...] = resid_ref[...] + _dot(merged, wo_ref[...])

        if do_head:
            u = proj(_U)
            ub = u.astype(BF16)
            t_abs = t * ts + lax.broadcasted_iota(jnp.int32, (ts, 1), 0)
            gdim = d_model // len(POOL_WINDOWS)
            gslices = [slice(gi * gdim, (gi + 1) * gdim) for gi in range(len(POOL_WINDOWS))]
            pooled_in = []
            for gi, gs in enumerate(gslices):
                parts = []
                for rows in fast_chunks:
                    look_back = (ulast_ref[:, gs] if rows.start == 0
                                 else ub[rows.start - POOL_LOOKBACK:rows.start, gs])
                    in_chunk = _dot(band_ref[gi], ub[rows, gs])
                    carried = _dot(bandlb_ref[gi], look_back)
                    parts += [in_chunk[:POOL_LOOKBACK] + carried, in_chunk[POOL_LOOKBACK:]]
                count = jnp.minimum(t_abs + 1, POOL_WINDOWS[gi]).astype(F32)
                pooled_in.append((jnp.concatenate(parts, axis=0) / count - u[:, gs]).astype(BF16))
            ulast_ref[...] = ub[ts - POOL_LOOKBACK:, :]

        if do_head:
            def gate_factor_columns(h):
                def issue():
                    cols = slice(h * hdv, (h + 1) * hdv)
                    g = _dot(xb, wbig_ref[:, _G * d_model + h * hdv:_G * d_model + (h + 1) * hdv])
                    ya_scale_ref[:, cols] = gnw_ref[:, cols] * (g * _sigmoid(g))
                return issue

            def merge_gate_columns(h):
                def issue():
                    cols = slice(_GATE_A * d_model + h * hdv, _GATE_A * d_model + (h + 1) * hdv)
                    gate_a_ref[:, h * hdv:(h + 1) * hdv] = _sigmoid(_dot(xb, wbig_ref[:, cols]))
                return issue

            chunk_states = (state_ref, mid_state_ref, new_state_ref)
            chunk_fillers = (gate_factor_columns, merge_gate_columns)
            for c, rows in enumerate(fast_chunks):
                _gla_chunk_factorised(gla_operands[c], v_ref, o_ref, rows, chunk_states[c], chunk_states[c + 1],
                                      hdk=hdk, hdv=hdv,
                                      fillers=[chunk_fillers[c](h) for h in range(GLA_HEADS)])

        if do_head:
            z = proj(_Z)
            silu_z = z * _sigmoid(z)

        if do_tail:
            r = out_ref[...]
            mu = jnp.mean(r, axis=-1, keepdims=True)
            rc = r - mu
            var = jnp.mean(rc * rc, axis=-1, keepdims=True)
            out_ref[...] = (rc * lax.rsqrt(var + NORM_EPS) * lnw_ref[...] + lnb_ref[...]).astype(out_ref.dtype)

        if do_head:
            p_parts =[_dot(pooled_in[gi], poolw_ref[gs, :]) for gi, gs in enumerate(gslices)]
            gate_b = _sigmoid(proj(_GATE_B))
            y_b = (jnp.concatenate(p_parts, axis=-1) + poolb_ref[...]) * pools_ref[...] * silu_z
            resid_ref[...] = alpha * xf
            merged_b_ref[...] = gate_b * _dot(y_b.astype(BF16), wb_ref[...])

    pl.when(s < n_tiles)(functools.partial(step, True, True))
    pl.when(s == n_tiles)(functools.partial(step, False, True))

    @pl.when(s < n_tiles)
    def _():
        @pl.when(factorisable_ref[0] == 0)
        def _():
            span = GLA_FAST_CHUNK
            row = lax.broadcasted_iota(jnp.int32, (span, span), 0)
            col = lax.broadcasted_iota(jnp.int32, (span, span), 1)
            tri_chunk = jnp.where(row // GLA_CHUNK == col // GLA_CHUNK, tri_ref[...], jnp.zeros((), BF16))
            la_hi, la_lo = _hi_lo(la_ref[...])
            for rows in fast_chunks:
                b_ref[rows, :] = _dot(tri_chunk, la_hi[rows, :]) + _dot(tri_chunk, la_lo[rows, :])
            _gla_tile_direct(qs_ref, k_ref, b_ref, v_ref, o_ref, state_ref, new_state_ref, hdk=hdk, hdv=hdv)

        state_ref[...] = new_state_ref[...]


def _const_spec(shape):
    nd = len(shape)
    return pl.BlockSpec(shape, lambda s: (0,) * nd, pipeline_mode=pl.Buffered(1))


def _tri_matrix():
    r = np.arange(GLA_FAST_CHUNK)
    return jnp.asarray((r[None, :] <= r[:, None]).astype(np.float32), dtype=BF16)


def _band_matrices():
    r = np.arange(GLA_FAST_CHUNK)[:, None]
    c = np.arange(GLA_FAST_CHUNK)[None, :]
    c_lb = np.arange(POOL_LOOKBACK)[None, :] - POOL_LOOKBACK
    in_win = lambda d, w: ((d >= 0) & (d < w)).astype(np.float32)
    bands = np.stack([in_win(r - c, w) for w in POOL_WINDOWS])
    bands_lb = np.stack([in_win(r[:POOL_LOOKBACK] - c_lb, w) for w in POOL_WINDOWS])
    return jnp.asarray(bands, dtype=BF16), jnp.asarray(bands_lb, dtype=BF16)


def _layer(x, w_in_t, w_gate_up, b_gate, gn_w, pool_w, pool_b, pool_scale, w_a, w_b, w_o, ln_w, ln_b,
           *, alpha):
    bsz, seq, d_model = x.shape
    rank, dk = w_gate_up.shape
    dv = gn_w.shape[0]
    assert seq % SEQ_TILE == 0 and SEQ_TILE % GLA_FAST_CHUNK == 0 and GLA_FAST_CHUNK % GLA_CHUNK == 0
    assert dv == d_model and 2 * dk == d_model and rank == GLA_GATE_RANK
    assert pool_w.shape[0] == len(POOL_WINDOWS)
    d_in = w_in_t.shape[0]
    gdim = pool_w.shape[1]
    row = lambda a: a.reshape(1, -1)
    hbm_weights = (w_in_t, w_a, w_b, w_o, pool_w.reshape(d_model, gdim))
    small = (w_gate_up, row(b_gate), row(gn_w), row(pool_b), row(pool_scale), row(ln_w), row(ln_b),
             _tri_matrix(), *_band_matrices())
    operands = (x, *hbm_weights, *small)
    tiles_per_seq = seq // SEQ_TILE
    n_tiles = bsz * tiles_per_seq

    def tile_block(tile):
        return (tile // tiles_per_seq, tile % tiles_per_seq, 0)

    in_specs = [pl.BlockSpec((None, SEQ_TILE, d_model), lambda s: tile_block(jnp.minimum(s, n_tiles - 1)))]
    in_specs += [pl.BlockSpec(memory_space=pl.ANY) for _ in hbm_weights]
    in_specs += [_const_spec(a.shape) for a in small]
    kernel = functools.partial(_block_kernel, d_model=d_model, dk=dk, dv=dv, alpha=alpha,
                               tiles_per_seq=tiles_per_seq, n_tiles=n_tiles)
    state_shape = (GLA_HEADS, dk // GLA_HEADS, dv // GLA_HEADS)
    return pl.pallas_call(
        kernel,
        grid=(n_tiles + 1,),
        in_specs=in_specs,
        out_specs=pl.BlockSpec((None, SEQ_TILE, d_model), lambda s: tile_block(jnp.maximum(s - 1, 0))),
        out_shape=jax.ShapeDtypeStruct(x.shape, x.dtype),
        scratch_shapes=[
            pltpu.VMEM((SEQ_TILE, dk), F32),
            pltpu.VMEM((SEQ_TILE, dk), F32),
            pltpu.VMEM((SEQ_TILE, dk), F32),
            pltpu.VMEM((SEQ_TILE, dv), BF16),
            pltpu.VMEM((SEQ_TILE, dv), F32),
            pltpu.VMEM(state_shape, F32),
            pltpu.VMEM(state_shape, F32),
            pltpu.VMEM(state_shape, F32),
            pltpu.VMEM((POOL_LOOKBACK, d_model), BF16),
            pltpu.VMEM((SEQ_TILE, dv), F32),
            pltpu.VMEM((SEQ_TILE, d_model), F32),
            pltpu.VMEM((SEQ_TILE, d_model), F32),
            pltpu.VMEM((SEQ_TILE, d_model), F32),
            pltpu.VMEM((SEQ_TILE, dk), F32),
            pltpu.SMEM((1,), jnp.int32),
            pltpu.VMEM((d_model, d_in - rank), BF16),
            pltpu.VMEM((d_model, rank), BF16),
            pltpu.VMEM(w_a.shape, BF16),
            pltpu.VMEM(w_b.shape, BF16),
            pltpu.VMEM(w_o.shape, BF16),
            pltpu.VMEM((d_model, gdim), BF16),
            pltpu.VMEM((WEIGHT_STAGE_SLOTS, WEIGHT_LOAD_ROWS, d_model), F32),
            pltpu.VMEM((rank, d_model), F32),
            pltpu.VMEM((d_model, gdim), F32),
            pltpu.SemaphoreType.DMA((WEIGHT_STAGE_SLOTS + 2,)),
        ],
        compiler_params=pltpu.CompilerParams(
            dimension_semantics=("arbitrary",),
            vmem_limit_bytes=VMEM_LIMIT_BYTES),
        name="hybrid_gla_pool_layer",
    )(*operands)


def kernel(x, w_in, w_gate_up, b_gate, gn_w, pool_w, pool_b, pool_scale, w_a, w_b, w_o, ln_w, ln_b):
    depth = w_in.shape[0]
    alpha = (2.0 * depth) ** 0.25
    w_in_t = jnp.swapaxes(w_in, 1, 2)
    for l in range(depth):
        x = _layer(x, w_in_t[l], w_gate_up[l], b_gate[l], gn_w[l], pool_w[l], pool_b[l], pool_scale[l],
                   w_a[l], w_b[l], w_o[l], ln_w[l], ln_b[l], alpha=alpha)
    return x
```

```python
import functools

import numpy as np
import jax
import jax.numpy as jnp
from jax import lax
from jax.experimental import pallas as pl
from jax.experimental.pallas import tpu as pltpu

GLA_HEADS = 4
GLA_GATE_RANK = 16
GLA_GATE_TAU = 16.0
POOL_WINDOWS = (2, 4, 8, 16)
POOL_LOOKBACK = 16
NORM_EPS = 1e-5

SEQ_TILE = 512
GLA_FAST_CHUNK = 256
GLA_CHUNK = 64
WEIGHT_LOAD_ROWS = 256
WEIGHT_STAGE_SLOTS = 6
VMEM_LIMIT_BYTES = 60 * 1024 * 1024
GLA_FACTOR_MAX_DECAY = 50.0

_QK, _V, _G, _U, _Z, _GATE_A, _GATE_B = range(7)

F32 = jnp.float32
BF16 = jnp.bfloat16
F32_SUBLANES = 8


def _dot(a, b):
    return jnp.dot(a, b, preferred_element_type=F32)


def _sigmoid(x):
    return 1.0 / (1.0 + jnp.exp(-x))


def _log_sigmoid(x):
    return jnp.minimum(x, 0.0) - jnp.log(1.0 + jnp.exp(-jnp.abs(x)))


def _hi_lo(x):
    hi = x.astype(BF16)
    return hi, (x - hi.astype(F32)).astype(BF16)


def _row_to_col(row):
    return jnp.transpose(jnp.broadcast_to(row, (F32_SUBLANES, row.shape[1])))[:, 0:1]


def _gla_factor_operands(qs_ref, k_ref, b_ref, rows):
    b = b_ref[rows, :]
    e_pos = jnp.exp(b)
    e_last = e_pos[GLA_FAST_CHUNK - 1:GLA_FAST_CHUNK, :]
    k_neg = k_ref[rows, :] * jnp.exp(-b)
    qd = (qs_ref[rows, :] * e_pos).astype(BF16)
    kn = k_neg.astype(BF16)
    kd = k_neg * e_last
    return qd, kn, kd, e_last


def _gla_chunk_factorised(operands, v_ref, o_ref, rows, state_ref, new_state_ref, *, hdk, hdv, fillers):
    ts = GLA_FAST_CHUNK
    qd, kn, kd, e_last = operands
    heads = range(GLA_HEADS)
    ks = [slice(h * hdk, (h + 1) * hdk) for h in heads]
    vs = [slice(h * hdv, (h + 1) * hdv) for h in heads]
    causal = (lax.broadcasted_iota(jnp.int32, (ts, ts), 0)
              >= lax.broadcasted_iota(jnp.int32, (ts, ts), 1))
    masked = []
    for h in heads:
        scores = lax.dot_general(qd[:, ks[h]], kn[:, ks[h]], (((1,), (1,)), ((), ())),
                                 preferred_element_type=F32)
        fillers[h]()
        masked.append(jnp.where(causal, scores, 0.0).astype(BF16))
    for h in heads:
        kd_t = jnp.transpose(kd[:, ks[h]]).astype(BF16)
        new_state_ref[h] = state_ref[h] * _row_to_col(e_last[:, ks[h]]) + _dot(kd_t, v_ref[rows, vs[h]])
    for h in heads:
        o_ref[rows, vs[h]] = (_dot(qd[:, ks[h]], state_ref[h].astype(BF16))
                              + _dot(masked[h], v_ref[rows, vs[h]]))


def _gla_tile_direct(qs_ref, k_ref, b_ref, v_ref, o_ref, state_ref, new_state_ref, *, hdk, hdv):
    new_state_ref[...] = state_ref[...]
    row_c = lax.broadcasted_iota(jnp.int32, (GLA_CHUNK, 1), 0)
    lane_c = lax.broadcasted_iota(jnp.int32, (GLA_CHUNK, GLA_CHUNK), 1)

    @pl.loop(0, SEQ_TILE // GLA_CHUNK)
    def _(c):
        r0 = pl.multiple_of(c * GLA_CHUNK, GLA_CHUNK)
        rows = pl.ds(r0, GLA_CHUNK)
        qs_c = qs_ref[rows, :]
        b_c = b_ref[rows, :]

        def col_body(j, a_heads):
            kj = k_ref[pl.ds(r0 + j, 1), :]
            bj = b_ref[pl.ds(r0 + j, 1), :]
            p = qs_c * kj * jnp.exp(jnp.minimum(b_c - bj, 0.0))
            p = jnp.where(row_c >= j, p, 0.0)
            new = []
            for h in range(GLA_HEADS):
                s = jnp.sum(p[:, h * hdk:(h + 1) * hdk], axis=-1, keepdims=True)
                new.append(jnp.where(lane_c == j, s, a_heads[h]))
            return tuple(new)

        a_heads = lax.fori_loop(
            0, GLA_CHUNK, col_body,
            tuple(jnp.zeros((GLA_CHUNK, GLA_CHUNK), F32) for _ in range(GLA_HEADS)))

        b_last = b_ref[pl.ds(r0 + GLA_CHUNK - 1, 1), :]
        qd = (qs_c * jnp.exp(b_c)).astype(BF16)
        kd = k_ref[rows, :] * jnp.exp(b_last - b_c)
        e_last = jnp.exp(b_last)
        for h in range(GLA_HEADS):
            ks = slice(h * hdk, (h + 1) * hdk)
            vs = slice(h * hdv, (h + 1) * hdv)
            v_ch = v_ref[rows, vs]
            st = new_state_ref[h]
            o_ref[rows, vs] = (_dot(qd[:, ks], st.astype(BF16))
                               + _dot(a_heads[h].astype(BF16), v_ch))
            kd_t = jnp.transpose(kd[:, ks]).astype(BF16)
            new_state_ref[h] = st * _row_to_col(e_last[:, ks]) + _dot(kd_t, v_ch)


def _load_weights(wint_hbm, wa_hbm, wb_hbm, wo_hbm, poolw_hbm,
                  wbig_ref, wal_ref, wa_ref, wb_ref, wo_ref, poolw_ref,
                  stage_ref, gate_stage_ref, pool_stage_ref, sems, *, n_before_gate, rank):
    rows = WEIGHT_LOAD_ROWS
    d_in, d_model = wint_hbm.shape
    assert n_before_gate % rows == 0 and (d_in - rank) % rows == 0 and d_model % rows == 0
    jobs = []
    for j in range((d_in - rank) // rows):
        row0 = j * rows if j * rows < n_before_gate else j * rows + rank
        jobs.append((wint_hbm, row0, wbig_ref.at[:, j * rows:(j + 1) * rows]))
    for src, dst in ((wa_hbm, wa_ref), (wb_hbm, wb_ref), (wo_hbm, wo_ref)):
        for r0 in range(0, d_model, rows):
            jobs.append((src, r0, dst.at[r0:r0 + rows, :]))

    slots = stage_ref.shape[0]

    def staged_copy(i):
        src, row0, _ = jobs[i]
        return pltpu.make_async_copy(src.at[pl.ds(row0, rows), :], stage_ref.at[i % slots], sems.at[i % slots])

    gate_copy = pltpu.make_async_copy(wint_hbm.at[pl.ds(n_before_gate, rank), :], gate_stage_ref, sems.at[slots])
    pool_copy = pltpu.make_async_copy(poolw_hbm, pool_stage_ref, sems.at[slots + 1])
    gate_copy.start()
    pool_copy.start()
    ahead = slots - 1
    for i in range(min(ahead, len(jobs))):
        staged_copy(i).start()
    for i, (src, _, dst) in enumerate(jobs):
        if i + ahead < len(jobs):
            staged_copy(i + ahead).start()
        staged_copy(i).wait()
        block = stage_ref[i % slots]
        dst[...] = (jnp.transpose(block) if src is wint_hbm else block).astype(BF16)
    gate_copy.wait()
    wal_ref[...] = jnp.transpose(gate_stage_ref[...]).astype(BF16)
    pool_copy.wait()
    poolw_ref[...] = pool_stage_ref[...].astype(BF16)


def _block_kernel(x_ref, wint_hbm, wa_hbm, wb_hbm, wo_hbm, poolw_hbm,
                  wgu_ref, bg_ref, gnw_ref, poolb_ref, pools_ref, lnw_ref, lnb_ref,
                  tri_ref, band_ref, bandlb_ref,
                  out_ref,
                  qs_ref, k_ref, b_ref, v_ref, o_ref, state_ref, mid_state_ref, new_state_ref, ulast_ref,
                  ya_scale_ref, gate_a_ref, merged_b_ref, resid_ref, la_ref, factorisable_ref,
                  wbig_ref, wal_ref, wa_ref, wb_ref, wo_ref, poolw_ref,
                  stage_ref, gate_stage_ref, pool_stage_ref, load_sems,
                  *, d_model, dk, dv, alpha, tiles_per_seq, n_tiles):
    ts = SEQ_TILE
    hdk = dk // GLA_HEADS
    hdv = dv // GLA_HEADS
    s = pl.program_id(0)
    t = lax.rem(jnp.minimum(s, n_tiles - 1), tiles_per_seq)
    fast_chunks = [slice(r0, r0 + GLA_FAST_CHUNK) for r0 in range(0, ts, GLA_FAST_CHUNK)]
    assert len(fast_chunks) == 2

    @pl.when(s == 0)
    def _():
        _load_weights(wint_hbm, wa_hbm, wb_hbm, wo_hbm, poolw_hbm,
                      wbig_ref, wal_ref, wa_ref, wb_ref, wo_ref, poolw_ref,
                      stage_ref, gate_stage_ref, pool_stage_ref, load_sems,
                      n_before_gate=2 * dk + 2 * dv, rank=GLA_GATE_RANK)

    @pl.when(t == 0)
    def _():
        state_ref[...] = jnp.zeros_like(state_ref)
        ulast_ref[...] = jnp.zeros_like(ulast_ref)

    def step(do_head, do_tail):
        if do_head:
            xf = x_ref[...]
            xb = xf.astype(BF16)

            def proj(n):
                return _dot(xb, wbig_ref[:, n * d_model:(n + 1) * d_model])

        if do_tail:
            o = o_ref[...]
            y_a_parts = []
            for h in range(GLA_HEADS):
                oh = o[:, h * hdv:(h + 1) * hdv]
                y_a_parts.append(oh * lax.rsqrt(jnp.mean(oh * oh, axis=-1, keepdims=True) + NORM_EPS))
            y_a = (jnp.concatenate(y_a_parts, axis=-1) * ya_scale_ref[...]).astype(BF16)

        if do_head:
            a_low = _dot(xb, wal_ref[...])
            qk = proj(_QK)
            qs_ref[...] = qk[:, :dk] * (hdk ** -0.5)
            k_ref[...] = qk[:, dk:]
            gate_pre = _dot(a_low.astype(BF16), wgu_ref[...].astype(BF16)) + bg_ref[...]
            log_a = _log_sigmoid(gate_pre) * (1.0 / GLA_GATE_TAU)
            la_ref[...] = log_a
            la_hi, la_lo = _hi_lo(log_a)

        if do_tail:
            merged = (gate_a_ref[...] * _dot(y_a, wa_ref[...]) + merged_b_ref[...]).astype(BF16)

        if do_head:
            for rows in fast_chunks:
                b_ref[rows, :] = _dot(tri_ref[...], la_hi[rows, :]) + _dot(tri_ref[...], la_lo[rows, :])
            min_b = functools.reduce(
                jnp.minimum, [jnp.min(b_ref[rows.stop - 1:rows.stop, :]) for rows in fast_chunks])
            factorisable_ref[0] = (min_b >= -GLA_FACTOR_MAX_DECAY).astype(jnp.int32)
            v_ref[...] = proj(_V).astype(BF16)
            gla_operands = [_gla_factor_operands(qs_ref, k_ref, b_ref, rows) for rows in fast_chunks]

        if do_tail:
            out_ref[...] = resid_ref[...] + _dot(merged, wo_ref[...])

        if do_head:
            u = proj(_U)
            ub = u.astype(BF16)
            t_abs = t * ts + lax.broadcasted_iota(jnp.int32, (ts, 1), 0)
            gdim = d_model // len(POOL_WINDOWS)
            gslices = [slice(gi * gdim, (gi + 1) * gdim) for gi in range(len(POOL_WINDOWS))]
            pooled_in = []
            for gi, gs in enumerate(gslices):
                parts = []
                for rows in fast_chunks:
                    look_back = (ulast_ref[:, gs] if rows.start == 0
                                 else ub[rows.start - POOL_LOOKBACK:rows.start, gs])
                    in_chunk = _dot(band_ref[gi], ub[rows, gs])
                    carried = _dot(bandlb_ref[gi], look_back)
                    parts += [in_chunk[:POOL_LOOKBACK] + carried, in_chunk[POOL_LOOKBACK:]]
                count = jnp.minimum(t_abs + 1, POOL_WINDOWS[gi]).astype(F32)
                pooled_in.append((jnp.concatenate(parts, axis=0) / count - u[:, gs]).astype(BF16))
            ulast_ref[...] = ub[ts - POOL_LOOKBACK:, :]

        if do_head:
            def gate_factor_columns(h):
                def issue():
                    cols = slice(h * hdv, (h + 1) * hdv)
                    g = _dot(xb, wbig_ref[:, _G * d_model + h * hdv:_G * d_model + (h + 1) * hdv])
                    ya_scale_ref[:, cols] = gnw_ref[:, cols] * (g * _sigmoid(g))
                return issue

            def merge_gate_columns(h):
                def issue():
                    cols = slice(_GATE_A * d_model + h * hdv, _GATE_A * d_model + (h + 1) * hdv)
                    gate_a_ref[:, h * hdv:(h + 1) * hdv] = _sigmoid(_dot(xb, wbig_ref[:, cols]))
                return issue

            chunk_states = (state_ref, mid_state_ref, new_state_ref)
            chunk_fillers = (gate_factor_columns, merge_gate_columns)
            for c, rows in enumerate(fast_chunks):
                _gla_chunk_factorised(gla_operands[c], v_ref, o_ref, rows, chunk_states[c], chunk_states[c + 1],
                                      hdk=hdk, hdv=hdv,
                                      fillers=[chunk_fillers[c](h) for h in range(GLA_HEADS)])

        if do_head:
            z = proj(_Z)
            silu_z = z * _sigmoid(z)

        if do_tail:
            r = out_ref[...]
            mu = jnp.mean(r, axis=-1, keepdims=True)
            rc = r - mu
            var = jnp.mean(rc * rc, axis=-1, keepdims=True)
            out_ref[...] = (rc * lax.rsqrt(var + NORM_EPS) * lnw_ref[...] + lnb_ref[...]).astype(out_ref.dtype)

        if do_head:
            p_parts =[_dot(pooled_in[gi], poolw_ref[gs, :]) for gi, gs in enumerate(gslices)]
            gate_b = _sigmoid(proj(_GATE_B))
            y_b = (jnp.concatenate(p_parts, axis=-1) + poolb_ref[...]) * pools_ref[...] * silu_z
            resid_ref[...] = alpha * xf
            merged_b_ref[...] = gate_b * _dot(y_b.astype(BF16), wb_ref[...])

    pl.when(s == 0)(functools.partial(step, True, False))
    pl.when(jnp.logical_and(s > 0, s < n_tiles))(functools.partial(step, True, True))
    pl.when(s == n_tiles)(functools.partial(step, False, True))

    @pl.when(s < n_tiles)
    def _():
        @pl.when(factorisable_ref[0] == 0)
        def _():
            span = GLA_FAST_CHUNK
            row = lax.broadcasted_iota(jnp.int32, (span, span), 0)
            col = lax.broadcasted_iota(jnp.int32, (span, span), 1)
            tri_chunk = jnp.where(row // GLA_CHUNK == col // GLA_CHUNK, tri_ref[...], jnp.zeros((), BF16))
            la_hi, la_lo = _hi_lo(la_ref[...])
            for rows in fast_chunks:
                b_ref[rows, :] = _dot(tri_chunk, la_hi[rows, :]) + _dot(tri_chunk, la_lo[rows, :])
            _gla_tile_direct(qs_ref, k_ref, b_ref, v_ref, o_ref, state_ref, new_state_ref, hdk=hdk, hdv=hdv)

        state_ref[...] = new_state_ref[...]


def _const_spec(shape):
    nd = len(shape)
    return pl.BlockSpec(shape, lambda s: (0,) * nd, pipeline_mode=pl.Buffered(1))


def _tri_matrix():
    r = np.arange(GLA_FAST_CHUNK)
    return jnp.asarray((r[None, :] <= r[:, None]).astype(np.float32), dtype=BF16)


def _band_matrices():
    r = np.arange(GLA_FAST_CHUNK)[:, None]
    c = np.arange(GLA_FAST_CHUNK)[None, :]
    c_lb = np.arange(POOL_LOOKBACK)[None, :] - POOL_LOOKBACK
    in_win = lambda d, w: ((d >= 0) & (d < w)).astype(np.float32)
    bands = np.stack([in_win(r - c, w) for w in POOL_WINDOWS])
    bands_lb = np.stack([in_win(r[:POOL_LOOKBACK] - c_lb, w) for w in POOL_WINDOWS])
    return jnp.asarray(bands, dtype=BF16), jnp.asarray(bands_lb, dtype=BF16)


def _layer(x, w_in_t, w_gate_up, b_gate, gn_w, pool_w, pool_b, pool_scale, w_a, w_b, w_o, ln_w, ln_b,
           *, alpha):
    bsz, seq, d_model = x.shape
    rank, dk = w_gate_up.shape
    dv = gn_w.shape[0]
    assert seq % SEQ_TILE == 0 and SEQ_TILE % GLA_FAST_CHUNK == 0 and GLA_FAST_CHUNK % GLA_CHUNK == 0
    assert dv == d_model and 2 * dk == d_model and rank == GLA_GATE_RANK
    assert pool_w.shape[0] == len(POOL_WINDOWS)
    d_in = w_in_t.shape[0]
    gdim = pool_w.shape[1]
    row = lambda a: a.reshape(1, -1)
    hbm_weights = (w_in_t, w_a, w_b, w_o, pool_w.reshape(d_model, gdim))
    small = (w_gate_up, row(b_gate), row(gn_w), row(pool_b), row(pool_scale), row(ln_w), row(ln_b),
             _tri_matrix(), *_band_matrices())
    operands = (x, *hbm_weights, *small)
    tiles_per_seq = seq // SEQ_TILE
    n_tiles = bsz * tiles_per_seq

    def tile_block(tile):
        return (tile // tiles_per_seq, tile % tiles_per_seq, 0)

    in_specs = [pl.BlockSpec((None, SEQ_TILE, d_model), lambda s: tile_block(jnp.minimum(s, n_tiles - 1)))]
    in_specs += [pl.BlockSpec(memory_space=pl.ANY) for _ in hbm_weights]
    in_specs += [_const_spec(a.shape) for a in small]
    kernel = functools.partial(_block_kernel, d_model=d_model, dk=dk, dv=dv, alpha=alpha,
                               tiles_per_seq=tiles_per_seq, n_tiles=n_tiles)
    state_shape = (GLA_HEADS, dk // GLA_HEADS, dv // GLA_HEADS)
    return pl.pallas_call(
        kernel,
        grid=(n_tiles + 1,),
        in_specs=in_specs,
        out_specs=pl.BlockSpec((None, SEQ_TILE, d_model), lambda s: tile_block(jnp.maximum(s - 1, 0))),
        out_shape=jax.ShapeDtypeStruct(x.shape, x.dtype),
        scratch_shapes=[
            pltpu.VMEM((SEQ_TILE, dk), F32),
            pltpu.VMEM((SEQ_TILE, dk), F32),
            pltpu.VMEM((SEQ_TILE, dk), F32),
            pltpu.VMEM((SEQ_TILE, dv), BF16),
            pltpu.VMEM((SEQ_TILE, dv), F32),
            pltpu.VMEM(state_shape, F32),
            pltpu.VMEM(state_shape, F32),
            pltpu.VMEM(state_shape, F32),
            pltpu.VMEM((POOL_LOOKBACK, d_model), BF16),
            pltpu.VMEM((SEQ_TILE, dv), F32),
            pltpu.VMEM((SEQ_TILE, d_model), F32),
            pltpu.VMEM((SEQ_TILE, d_model), F32),
            pltpu.VMEM((SEQ_TILE, d_model), F32),
            pltpu.VMEM((SEQ_TILE, dk), F32),
            pltpu.SMEM((1,), jnp.int32),
            pltpu.VMEM((d_model, d_in - rank), BF16),
            pltpu.VMEM((d_model, rank), BF16),
            pltpu.VMEM(w_a.shape, BF16),
            pltpu.VMEM(w_b.shape, BF16),
            pltpu.VMEM(w_o.shape, BF16),
            pltpu.VMEM((d_model, gdim), BF16),
            pltpu.VMEM((WEIGHT_STAGE_SLOTS, WEIGHT_LOAD_ROWS, d_model), F32),
            pltpu.VMEM((rank, d_model), F32),
            pltpu.VMEM((d_model, gdim), F32),
            pltpu.SemaphoreType.DMA((WEIGHT_STAGE_SLOTS + 2,)),
        ],
        compiler_params=pltpu.CompilerParams(
            dimension_semantics=("arbitrary",),
            vmem_limit_bytes=VMEM_LIMIT_BYTES),
        name="hybrid_gla_pool_layer",
    )(*operands)


def kernel(x, w_in, w_gate_up, b_gate, gn_w, pool_w, pool_b, pool_scale, w_a, w_b, w_o, ln_w, ln_b):
    depth = w_in.shape[0]
    alpha = (2.0 * depth) ** 0.25
    w_in_t = jnp.swapaxes(w_in, 1, 2)
    for l in range(depth):
        x = _layer(x, w_in_t[l], w_gate_up[l], b_gate[l], gn_w[l], pool_w[l], pool_b[l], pool_scale[l],
                   w_a[l], w_b[l], w_o[l], ln_w[l], ln_b[l], alpha=alpha)
    return x
```

```python
import functools

import numpy as np
import jax
import jax.numpy as jnp
from jax import lax
from jax.experimental import pallas as pl
from jax.experimental.pallas import tpu as pltpu

GLA_HEADS = 4
GLA_GATE_RANK = 16
GLA_GATE_TAU = 16.0
POOL_WINDOWS = (2, 4, 8, 16)
POOL_LOOKBACK = 16
NORM_EPS = 1e-5

SEQ_TILE = 512
GLA_FAST_CHUNK = 256
GLA_CHUNK = 64
WEIGHT_LOAD_ROWS = 256
WEIGHT_STAGE_SLOTS = 8
VMEM_LIMIT_BYTES = 62 * 1024 * 1024
GLA_FACTOR_MAX_DECAY = 50.0

_QK, _V, _G, _U, _Z, _GATE_A, _GATE_B = range(7)

F32 = jnp.float32
BF16 = jnp.bfloat16
F32_SUBLANES = 8


def _dot(a, b):
    return jnp.dot(a, b, preferred_element_type=F32)


def _sigmoid(x):
    return 1.0 / (1.0 + jnp.exp(-x))


def _log_sigmoid(x):
    return jnp.minimum(x, 0.0) - jnp.log(1.0 + jnp.exp(-jnp.abs(x)))


def _hi_lo(x):
    hi = x.astype(BF16)
    return hi, (x - hi.astype(F32)).astype(BF16)


def _row_to_col(row):
    return jnp.transpose(jnp.broadcast_to(row, (F32_SUBLANES, row.shape[1])))[:, 0:1]


def _gla_factor_operands(qs_ref, k_ref, b_ref, rows):
    b = b_ref[rows, :]
    e_pos = jnp.exp(b)
    e_last = e_pos[GLA_FAST_CHUNK - 1:GLA_FAST_CHUNK, :]
    k_neg = k_ref[rows, :] * jnp.exp(-b)
    qd = (qs_ref[rows, :] * e_pos).astype(BF16)
    kn = k_neg.astype(BF16)
    kd = k_neg * e_last
    return qd, kn, kd, e_last


def _gla_chunk_factorised(operands, v_ref, o_ref, rows, state_ref, new_state_ref, *, hdk, hdv, fillers):
    ts = GLA_FAST_CHUNK
    qd, kn, kd, e_last = operands
    heads = range(GLA_HEADS)
    ks = [slice(h * hdk, (h + 1) * hdk) for h in heads]
    vs = [slice(h * hdv, (h + 1) * hdv) for h in heads]
    causal = (lax.broadcasted_iota(jnp.int32, (ts, ts), 0)
              >= lax.broadcasted_iota(jnp.int32, (ts, ts), 1))
    masked = []
    for h in heads:
        scores = lax.dot_general(qd[:, ks[h]], kn[:, ks[h]], (((1,), (1,)), ((), ())),
                                 preferred_element_type=F32)
        fillers[h]()
        masked.append(jnp.where(causal, scores, 0.0).astype(BF16))
    for h in heads:
        kd_t = jnp.transpose(kd[:, ks[h]]).astype(BF16)
        new_state_ref[h] = state_ref[h] * _row_to_col(e_last[:, ks[h]]) + _dot(kd_t, v_ref[rows, vs[h]])
    for h in heads:
        o_ref[rows, vs[h]] = (_dot(qd[:, ks[h]], state_ref[h].astype(BF16))
                              + _dot(masked[h], v_ref[rows, vs[h]]))


def _gla_tile_direct(qs_ref, k_ref, b_ref, v_ref, o_ref, state_ref, new_state_ref, *, hdk, hdv):
    new_state_ref[...] = state_ref[...]
    row_c = lax.broadcasted_iota(jnp.int32, (GLA_CHUNK, 1), 0)
    lane_c = lax.broadcasted_iota(jnp.int32, (GLA_CHUNK, GLA_CHUNK), 1)

    @pl.loop(0, SEQ_TILE // GLA_CHUNK)
    def _(c):
        r0 = pl.multiple_of(c * GLA_CHUNK, GLA_CHUNK)
        rows = pl.ds(r0, GLA_CHUNK)
        qs_c = qs_ref[rows, :]
        b_c = b_ref[rows, :]

        def col_body(j, a_heads):
            kj = k_ref[pl.ds(r0 + j, 1), :]
            bj = b_ref[pl.ds(r0 + j, 1), :]
            p = qs_c * kj * jnp.exp(jnp.minimum(b_c - bj, 0.0))
            p = jnp.where(row_c >= j, p, 0.0)
            new = []
            for h in range(GLA_HEADS):
                s = jnp.sum(p[:, h * hdk:(h + 1) * hdk], axis=-1, keepdims=True)
                new.append(jnp.where(lane_c == j, s, a_heads[h]))
            return tuple(new)

        a_heads = lax.fori_loop(
            0, GLA_CHUNK, col_body,
            tuple(jnp.zeros((GLA_CHUNK, GLA_CHUNK), F32) for _ in range(GLA_HEADS)))

        b_last = b_ref[pl.ds(r0 + GLA_CHUNK - 1, 1), :]
        qd = (qs_c * jnp.exp(b_c)).astype(BF16)
        kd = k_ref[rows, :] * jnp.exp(b_last - b_c)
        e_last = jnp.exp(b_last)
        for h in range(GLA_HEADS):
            ks = slice(h * hdk, (h + 1) * hdk)
            vs = slice(h * hdv, (h + 1) * hdv)
            v_ch = v_ref[rows, vs]
            st = new_state_ref[h]
            o_ref[rows, vs] = (_dot(qd[:, ks], st.astype(BF16))
                               + _dot(a_heads[h].astype(BF16), v_ch))
            kd_t = jnp.transpose(kd[:, ks]).astype(BF16)
            new_state_ref[h] = st * _row_to_col(e_last[:, ks]) + _dot(kd_t, v_ch)


def _load_weights(wint_hbm, wa_hbm, wb_hbm, wo_hbm, poolw_hbm,
                  wbig_ref, wal_ref, wa_ref, wb_ref, wo_ref, poolw_ref,
                  stage_ref, gate_stage_ref, pool_stage_ref, sems, *, n_before_gate, rank):
    rows = WEIGHT_LOAD_ROWS
    d_in, d_model = wint_hbm.shape
    assert n_before_gate % rows == 0 and (d_in - rank) % rows == 0 and d_model % rows == 0
    jobs = []
    for j in range((d_in - rank) // rows):
        row0 = j * rows if j * rows < n_before_gate else j * rows + rank
        jobs.append((wint_hbm, row0, wbig_ref.at[:, j * rows:(j + 1) * rows]))
    for src, dst in ((wa_hbm, wa_ref), (wb_hbm, wb_ref), (wo_hbm, wo_ref)):
        for r0 in range(0, d_model, rows):
            jobs.append((src, r0, dst.at[r0:r0 + rows, :]))

    slots = stage_ref.shape[0]

    def staged_copy(i):
        src, row0, _ = jobs[i]
        return pltpu.make_async_copy(src.at[pl.ds(row0, rows), :], stage_ref.at[i % slots], sems.at[i % slots])

    gate_copy = pltpu.make_async_copy(wint_hbm.at[pl.ds(n_before_gate, rank), :], gate_stage_ref, sems.at[slots])
    pool_copy = pltpu.make_async_copy(poolw_hbm, pool_stage_ref, sems.at[slots + 1])
    gate_copy.start()
    pool_copy.start()
    ahead = slots - 1
    for i in range(min(ahead, len(jobs))):
        staged_copy(i).start()
    for i, (src, _, dst) in enumerate(jobs):
        if i + ahead < len(jobs):
            staged_copy(i + ahead).start()
        staged_copy(i).wait()
        block = stage_ref[i % slots]
        dst[...] = (jnp.transpose(block) if src is wint_hbm else block).astype(BF16)
    gate_copy.wait()
    wal_ref[...] = jnp.transpose(gate_stage_ref[...]).astype(BF16)
    pool_copy.wait()
    poolw_ref[...] = pool_stage_ref[...].astype(BF16)


def _block_kernel(x_ref, wint_hbm, wa_hbm, wb_hbm, wo_hbm, poolw_hbm,
                  wgu_ref, bg_ref, gnw_ref, poolb_ref, pools_ref, lnw_ref, lnb_ref,
                  tri_ref, band_ref, bandlb_ref,
                  out_ref,
                  qs_ref, k_ref, b_ref, v_ref, o_ref, state_ref, mid_state_ref, new_state_ref, ulast_ref,
                  ya_scale_ref, gate_a_ref, merged_b_ref, resid_ref, la_ref, factorisable_ref,
                  wbig_ref, wal_ref, wa_ref, wb_ref, wo_ref, poolw_ref,
                  stage_ref, gate_stage_ref, pool_stage_ref, load_sems,
                  *, d_model, dk, dv, alpha, tiles_per_seq, n_tiles):
    ts = SEQ_TILE
    hdk = dk // GLA_HEADS
    hdv = dv // GLA_HEADS
    s = pl.program_id(0)
    t = lax.rem(jnp.minimum(s, n_tiles - 1), tiles_per_seq)
    fast_chunks = [slice(r0, r0 + GLA_FAST_CHUNK) for r0 in range(0, ts, GLA_FAST_CHUNK)]
    assert len(fast_chunks) == 2

    @pl.when(s == 0)
    def _():
        _load_weights(wint_hbm, wa_hbm, wb_hbm, wo_hbm, poolw_hbm,
                      wbig_ref, wal_ref, wa_ref, wb_ref, wo_ref, poolw_ref,
                      stage_ref, gate_stage_ref, pool_stage_ref, load_sems,
                      n_before_gate=2 * dk + 2 * dv, rank=GLA_GATE_RANK)
        for ref in (o_ref, ya_scale_ref, gate_a_ref, merged_b_ref, resid_ref):
            @pl.loop(0, ts // GLA_CHUNK)
            def _(i, ref=ref):
                ref[pl.ds(pl.multiple_of(i * GLA_CHUNK, GLA_CHUNK), GLA_CHUNK), :] = jnp.zeros(
                    (GLA_CHUNK, ref.shape[1]), ref.dtype)

    @pl.when(t == 0)
    def _():
        state_ref[...] = jnp.zeros_like(state_ref)
        ulast_ref[...] = jnp.zeros_like(ulast_ref)

    def step(do_head, do_tail):
        if do_head:
            xf = x_ref[...]
            xb = xf.astype(BF16)

            def proj(n):
                return _dot(xb, wbig_ref[:, n * d_model:(n + 1) * d_model])

        if do_tail:
            o = o_ref[...]
            y_a_parts = []
            for h in range(GLA_HEADS):
                oh = o[:, h * hdv:(h + 1) * hdv]
                y_a_parts.append(oh * lax.rsqrt(jnp.mean(oh * oh, axis=-1, keepdims=True) + NORM_EPS))
            y_a = (jnp.concatenate(y_a_parts, axis=-1) * ya_scale_ref[...]).astype(BF16)

        if do_head:
            a_low = _dot(xb, wal_ref[...])
            qk = proj(_QK)
            qs_ref[...] = qk[:, :dk] * (hdk ** -0.5)
            k_ref[...] = qk[:, dk:]
            gate_pre = _dot(a_low.astype(BF16), wgu_ref[...].astype(BF16)) + bg_ref[...]
            log_a = _log_sigmoid(gate_pre) * (1.0 / GLA_GATE_TAU)
            la_ref[...] = log_a
            la_hi, la_lo = _hi_lo(log_a)

        if do_tail:
            merged = (gate_a_ref[...] * _dot(y_a, wa_ref[...]) + merged_b_ref[...]).astype(BF16)

        if do_head:
            for rows in fast_chunks:
                b_ref[rows, :] = _dot(tri_ref[...], la_hi[rows, :]) + _dot(tri_ref[...], la_lo[rows, :])
            min_b = functools.reduce(
                jnp.minimum, [jnp.min(b_ref[rows.stop - 1:rows.stop, :]) for rows in fast_chunks])
            factorisable_ref[0] = (min_b >= -GLA_FACTOR_MAX_DECAY).astype(jnp.int32)
            v_ref[...] = proj(_V).astype(BF16)
            gla_operands = [_gla_factor_operands(qs_ref, k_ref, b_ref, rows) for rows in fast_chunks]

        if do_tail:
            out_ref[...] = resid_ref[...] + _dot(merged, wo_ref[...])

        if do_head:
            u = proj(_U)
            ub = u.astype(BF16)
            t_abs = t * ts + lax.broadcasted_iota(jnp.int32, (ts, 1), 0)
            gdim = d_model // len(POOL_WINDOWS)
            gslices = [slice(gi * gdim, (gi + 1) * gdim) for gi in range(len(POOL_WINDOWS))]
            pooled_in = []
            for gi, gs in enumerate(gslices):
                parts = []
                for rows in fast_chunks:
                    look_back = (ulast_ref[:, gs] if rows.start == 0
                                 else ub[rows.start - POOL_LOOKBACK:rows.start, gs])
                    in_chunk = _dot(band_ref[gi], ub[rows, gs])
                    carried = _dot(bandlb_ref[gi], look_back)
                    parts += [in_chunk[:POOL_LOOKBACK] + carried, in_chunk[POOL_LOOKBACK:]]
                count = jnp.minimum(t_abs + 1, POOL_WINDOWS[gi]).astype(F32)
                pooled_in.append((jnp.concatenate(parts, axis=0) / count - u[:, gs]).astype(BF16))
            ulast_ref[...] = ub[ts - POOL_LOOKBACK:, :]

        if do_head:
            def gate_factor_columns(h):
                def issue():
                    cols = slice(h * hdv, (h + 1) * hdv)
                    g = _dot(xb, wbig_ref[:, _G * d_model + h * hdv:_G * d_model + (h + 1) * hdv])
                    ya_scale_ref[:, cols] = gnw_ref[:, cols] * (g * _sigmoid(g))
                return issue

            def merge_gate_columns(h):
                def issue():
                    cols = slice(_GATE_A * d_model + h * hdv, _GATE_A * d_model + (h + 1) * hdv)
                    gate_a_ref[:, h * hdv:(h + 1) * hdv] = _sigmoid(_dot(xb, wbig_ref[:, cols]))
                return issue

            chunk_states = (state_ref, mid_state_ref, new_state_ref)
            chunk_fillers = (gate_factor_columns, merge_gate_columns)
            for c, rows in enumerate(fast_chunks):
                _gla_chunk_factorised(gla_operands[c], v_ref, o_ref, rows, chunk_states[c], chunk_states[c + 1],
                                      hdk=hdk, hdv=hdv,
                                      fillers=[chunk_fillers[c](h) for h in range(GLA_HEADS)])

        if do_head:
            z = proj(_Z)
            silu_z = z * _sigmoid(z)

        if do_tail:
            r = out_ref[...]
            mu = jnp.mean(r, axis=-1, keepdims=True)
            rc = r - mu
            var = jnp.mean(rc * rc, axis=-1, keepdims=True)
            out_ref[...] = (rc * lax.rsqrt(var + NORM_EPS) * lnw_ref[...] + lnb_ref[...]).astype(out_ref.dtype)

        if do_head:
            p_parts =[_dot(pooled_in[gi], poolw_ref[gs, :]) for gi, gs in enumerate(gslices)]
            gate_b = _sigmoid(proj(_GATE_B))
            y_b = (jnp.concatenate(p_parts, axis=-1) + poolb_ref[...]) * pools_ref[...] * silu_z
            resid_ref[...] = alpha * xf
            merged_b_ref[...] = gate_b * _dot(y_b.astype(BF16), wb_ref[...])

    pl.when(s < n_tiles)(functools.partial(step, True, True))
    pl.when(s == n_tiles)(functools.partial(step, False, True))

    @pl.when(s < n_tiles)
    def _():
        @pl.when(factorisable_ref[0] == 0)
        def _():
            span = GLA_FAST_CHUNK
            row = lax.broadcasted_iota(jnp.int32, (span, span), 0)
            col = lax.broadcasted_iota(jnp.int32, (span, span), 1)
            tri_chunk = jnp.where(row // GLA_CHUNK == col // GLA_CHUNK, tri_ref[...], jnp.zeros((), BF16))
            la_hi, la_lo = _hi_lo(la_ref[...])
            for rows in fast_chunks:
                b_ref[rows, :] = _dot(tri_chunk, la_hi[rows, :]) + _dot(tri_chunk, la_lo[rows, :])
            _gla_tile_direct(qs_ref, k_ref, b_ref, v_ref, o_ref, state_ref, new_state_ref, hdk=hdk, hdv=hdv)

        state_ref[...] = new_state_ref[...]


def _const_spec(shape):
    nd = len(shape)
    return pl.BlockSpec(shape, lambda s: (0,) * nd, pipeline_mode=pl.Buffered(1))


def _tri_matrix():
    r = np.arange(GLA_FAST_CHUNK)
    return jnp.asarray((r[None, :] <= r[:, None]).astype(np.float32), dtype=BF16)


def _band_matrices():
    r = np.arange(GLA_FAST_CHUNK)[:, None]
    c = np.arange(GLA_FAST_CHUNK)[None, :]
    c_lb = np.arange(POOL_LOOKBACK)[None, :] - POOL_LOOKBACK
    in_win = lambda d, w: ((d >= 0) & (d < w)).astype(np.float32)
    bands = np.stack([in_win(r - c, w) for w in POOL_WINDOWS])
    bands_lb = np.stack([in_win(r[:POOL_LOOKBACK] - c_lb, w) for w in POOL_WINDOWS])
    return jnp.asarray(bands, dtype=BF16), jnp.asarray(bands_lb, dtype=BF16)


def _layer(x, w_in_t, w_gate_up, b_gate, gn_w, pool_w, pool_b, pool_scale, w_a, w_b, w_o, ln_w, ln_b,
           *, alpha):
    bsz, seq, d_model = x.shape
    rank, dk = w_gate_up.shape
    dv = gn_w.shape[0]
    assert seq % SEQ_TILE == 0 and SEQ_TILE % GLA_FAST_CHUNK == 0 and GLA_FAST_CHUNK % GLA_CHUNK == 0
    assert dv == d_model and 2 * dk == d_model and rank == GLA_GATE_RANK
    assert pool_w.shape[0] == len(POOL_WINDOWS)
    d_in = w_in_t.shape[0]
    gdim = pool_w.shape[1]
    row = lambda a: a.reshape(1, -1)
    hbm_weights = (w_in_t, w_a, w_b, w_o, pool_w.reshape(d_model, gdim))
    small = (w_gate_up, row(b_gate), row(gn_w), row(pool_b), row(pool_scale), row(ln_w), row(ln_b),
             _tri_matrix(), *_band_matrices())
    operands = (x, *hbm_weights, *small)
    tiles_per_seq = seq // SEQ_TILE
    n_tiles = bsz * tiles_per_seq

    def tile_block(tile):
        return (tile // tiles_per_seq, tile % tiles_per_seq, 0)

    in_specs = [pl.BlockSpec((None, SEQ_TILE, d_model), lambda s: tile_block(jnp.minimum(s, n_tiles - 1)))]
    in_specs += [pl.BlockSpec(memory_space=pl.ANY) for _ in hbm_weights]
    in_specs += [_const_spec(a.shape) for a in small]
    kernel = functools.partial(_block_kernel, d_model=d_model, dk=dk, dv=dv, alpha=alpha,
                               tiles_per_seq=tiles_per_seq, n_tiles=n_tiles)
    state_shape = (GLA_HEADS, dk // GLA_HEADS, dv // GLA_HEADS)
    return pl.pallas_call(
        kernel,
        grid=(n_tiles + 1,),
        in_specs=in_specs,
        out_specs=pl.BlockSpec((None, SEQ_TILE, d_model), lambda s: tile_block(jnp.maximum(s - 1, 0))),
        out_shape=jax.ShapeDtypeStruct(x.shape, x.dtype),
        scratch_shapes=[
            pltpu.VMEM((SEQ_TILE, dk), F32),
            pltpu.VMEM((SEQ_TILE, dk), F32),
            pltpu.VMEM((SEQ_TILE, dk), F32),
            pltpu.VMEM((SEQ_TILE, dv), BF16),
            pltpu.VMEM((SEQ_TILE, dv), F32),
            pltpu.VMEM(state_shape, F32),
            pltpu.VMEM(state_shape, F32),
            pltpu.VMEM(state_shape, F32),
            pltpu.VMEM((POOL_LOOKBACK, d_model), BF16),
            pltpu.VMEM((SEQ_TILE, dv), F32),
            pltpu.VMEM((SEQ_TILE, d_model), F32),
            pltpu.VMEM((SEQ_TILE, d_model), F32),
            pltpu.VMEM((SEQ_TILE, d_model), F32),
            pltpu.VMEM((SEQ_TILE, dk), F32),
            pltpu.SMEM((1,), jnp.int32),
            pltpu.VMEM((d_model, d_in - rank), BF16),
            pltpu.VMEM((d_model, rank), BF16),
            pltpu.VMEM(w_a.shape, BF16),
            pltpu.VMEM(w_b.shape, BF16),
            pltpu.VMEM(w_o.shape, BF16),
            pltpu.VMEM((d_model, gdim), BF16),
            pltpu.VMEM((WEIGHT_STAGE_SLOTS, WEIGHT_LOAD_ROWS, d_model), F32),
            pltpu.VMEM((rank, d_model), F32),
            pltpu.VMEM((d_model, gdim), F32),
            pltpu.SemaphoreType.DMA((WEIGHT_STAGE_SLOTS + 2,)),
        ],
        compiler_params=pltpu.CompilerParams(
            dimension_semantics=("arbitrary",),
            vmem_limit_bytes=VMEM_LIMIT_BYTES),
        name="hybrid_gla_pool_layer",
    )(*operands)


def kernel(x, w_in, w_gate_up, b_gate, gn_w, pool_w, pool_b, pool_scale, w_a, w_b, w_o, ln_w, ln_b):
    depth = w_in.shape[0]
    alpha = (2.0 * depth) ** 0.25
    w_in_t = jnp.swapaxes(w_in, 1, 2)
    for l in range(depth):
        x = _layer(x, w_in_t[l], w_gate_up[l], b_gate[l], gn_w[l], pool_w[l], pool_b[l], pool_scale[l],
                   w_a[l], w_b[l], w_o[l], ln_w[l], ln_b[l], alpha=alpha)
    return x
```

```python
import functools

import numpy as np
import jax
import jax.numpy as jnp
from jax import lax
from jax.experimental import pallas as pl
from jax.experimental.pallas import tpu as pltpu

GLA_HEADS = 4
GLA_GATE_RANK = 16
GLA_GATE_TAU = 16.0
POOL_WINDOWS = (2, 4, 8, 16)
POOL_LOOKBACK = 16
NORM_EPS = 1e-5

SEQ_TILE = 512
GLA_FAST_CHUNK = 256
GLA_CHUNK = 64
WEIGHT_LOAD_ROWS = 256
WEIGHT_STAGE_SLOTS = 6
VMEM_LIMIT_BYTES = 60 * 1024 * 1024
GLA_FACTOR_MAX_DECAY = 50.0

_QK, _V, _G, _U, _Z, _GATE_A, _GATE_B = range(7)
GATE_PAD = 128

F32 = jnp.float32
BF16 = jnp.bfloat16
F32_SUBLANES = 8


def _dot(a, b):
    return jnp.dot(a, b, preferred_element_type=F32)


def _sigmoid(x):
    return 1.0 / (1.0 + jnp.exp(-x))


def _log_sigmoid(x):
    return jnp.minimum(x, 0.0) - jnp.log(1.0 + jnp.exp(-jnp.abs(x)))


def _hi_lo(x):
    hi = x.astype(BF16)
    return hi, (x - hi.astype(F32)).astype(BF16)


def _row_to_col(row):
    return jnp.transpose(jnp.broadcast_to(row, (F32_SUBLANES, row.shape[1])))[:, 0:1]


def _gla_factor_operands(qs_ref, k_ref, b_ref, rows):
    b = b_ref[rows, :]
    e_pos = jnp.exp(b)
    e_last = e_pos[GLA_FAST_CHUNK - 1:GLA_FAST_CHUNK, :]
    k_neg = k_ref[rows, :] * jnp.exp(-b)
    qd = (qs_ref[rows, :] * e_pos).astype(BF16)
    kn = k_neg.astype(BF16)
    kd = k_neg * e_last
    return qd, kn, kd, e_last


def _gla_chunk_factorised(operands, v_ref, o_ref, rows, state_ref, new_state_ref, *, hdk, hdv, fillers):
    ts = GLA_FAST_CHUNK
    qd, kn, kd, e_last = operands
    heads = range(GLA_HEADS)
    ks = [slice(h * hdk, (h + 1) * hdk) for h in heads]
    vs = [slice(h * hdv, (h + 1) * hdv) for h in heads]
    causal = (lax.broadcasted_iota(jnp.int32, (ts, ts), 0)
              >= lax.broadcasted_iota(jnp.int32, (ts, ts), 1))
    masked = []
    for h in heads:
        scores = lax.dot_general(qd[:, ks[h]], kn[:, ks[h]], (((1,), (1,)), ((), ())),
                                 preferred_element_type=F32)
        fillers[h]()
        masked.append(jnp.where(causal, scores, 0.0).astype(BF16))
    for h in heads:
        kd_t = jnp.transpose(kd[:, ks[h]]).astype(BF16)
        new_state_ref[h] = state_ref[h] * _row_to_col(e_last[:, ks[h]]) + _dot(kd_t, v_ref[rows, vs[h]])
    for h in heads:
        o_ref[rows, vs[h]] = (_dot(qd[:, ks[h]], state_ref[h].astype(BF16))
                              + _dot(masked[h], v_ref[rows, vs[h]]))


def _gla_tile_direct(qs_ref, k_ref, b_ref, v_ref, o_ref, state_ref, new_state_ref, *, hdk, hdv):
    new_state_ref[...] = state_ref[...]
    row_c = lax.broadcasted_iota(jnp.int32, (GLA_CHUNK, 1), 0)
    lane_c = lax.broadcasted_iota(jnp.int32, (GLA_CHUNK, GLA_CHUNK), 1)

    @pl.loop(0, SEQ_TILE // GLA_CHUNK)
    def _(c):
        r0 = pl.multiple_of(c * GLA_CHUNK, GLA_CHUNK)
        rows = pl.ds(r0, GLA_CHUNK)
        qs_c = qs_ref[rows, :]
        b_c = b_ref[rows, :]

        def col_body(j, a_heads):
            kj = k_ref[pl.ds(r0 + j, 1), :]
            bj = b_ref[pl.ds(r0 + j, 1), :]
            p = qs_c * kj * jnp.exp(jnp.minimum(b_c - bj, 0.0))
            p = jnp.where(row_c >= j, p, 0.0)
            new = []
            for h in range(GLA_HEADS):
                s = jnp.sum(p[:, h * hdk:(h + 1) * hdk], axis=-1, keepdims=True)
                new.append(jnp.where(lane_c == j, s, a_heads[h]))
            return tuple(new)

        a_heads = lax.fori_loop(
            0, GLA_CHUNK, col_body,
            tuple(jnp.zeros((GLA_CHUNK, GLA_CHUNK), F32) for _ in range(GLA_HEADS)))

        b_last = b_ref[pl.ds(r0 + GLA_CHUNK - 1, 1), :]
        qd = (qs_c * jnp.exp(b_c)).astype(BF16)
        kd = k_ref[rows, :] * jnp.exp(b_last - b_c)
        e_last = jnp.exp(b_last)
        for h in range(GLA_HEADS):
            ks = slice(h * hdk, (h + 1) * hdk)
            vs = slice(h * hdv, (h + 1) * hdv)
            v_ch = v_ref[rows, vs]
            st = new_state_ref[h]
            o_ref[rows, vs] = (_dot(qd[:, ks], st.astype(BF16))
                               + _dot(a_heads[h].astype(BF16), v_ch))
            kd_t = jnp.transpose(kd[:, ks]).astype(BF16)
            new_state_ref[h] = st * _row_to_col(e_last[:, ks]) + _dot(kd_t, v_ch)


def _load_weights(wint_hbm, wa_hbm, wb_hbm, wo_hbm, poolw_hbm,
                  wbig_ref, wa_ref, wb_ref, wo_ref, poolw_ref,
                  stage_ref, gate_stage_ref, pool_stage_ref, sems, *, n_before_gate, rank):
    rows = WEIGHT_LOAD_ROWS
    d_in, d_model = wint_hbm.shape
    assert n_before_gate % rows == 0 and (d_in - rank) % rows == 0 and d_model % rows == 0
    jobs = []
    for j in range((d_in - rank) // rows):
        row0 = j * rows if j * rows < n_before_gate else j * rows + rank
        jobs.append((wint_hbm, row0, wbig_ref.at[:, GATE_PAD + j * rows:GATE_PAD + (j + 1) * rows]))
    for src, dst in ((wa_hbm, wa_ref), (wb_hbm, wb_ref), (wo_hbm, wo_ref)):
        for r0 in range(0, d_model, rows):
            jobs.append((src, r0, dst.at[r0:r0 + rows, :]))

    slots = stage_ref.shape[0]

    def staged_copy(i):
        src, row0, _ = jobs[i]
        return pltpu.make_async_copy(src.at[pl.ds(row0, rows), :], stage_ref.at[i % slots], sems.at[i % slots])

    gate_copy = pltpu.make_async_copy(wint_hbm.at[pl.ds(n_before_gate, rank), :], gate_stage_ref, sems.at[slots])
    pool_copy = pltpu.make_async_copy(poolw_hbm, pool_stage_ref, sems.at[slots + 1])
    gate_copy.start()
    pool_copy.start()
    ahead = slots - 1
    for i in range(min(ahead, len(jobs))):
        staged_copy(i).start()
    for i, (src, _, dst) in enumerate(jobs):
        if i + ahead < len(jobs):
            staged_copy(i + ahead).start()
        staged_copy(i).wait()
        block = stage_ref[i % slots]
        dst[...] = (jnp.transpose(block) if src is wint_hbm else block).astype(BF16)
    gate_copy.wait()
    wbig_ref[:, 0:GATE_PAD] = jnp.zeros((d_model, GATE_PAD), BF16)
    wbig_ref[:, 0:rank] = jnp.transpose(gate_stage_ref[...]).astype(BF16)
    pool_copy.wait()
    poolw_ref[...] = pool_stage_ref[...].astype(BF16)


def _block_kernel(x_ref, wint_hbm, wa_hbm, wb_hbm, wo_hbm, poolw_hbm,
                  wgu_ref, bg_ref, gnw_ref, poolb_ref, pools_ref, lnw_ref, lnb_ref,
                  tri_ref, band_ref, bandlb_ref,
                  out_ref,
                  qs_ref, k_ref, b_ref, v_ref, o_ref, state_ref, mid_state_ref, new_state_ref, ulast_ref,
                  ya_scale_ref, gate_a_ref, merged_b_ref, resid_ref, la_ref, factorisable_ref,
                  wbig_ref, wa_ref, wb_ref, wo_ref, poolw_ref,
                  stage_ref, gate_stage_ref, pool_stage_ref, load_sems,
                  *, d_model, dk, dv, alpha, tiles_per_seq, n_tiles):
    ts = SEQ_TILE
    hdk = dk // GLA_HEADS
    hdv = dv // GLA_HEADS
    s = pl.program_id(0)
    t = lax.rem(jnp.minimum(s, n_tiles - 1), tiles_per_seq)
    fast_chunks = [slice(r0, r0 + GLA_FAST_CHUNK) for r0 in range(0, ts, GLA_FAST_CHUNK)]
    assert len(fast_chunks) == 2

    @pl.when(s == 0)
    def _():
        _load_weights(wint_hbm, wa_hbm, wb_hbm, wo_hbm, poolw_hbm,
                      wbig_ref, wa_ref, wb_ref, wo_ref, poolw_ref,
                      stage_ref, gate_stage_ref, pool_stage_ref, load_sems,
                      n_before_gate=2 * dk + 2 * dv, rank=GLA_GATE_RANK)
        for ref in (o_ref, ya_scale_ref, gate_a_ref, merged_b_ref, resid_ref):
            @pl.loop(0, ts // GLA_CHUNK)
            def _(i, ref=ref):
                ref[pl.ds(pl.multiple_of(i * GLA_CHUNK, GLA_CHUNK), GLA_CHUNK), :] = jnp.zeros(
                    (GLA_CHUNK, ref.shape[1]), ref.dtype)

    @pl.when(t == 0)
    def _():
        state_ref[...] = jnp.zeros_like(state_ref)
        ulast_ref[...] = jnp.zeros_like(ulast_ref)

    def step(do_head, do_tail):
        if do_head:
            xf = x_ref[...]
            xb = xf.astype(BF16)

            def proj(n):
                return _dot(xb, wbig_ref[:, GATE_PAD + n * d_model:GATE_PAD + (n + 1) * d_model])

        if do_tail:
            o = o_ref[...]
            y_a_parts = []
            for h in range(GLA_HEADS):
                oh = o[:, h * hdv:(h + 1) * hdv]
                y_a_parts.append(oh * lax.rsqrt(jnp.mean(oh * oh, axis=-1, keepdims=True) + NORM_EPS))
            y_a = (jnp.concatenate(y_a_parts, axis=-1) * ya_scale_ref[...]).astype(BF16)

        if do_head:
            gate_qk = _dot(xb, wbig_ref[:, 0:GATE_PAD + d_model])
            a_low = gate_qk[:, :GLA_GATE_RANK]
            qk = gate_qk[:, GATE_PAD:]
            qs_ref[...] = qk[:, :dk] * (hdk ** -0.5)
            k_ref[...] = qk[:, dk:]
            gate_pre = _dot(a_low.astype(BF16), wgu_ref[...].astype(BF16)) + bg_ref[...]
            log_a = _log_sigmoid(gate_pre) * (1.0 / GLA_GATE_TAU)
            la_ref[...] = log_a
            la_hi, la_lo = _hi_lo(log_a)

        if do_tail:
            merged = (gate_a_ref[...] * _dot(y_a, wa_ref[...]) + merged_b_ref[...]).astype(BF16)

        if do_head:
            for rows in fast_chunks:
                b_ref[rows, :] = _dot(tri_ref[...], la_hi[rows, :]) + _dot(tri_ref[...], la_lo[rows, :])
            min_b = functools.reduce(
                jnp.minimum, [jnp.min(b_ref[rows.stop - 1:rows.stop, :]) for rows in fast_chunks])
            factorisable_ref[0] = (min_b >= -GLA_FACTOR_MAX_DECAY).astype(jnp.int32)
            v_ref[...] = proj(_V).astype(BF16)
            gla_operands = [_gla_factor_operands(qs_ref, k_ref, b_ref, rows) for rows in fast_chunks]

        if do_tail:
            out_ref[...] = resid_ref[...] + _dot(merged, wo_ref[...])

        if do_head:
            u = proj(_U)
            ub = u.astype(BF16)
            t_abs = t * ts + lax.broadcasted_iota(jnp.int32, (ts, 1), 0)
            gdim = d_model // len(POOL_WINDOWS)
            gslices = [slice(gi * gdim, (gi + 1) * gdim) for gi in range(len(POOL_WINDOWS))]
            pooled_in = []
            for gi, gs in enumerate(gslices):
                parts = []
                for rows in fast_chunks:
                    look_back = (ulast_ref[:, gs] if rows.start == 0
                                 else ub[rows.start - POOL_LOOKBACK:rows.start, gs])
                    in_chunk = _dot(band_ref[gi], ub[rows, gs])
                    carried = _dot(bandlb_ref[gi], look_back)
                    parts += [in_chunk[:POOL_LOOKBACK] + carried, in_chunk[POOL_LOOKBACK:]]
                count = jnp.minimum(t_abs + 1, POOL_WINDOWS[gi]).astype(F32)
                pooled_in.append((jnp.concatenate(parts, axis=0) / count - u[:, gs]).astype(BF16))
            ulast_ref[...] = ub[ts - POOL_LOOKBACK:, :]

        if do_head:
            def gate_factor_columns(h):
                def issue():
                    cols = slice(h * hdv, (h + 1) * hdv)
                    g0 = GATE_PAD + _G * d_model + h * hdv
                    g = _dot(xb, wbig_ref[:, g0:g0 + hdv])
                    ya_scale_ref[:, cols] = gnw_ref[:, cols] * (g * _sigmoid(g))
                return issue

            def merge_gate_columns(h):
                def issue():
                    c0 = GATE_PAD + _GATE_A * d_model + h * hdv
                    cols = slice(c0, c0 + hdv)
                    gate_a_ref[:, h * hdv:(h + 1) * hdv] = _sigmoid(_dot(xb, wbig_ref[:, cols]))
                return issue

            chunk_states = (state_ref, mid_state_ref, new_state_ref)
            chunk_fillers = (gate_factor_columns, merge_gate_columns)
            for c, rows in enumerate(fast_chunks):
                _gla_chunk_factorised(gla_operands[c], v_ref, o_ref, rows, chunk_states[c], chunk_states[c + 1],
                                      hdk=hdk, hdv=hdv,
                                      fillers=[chunk_fillers[c](h) for h in range(GLA_HEADS)])

        if do_head:
            z = proj(_Z)
            silu_z = z * _sigmoid(z)

        if do_tail:
            r = out_ref[...]
            mu = jnp.mean(r, axis=-1, keepdims=True)
            rc = r - mu
            var = jnp.mean(rc * rc, axis=-1, keepdims=True)
            out_ref[...] = (rc * lax.rsqrt(var + NORM_EPS) * lnw_ref[...] + lnb_ref[...]).astype(out_ref.dtype)

        if do_head:
            p_parts =[_dot(pooled_in[gi], poolw_ref[gs, :]) for gi, gs in enumerate(gslices)]
            gate_b = _sigmoid(proj(_GATE_B))
            y_b = (jnp.concatenate(p_parts, axis=-1) + poolb_ref[...]) * pools_ref[...] * silu_z
            resid_ref[...] = alpha * xf
            merged_b_ref[...] = gate_b * _dot(y_b.astype(BF16), wb_ref[...])

    pl.when(s < n_tiles)(functools.partial(step, True, True))
    pl.when(s == n_tiles)(functools.partial(step, False, True))

    @pl.when(s < n_tiles)
    def _():
        @pl.when(factorisable_ref[0] == 0)
        def _():
            span = GLA_FAST_CHUNK
            row = lax.broadcasted_iota(jnp.int32, (span, span), 0)
            col = lax.broadcasted_iota(jnp.int32, (span, span), 1)
            tri_chunk = jnp.where(row // GLA_CHUNK == col // GLA_CHUNK, tri_ref[...], jnp.zeros((), BF16))
            la_hi, la_lo = _hi_lo(la_ref[...])
            for rows in fast_chunks:
                b_ref[rows, :] = _dot(tri_chunk, la_hi[rows, :]) + _dot(tri_chunk, la_lo[rows, :])
            _gla_tile_direct(qs_ref, k_ref, b_ref, v_ref, o_ref, state_ref, new_state_ref, hdk=hdk, hdv=hdv)

        state_ref[...] = new_state_ref[...]


def _const_spec(shape):
    nd = len(shape)
    return pl.BlockSpec(shape, lambda s: (0,) * nd, pipeline_mode=pl.Buffered(1))


def _tri_matrix():
    r = np.arange(GLA_FAST_CHUNK)
    return jnp.asarray((r[None, :] <= r[:, None]).astype(np.float32), dtype=BF16)


def _band_matrices():
    r = np.arange(GLA_FAST_CHUNK)[:, None]
    c = np.arange(GLA_FAST_CHUNK)[None, :]
    c_lb = np.arange(POOL_LOOKBACK)[None, :] - POOL_LOOKBACK
    in_win = lambda d, w: ((d >= 0) & (d < w)).astype(np.float32)
    bands = np.stack([in_win(r - c, w) for w in POOL_WINDOWS])
    bands_lb = np.stack([in_win(r[:POOL_LOOKBACK] - c_lb, w) for w in POOL_WINDOWS])
    return jnp.asarray(bands, dtype=BF16), jnp.asarray(bands_lb, dtype=BF16)


def _layer(x, w_in_t, w_gate_up, b_gate, gn_w, pool_w, pool_b, pool_scale, w_a, w_b, w_o, ln_w, ln_b,
           *, alpha):
    bsz, seq, d_model = x.shape
    rank, dk = w_gate_up.shape
    dv = gn_w.shape[0]
    assert seq % SEQ_TILE == 0 and SEQ_TILE % GLA_FAST_CHUNK == 0 and GLA_FAST_CHUNK % GLA_CHUNK == 0
    assert dv == d_model and 2 * dk == d_model and rank == GLA_GATE_RANK
    assert pool_w.shape[0] == len(POOL_WINDOWS)
    d_in = w_in_t.shape[0]
    gdim = pool_w.shape[1]
    row = lambda a: a.reshape(1, -1)
    hbm_weights = (w_in_t, w_a, w_b, w_o, pool_w.reshape(d_model, gdim))
    small = (w_gate_up, row(b_gate), row(gn_w), row(pool_b), row(pool_scale), row(ln_w), row(ln_b),
             _tri_matrix(), *_band_matrices())
    operands = (x, *hbm_weights, *small)
    tiles_per_seq = seq // SEQ_TILE
    n_tiles = bsz * tiles_per_seq

    def tile_block(tile):
        return (tile // tiles_per_seq, tile % tiles_per_seq, 0)

    in_specs = [pl.BlockSpec((None, SEQ_TILE, d_model), lambda s: tile_block(jnp.minimum(s, n_tiles - 1)))]
    in_specs += [pl.BlockSpec(memory_space=pl.ANY) for _ in hbm_weights]
    in_specs += [_const_spec(a.shape) for a in small]
    kernel = functools.partial(_block_kernel, d_model=d_model, dk=dk, dv=dv, alpha=alpha,
                               tiles_per_seq=tiles_per_seq, n_tiles=n_tiles)
    state_shape = (GLA_HEADS, dk // GLA_HEADS, dv // GLA_HEADS)
    return pl.pallas_call(
        kernel,
        grid=(n_tiles + 1,),
        in_specs=in_specs,
        out_specs=pl.BlockSpec((None, SEQ_TILE, d_model), lambda s: tile_block(jnp.maximum(s - 1, 0))),
        out_shape=jax.ShapeDtypeStruct(x.shape, x.dtype),
        scratch_shapes=[
            pltpu.VMEM((SEQ_TILE, dk), F32),
            pltpu.VMEM((SEQ_TILE, dk), F32),
            pltpu.VMEM((SEQ_TILE, dk), F32),
            pltpu.VMEM((SEQ_TILE, dv), BF16),
            pltpu.VMEM((SEQ_TILE, dv), F32),
            pltpu.VMEM(state_shape, F32),
            pltpu.VMEM(state_shape, F32),
            pltpu.VMEM(state_shape, F32),
            pltpu.VMEM((POOL_LOOKBACK, d_model), BF16),
            pltpu.VMEM((SEQ_TILE, dv), F32),
            pltpu.VMEM((SEQ_TILE, d_model), F32),
            pltpu.VMEM((SEQ_TILE, d_model), F32),
            pltpu.VMEM((SEQ_TILE, d_model), F32),
            pltpu.VMEM((SEQ_TILE, dk), F32),
            pltpu.SMEM((1,), jnp.int32),
            pltpu.VMEM((d_model, GATE_PAD + d_in - rank), BF16),
            pltpu.VMEM(w_a.shape, BF16),
            pltpu.VMEM(w_b.shape, BF16),
            pltpu.VMEM(w_o.shape, BF16),
            pltpu.VMEM((d_model, gdim), BF16),
            pltpu.VMEM((WEIGHT_STAGE_SLOTS, WEIGHT_LOAD_ROWS, d_model), F32),
            pltpu.VMEM((rank, d_model), F32),
            pltpu.VMEM((d_model, gdim), F32),
            pltpu.SemaphoreType.DMA((WEIGHT_STAGE_SLOTS + 2,)),
        ],
        compiler_params=pltpu.CompilerParams(
            dimension_semantics=("arbitrary",),
            vmem_limit_bytes=VMEM_LIMIT_BYTES),
        name="hybrid_gla_pool_layer",
    )(*operands)


def kernel(x, w_in, w_gate_up, b_gate, gn_w, pool_w, pool_b, pool_scale, w_a, w_b, w_o, ln_w, ln_b):
    depth = w_in.shape[0]
    alpha = (2.0 * depth) ** 0.25
    w_in_t = jnp.swapaxes(w_in, 1, 2)
    for l in range(depth):
        x = _layer(x, w_in_t[l], w_gate_up[l], b_gate[l], gn_w[l], pool_w[l], pool_b[l], pool_scale[l],
                   w_a[l], w_b[l], w_o[l], ln_w[l], ln_b[l], alpha=alpha)
    return x
```

```python
import functools

import numpy as np
import jax
import jax.numpy as jnp
from jax import lax
from jax.experimental import pallas as pl
from jax.experimental.pallas import tpu as pltpu

GLA_HEADS = 4
GLA_GATE_RANK = 16
GLA_GATE_TAU = 16.0
POOL_WINDOWS = (2, 4, 8, 16)
POOL_LOOKBACK = 16
NORM_EPS = 1e-5

SEQ_TILE = 512
GLA_FAST_CHUNK = 256
GLA_CHUNK = 64
WEIGHT_LOAD_ROWS = 256
WEIGHT_STAGE_SLOTS = 6
VMEM_LIMIT_BYTES = 60 * 1024 * 1024
GLA_FACTOR_MAX_DECAY = 50.0

_QK, _V, _G, _U, _Z, _GATE_A, _GATE_B = range(7)
GATE_PAD = 128

F32 = jnp.float32
BF16 = jnp.bfloat16
F32_SUBLANES = 8


def _dot(a, b):
    return jnp.dot(a, b, preferred_element_type=F32)


def _sigmoid(x):
    return 1.0 / (1.0 + jnp.exp(-x))


def _log_sigmoid(x):
    return jnp.minimum(x, 0.0) - jnp.log(1.0 + jnp.exp(-jnp.abs(x)))


def _hi_lo(x):
    hi = x.astype(BF16)
    return hi, (x - hi.astype(F32)).astype(BF16)


def _row_to_col(row):
    return jnp.transpose(jnp.broadcast_to(row, (F32_SUBLANES, row.shape[1])))[:, 0:1]


def _gla_factor_operands(qs_ref, k_ref, b_ref, rows):
    b = b_ref[rows, :]
    e_pos = jnp.exp(b)
    e_last = e_pos[GLA_FAST_CHUNK - 1:GLA_FAST_CHUNK, :]
    k_neg = k_ref[rows, :] * jnp.exp(-b)
    qd = (qs_ref[rows, :] * e_pos).astype(BF16)
    kn = k_neg.astype(BF16)
    kd = k_neg * e_last
    return qd, kn, kd, e_last


def _gla_chunk_factorised(operands, v_ref, o_ref, rows, state_ref, new_state_ref, *, hdk, hdv, fillers):
    ts = GLA_FAST_CHUNK
    qd, kn, kd, e_last = operands
    heads = range(GLA_HEADS)
    ks = [slice(h * hdk, (h + 1) * hdk) for h in heads]
    vs = [slice(h * hdv, (h + 1) * hdv) for h in heads]
    causal = (lax.broadcasted_iota(jnp.int32, (ts, ts), 0)
              >= lax.broadcasted_iota(jnp.int32, (ts, ts), 1))
    masked = []
    for h in heads:
        scores = lax.dot_general(qd[:, ks[h]], kn[:, ks[h]], (((1,), (1,)), ((), ())),
                                 preferred_element_type=F32)
        fillers[h]()
        masked.append(jnp.where(causal, scores, 0.0).astype(BF16))
    for h in heads:
        kd_t = jnp.transpose(kd[:, ks[h]]).astype(BF16)
        new_state_ref[h] = state_ref[h] * _row_to_col(e_last[:, ks[h]]) + _dot(kd_t, v_ref[rows, vs[h]])
    for h in heads:
        o_ref[rows, vs[h]] = (_dot(qd[:, ks[h]], state_ref[h].astype(BF16))
                              + _dot(masked[h], v_ref[rows, vs[h]]))


def _gla_tile_direct(qs_ref, k_ref, b_ref, v_ref, o_ref, state_ref, new_state_ref, *, hdk, hdv):
    new_state_ref[...] = state_ref[...]
    row_c = lax.broadcasted_iota(jnp.int32, (GLA_CHUNK, 1), 0)
    lane_c = lax.broadcasted_iota(jnp.int32, (GLA_CHUNK, GLA_CHUNK), 1)

    @pl.loop(0, SEQ_TILE // GLA_CHUNK)
    def _(c):
        r0 = pl.multiple_of(c * GLA_CHUNK, GLA_CHUNK)
        rows = pl.ds(r0, GLA_CHUNK)
        qs_c = qs_ref[rows, :]
        b_c = b_ref[rows, :]

        def col_body(j, a_heads):
            kj = k_ref[pl.ds(r0 + j, 1), :]
            bj = b_ref[pl.ds(r0 + j, 1), :]
            p = qs_c * kj * jnp.exp(jnp.minimum(b_c - bj, 0.0))
            p = jnp.where(row_c >= j, p, 0.0)
            new = []
            for h in range(GLA_HEADS):
                s = jnp.sum(p[:, h * hdk:(h + 1) * hdk], axis=-1, keepdims=True)
                new.append(jnp.where(lane_c == j, s, a_heads[h]))
            return tuple(new)

        a_heads = lax.fori_loop(
            0, GLA_CHUNK, col_body,
            tuple(jnp.zeros((GLA_CHUNK, GLA_CHUNK), F32) for _ in range(GLA_HEADS)))

        b_last = b_ref[pl.ds(r0 + GLA_CHUNK - 1, 1), :]
        qd = (qs_c * jnp.exp(b_c)).astype(BF16)
        kd = k_ref[rows, :] * jnp.exp(b_last - b_c)
        e_last = jnp.exp(b_last)
        for h in range(GLA_HEADS):
            ks = slice(h * hdk, (h + 1) * hdk)
            vs = slice(h * hdv, (h + 1) * hdv)
            v_ch = v_ref[rows, vs]
            st = new_state_ref[h]
            o_ref[rows, vs] = (_dot(qd[:, ks], st.astype(BF16))
                               + _dot(a_heads[h].astype(BF16), v_ch))
            kd_t = jnp.transpose(kd[:, ks]).astype(BF16)
            new_state_ref[h] = st * _row_to_col(e_last[:, ks]) + _dot(kd_t, v_ch)


def _load_weights(wint_hbm, wa_hbm, wb_hbm, wo_hbm, poolw_hbm,
                  wbig_ref, wa_ref, wb_ref, wo_ref, poolw_ref,
                  stage_ref, gate_stage_ref, pool_stage_ref, sems, *, n_before_gate, rank):
    rows = WEIGHT_LOAD_ROWS
    d_in, d_model = wint_hbm.shape
    assert n_before_gate % rows == 0 and (d_in - rank) % rows == 0 and d_model % rows == 0
    jobs = []
    for j in range((d_in - rank) // rows):
        row0 = j * rows if j * rows < n_before_gate else j * rows + rank
        jobs.append((wint_hbm, row0, wbig_ref.at[:, GATE_PAD + j * rows:GATE_PAD + (j + 1) * rows]))
    for src, dst in ((wa_hbm, wa_ref), (wb_hbm, wb_ref), (wo_hbm, wo_ref)):
        for r0 in range(0, d_model, rows):
            jobs.append((src, r0, dst.at[r0:r0 + rows, :]))

    slots = stage_ref.shape[0]

    def staged_copy(i):
        src, row0, _ = jobs[i]
        return pltpu.make_async_copy(src.at[pl.ds(row0, rows), :], stage_ref.at[i % slots], sems.at[i % slots])

    gate_copy = pltpu.make_async_copy(wint_hbm.at[pl.ds(n_before_gate, rank), :], gate_stage_ref, sems.at[slots])
    pool_copy = pltpu.make_async_copy(poolw_hbm, pool_stage_ref, sems.at[slots + 1])
    gate_copy.start()
    pool_copy.start()
    ahead = slots - 1
    for i in range(min(ahead, len(jobs))):
        staged_copy(i).start()
    for i, (src, _, dst) in enumerate(jobs):
        if i + ahead < len(jobs):
            staged_copy(i + ahead).start()
        staged_copy(i).wait()
        block = stage_ref[i % slots]
        dst[...] = (jnp.transpose(block) if src is wint_hbm else block).astype(BF16)
    gate_copy.wait()
    wbig_ref[:, 0:GATE_PAD] = jnp.zeros((d_model, GATE_PAD), BF16)
    wbig_ref[:, 0:rank] = jnp.transpose(gate_stage_ref[...]).astype(BF16)
    pool_copy.wait()
    poolw_ref[...] = pool_stage_ref[...].astype(BF16)


def _block_kernel(x_ref, wint_hbm, wa_hbm, wb_hbm, wo_hbm, poolw_hbm,
                  wgu_ref, bg_ref, gnw_ref, poolb_ref, pools_ref, lnw_ref, lnb_ref,
                  tri_ref,
                  out_ref,
                  qs_ref, k_ref, b_ref, v_ref, o_ref, state_ref, mid_state_ref, new_state_ref, ulast_ref,
                  ya_scale_ref, gate_a_ref, merged_b_ref, resid_ref, la_ref, factorisable_ref,
                  wbig_ref, wa_ref, wb_ref, wo_ref, poolw_ref,
                  stage_ref, gate_stage_ref, pool_stage_ref, load_sems,
                  *, d_model, dk, dv, alpha, tiles_per_seq, n_tiles):
    ts = SEQ_TILE
    hdk = dk // GLA_HEADS
    hdv = dv // GLA_HEADS
    s = pl.program_id(0)
    t = lax.rem(jnp.minimum(s, n_tiles - 1), tiles_per_seq)
    fast_chunks = [slice(r0, r0 + GLA_FAST_CHUNK) for r0 in range(0, ts, GLA_FAST_CHUNK)]
    assert len(fast_chunks) == 2

    @pl.when(s == 0)
    def _():
        _load_weights(wint_hbm, wa_hbm, wb_hbm, wo_hbm, poolw_hbm,
                      wbig_ref, wa_ref, wb_ref, wo_ref, poolw_ref,
                      stage_ref, gate_stage_ref, pool_stage_ref, load_sems,
                      n_before_gate=2 * dk + 2 * dv, rank=GLA_GATE_RANK)
        for ref in (o_ref, ya_scale_ref, gate_a_ref, merged_b_ref, resid_ref):
            @pl.loop(0, ts // GLA_CHUNK)
            def _(i, ref=ref):
                ref[pl.ds(pl.multiple_of(i * GLA_CHUNK, GLA_CHUNK), GLA_CHUNK), :] = jnp.zeros(
                    (GLA_CHUNK, ref.shape[1]), ref.dtype)

    @pl.when(t == 0)
    def _():
        state_ref[...] = jnp.zeros_like(state_ref)
        ulast_ref[...] = jnp.zeros_like(ulast_ref)

    def step(do_head, do_tail):
        if do_head:
            xf = x_ref[...]
            xb = xf.astype(BF16)

            def proj(n):
                return _dot(xb, wbig_ref[:, GATE_PAD + n * d_model:GATE_PAD + (n + 1) * d_model])

        if do_tail:
            o = o_ref[...]
            y_a_parts = []
            for h in range(GLA_HEADS):
                oh = o[:, h * hdv:(h + 1) * hdv]
                y_a_parts.append(oh * lax.rsqrt(jnp.mean(oh * oh, axis=-1, keepdims=True) + NORM_EPS))
            y_a = (jnp.concatenate(y_a_parts, axis=-1) * ya_scale_ref[...]).astype(BF16)

        if do_head:
            gate_qk = _dot(xb, wbig_ref[:, 0:GATE_PAD + d_model])
            a_low = gate_qk[:, :GLA_GATE_RANK]
            qk = gate_qk[:, GATE_PAD:]
            qs_ref[...] = qk[:, :dk] * (hdk ** -0.5)
            k_ref[...] = qk[:, dk:]
            gate_pre = _dot(a_low.astype(BF16), wgu_ref[...].astype(BF16)) + bg_ref[...]
            log_a = _log_sigmoid(gate_pre) * (1.0 / GLA_GATE_TAU)
            la_ref[...] = log_a
            la_hi, la_lo = _hi_lo(log_a)

        if do_tail:
            merged = (gate_a_ref[...] * _dot(y_a, wa_ref[...]) + merged_b_ref[...]).astype(BF16)

        if do_head:
            for rows in fast_chunks:
                b_ref[rows, :] = _dot(tri_ref[...], la_hi[rows, :]) + _dot(tri_ref[...], la_lo[rows, :])
            min_b = functools.reduce(
                jnp.minimum, [jnp.min(b_ref[rows.stop - 1:rows.stop, :]) for rows in fast_chunks])
            factorisable_ref[0] = (min_b >= -GLA_FACTOR_MAX_DECAY).astype(jnp.int32)
            v_ref[...] = proj(_V).astype(BF16)
            gla_operands = [_gla_factor_operands(qs_ref, k_ref, b_ref, rows) for rows in fast_chunks]

        if do_tail:
            out_ref[...] = resid_ref[...] + _dot(merged, wo_ref[...])

        if do_head:
            u = proj(_U)
            t_abs = t * ts + lax.broadcasted_iota(jnp.int32, (ts, 1), 0)
            gdim = d_model // len(POOL_WINDOWS)
            gslices = [slice(gi * gdim, (gi + 1) * gdim) for gi in range(len(POOL_WINDOWS))]
            pooled_in = []
            for gi, gs in enumerate(gslices):
                win = jnp.concatenate([ulast_ref[:, gs], u[:, gs]], axis=0)
                span = 1
                while span < POOL_WINDOWS[gi]:
                    win = win + pltpu.roll(win, span, axis=0)
                    span *= 2
                count = jnp.minimum(t_abs + 1, POOL_WINDOWS[gi]).astype(F32)
                pooled_in.append((win[POOL_LOOKBACK:] / count - u[:, gs]).astype(BF16))
            ulast_ref[...] = u[ts - POOL_LOOKBACK:, :]

        if do_head:
            def gate_factor_columns(h):
                def issue():
                    cols = slice(h * hdv, (h + 1) * hdv)
                    g0 = GATE_PAD + _G * d_model + h * hdv
                    g = _dot(xb, wbig_ref[:, g0:g0 + hdv])
                    ya_scale_ref[:, cols] = gnw_ref[:, cols] * (g * _sigmoid(g))
                return issue

            def merge_gate_columns(h):
                def issue():
                    c0 = GATE_PAD + _GATE_A * d_model + h * hdv
                    cols = slice(c0, c0 + hdv)
                    gate_a_ref[:, h * hdv:(h + 1) * hdv] = _sigmoid(_dot(xb, wbig_ref[:, cols]))
                return issue

            chunk_states = (state_ref, mid_state_ref, new_state_ref)
            chunk_fillers = (gate_factor_columns, merge_gate_columns)
            for c, rows in enumerate(fast_chunks):
                _gla_chunk_factorised(gla_operands[c], v_ref, o_ref, rows, chunk_states[c], chunk_states[c + 1],
                                      hdk=hdk, hdv=hdv,
                                      fillers=[chunk_fillers[c](h) for h in range(GLA_HEADS)])

        if do_head:
            z = proj(_Z)
            silu_z = z * _sigmoid(z)

        if do_tail:
            r = out_ref[...]
            mu = jnp.mean(r, axis=-1, keepdims=True)
            rc = r - mu
            var = jnp.mean(rc * rc, axis=-1, keepdims=True)
            out_ref[...] = (rc * lax.rsqrt(var + NORM_EPS) * lnw_ref[...] + lnb_ref[...]).astype(out_ref.dtype)

        if do_head:
            p_parts =[_dot(pooled_in[gi], poolw_ref[gs, :]) for gi, gs in enumerate(gslices)]
            gate_b = _sigmoid(proj(_GATE_B))
            y_b = (jnp.concatenate(p_parts, axis=-1) + poolb_ref[...]) * pools_ref[...] * silu_z
            resid_ref[...] = alpha * xf
            merged_b_ref[...] = gate_b * _dot(y_b.astype(BF16), wb_ref[...])

    pl.when(s < n_tiles)(functools.partial(step, True, True))
    pl.when(s == n_tiles)(functools.partial(step, False, True))

    @pl.when(s < n_tiles)
    def _():
        @pl.when(factorisable_ref[0] == 0)
        def _():
            span = GLA_FAST_CHUNK
            row = lax.broadcasted_iota(jnp.int32, (span, span), 0)
            col = lax.broadcasted_iota(jnp.int32, (span, span), 1)
            tri_chunk = jnp.where(row // GLA_CHUNK == col // GLA_CHUNK, tri_ref[...], jnp.zeros((), BF16))
            la_hi, la_lo = _hi_lo(la_ref[...])
            for rows in fast_chunks:
                b_ref[rows, :] = _dot(tri_chunk, la_hi[rows, :]) + _dot(tri_chunk, la_lo[rows, :])
            _gla_tile_direct(qs_ref, k_ref, b_ref, v_ref, o_ref, state_ref, new_state_ref, hdk=hdk, hdv=hdv)

        state_ref[...] = new_state_ref[...]


def _const_spec(shape):
    nd = len(shape)
    return pl.BlockSpec(shape, lambda s: (0,) * nd, pipeline_mode=pl.Buffered(1))


def _tri_matrix():
    r = np.arange(GLA_FAST_CHUNK)
    return jnp.asarray((r[None, :] <= r[:, None]).astype(np.float32), dtype=BF16)


def _layer(x, w_in_t, w_gate_up, b_gate, gn_w, pool_w, pool_b, pool_scale, w_a, w_b, w_o, ln_w, ln_b,
           *, alpha):
    bsz, seq, d_model = x.shape
    rank, dk = w_gate_up.shape
    dv = gn_w.shape[0]
    assert seq % SEQ_TILE == 0 and SEQ_TILE % GLA_FAST_CHUNK == 0 and GLA_FAST_CHUNK % GLA_CHUNK == 0
    assert dv == d_model and 2 * dk == d_model and rank == GLA_GATE_RANK
    assert pool_w.shape[0] == len(POOL_WINDOWS)
    assert all(w & (w - 1) == 0 and w <= POOL_LOOKBACK for w in POOL_WINDOWS)
    d_in = w_in_t.shape[0]
    gdim = pool_w.shape[1]
    row = lambda a: a.reshape(1, -1)
    hbm_weights = (w_in_t, w_a, w_b, w_o, pool_w.reshape(d_model, gdim))
    small = (w_gate_up, row(b_gate), row(gn_w), row(pool_b), row(pool_scale), row(ln_w), row(ln_b),
             _tri_matrix())
    operands = (x, *hbm_weights, *small)
    tiles_per_seq = seq // SEQ_TILE
    n_tiles = bsz * tiles_per_seq

    def tile_block(tile):
        return (tile // tiles_per_seq, tile % tiles_per_seq, 0)

    in_specs = [pl.BlockSpec((None, SEQ_TILE, d_model), lambda s: tile_block(jnp.minimum(s, n_tiles - 1)))]
    in_specs += [pl.BlockSpec(memory_space=pl.ANY) for _ in hbm_weights]
    in_specs += [_const_spec(a.shape) for a in small]
    kernel = functools.partial(_block_kernel, d_model=d_model, dk=dk, dv=dv, alpha=alpha,
                               tiles_per_seq=tiles_per_seq, n_tiles=n_tiles)
    state_shape = (GLA_HEADS, dk // GLA_HEADS, dv // GLA_HEADS)
    return pl.pallas_call(
        kernel,
        grid=(n_tiles + 1,),
        in_specs=in_specs,
        out_specs=pl.BlockSpec((None, SEQ_TILE, d_model), lambda s: tile_block(jnp.maximum(s - 1, 0))),
        out_shape=jax.ShapeDtypeStruct(x.shape, x.dtype),
        scratch_shapes=[
            pltpu.VMEM((SEQ_TILE, dk), F32),
            pltpu.VMEM((SEQ_TILE, dk), F32),
            pltpu.VMEM((SEQ_TILE, dk), F32),
            pltpu.VMEM((SEQ_TILE, dv), BF16),
            pltpu.VMEM((SEQ_TILE, dv), F32),
            pltpu.VMEM(state_shape, F32),
            pltpu.VMEM(state_shape, F32),
            pltpu.VMEM(state_shape, F32),
            pltpu.VMEM((POOL_LOOKBACK, d_model), F32),
            pltpu.VMEM((SEQ_TILE, dv), F32),
            pltpu.VMEM((SEQ_TILE, d_model), F32),
            pltpu.VMEM((SEQ_TILE, d_model), F32),
            pltpu.VMEM((SEQ_TILE, d_model), F32),
            pltpu.VMEM((SEQ_TILE, dk), F32),
            pltpu.SMEM((1,), jnp.int32),
            pltpu.VMEM((d_model, GATE_PAD + d_in - rank), BF16),
            pltpu.VMEM(w_a.shape, BF16),
            pltpu.VMEM(w_b.shape, BF16),
            pltpu.VMEM(w_o.shape, BF16),
            pltpu.VMEM((d_model, gdim), BF16),
            pltpu.VMEM((WEIGHT_STAGE_SLOTS, WEIGHT_LOAD_ROWS, d_model), F32),
            pltpu.VMEM((rank, d_model), F32),
            pltpu.VMEM((d_model, gdim), F32),
            pltpu.SemaphoreType.DMA((WEIGHT_STAGE_SLOTS + 2,)),
        ],
        compiler_params=pltpu.CompilerParams(
            dimension_semantics=("arbitrary",),
            vmem_limit_bytes=VMEM_LIMIT_BYTES),
        name="hybrid_gla_pool_layer",
    )(*operands)


def kernel(x, w_in, w_gate_up, b_gate, gn_w, pool_w, pool_b, pool_scale, w_a, w_b, w_o, ln_w, ln_b):
    depth = w_in.shape[0]
    alpha = (2.0 * depth) ** 0.25
    w_in_t = jnp.swapaxes(w_in, 1, 2)
    for l in range(depth):
        x = _layer(x, w_in_t[l], w_gate_up[l], b_gate[l], gn_w[l], pool_w[l], pool_b[l], pool_scale[l],
                   w_a[l], w_b[l], w_o[l], ln_w[l], ln_b[l], alpha=alpha)
    return x
```

```python
import functools

import numpy as np
import jax
import jax.numpy as jnp
from jax import lax
from jax.experimental import pallas as pl
from jax.experimental.pallas import tpu as pltpu

GLA_HEADS = 4
GLA_GATE_RANK = 16
GLA_GATE_TAU = 16.0
POOL_WINDOWS = (2, 4, 8, 16)
POOL_LOOKBACK = 16
NORM_EPS = 1e-5

SEQ_TILE = 512
GLA_FAST_CHUNK = 256
GLA_CHUNK = 64
WEIGHT_LOAD_ROWS = 256
WEIGHT_STAGE_SLOTS = 6
VMEM_LIMIT_BYTES = 60 * 1024 * 1024
GLA_FACTOR_MAX_DECAY = 50.0

_QK, _V, _G, _U, _Z, _GATE_A, _GATE_B = range(7)
GATE_PAD = 128

F32 = jnp.float32
BF16 = jnp.bfloat16
F32_SUBLANES = 8


def _dot(a, b):
    return jnp.dot(a, b, preferred_element_type=F32)


def _sigmoid(x):
    return 1.0 / (1.0 + jnp.exp(-x))


def _log_sigmoid(x):
    return jnp.minimum(x, 0.0) - jnp.log(1.0 + jnp.exp(-jnp.abs(x)))


def _hi_lo(x):
    hi = x.astype(BF16)
    return hi, (x - hi.astype(F32)).astype(BF16)


def _cumsum_rows(x):
    n = x.shape[0]
    row = lax.broadcasted_iota(jnp.int32, (n, 1), 0)
    span = 1
    while span < n:
        x = x + jnp.where(row >= span, pltpu.roll(x, span, axis=0), 0.0)
        span *= 2
    return x


def _row_to_col(row):
    return jnp.transpose(jnp.broadcast_to(row, (F32_SUBLANES, row.shape[1])))[:, 0:1]


def _gla_factor_operands(qs_ref, k_ref, b_ref, rows):
    b = b_ref[rows, :]
    e_pos = jnp.exp(b)
    e_last = e_pos[GLA_FAST_CHUNK - 1:GLA_FAST_CHUNK, :]
    k_neg = k_ref[rows, :] * jnp.exp(-b)
    qd = (qs_ref[rows, :] * e_pos).astype(BF16)
    kn = k_neg.astype(BF16)
    kd = k_neg * e_last
    return qd, kn, kd, e_last


def _gla_chunk_factorised(operands, v_ref, o_ref, rows, state_ref, new_state_ref, *, hdk, hdv, fillers):
    ts = GLA_FAST_CHUNK
    qd, kn, kd, e_last = operands
    heads = range(GLA_HEADS)
    ks = [slice(h * hdk, (h + 1) * hdk) for h in heads]
    vs = [slice(h * hdv, (h + 1) * hdv) for h in heads]
    causal = (lax.broadcasted_iota(jnp.int32, (ts, ts), 0)
              >= lax.broadcasted_iota(jnp.int32, (ts, ts), 1))
    masked = []
    for h in heads:
        scores = lax.dot_general(qd[:, ks[h]], kn[:, ks[h]], (((1,), (1,)), ((), ())),
                                 preferred_element_type=F32)
        fillers[h]()
        masked.append(jnp.where(causal, scores, 0.0).astype(BF16))
    for h in heads:
        kd_t = jnp.transpose(kd[:, ks[h]]).astype(BF16)
        new_state_ref[h] = state_ref[h] * _row_to_col(e_last[:, ks[h]]) + _dot(kd_t, v_ref[rows, vs[h]])
    for h in heads:
        o_ref[rows, vs[h]] = (_dot(qd[:, ks[h]], state_ref[h].astype(BF16))
                              + _dot(masked[h], v_ref[rows, vs[h]]))


def _gla_tile_direct(qs_ref, k_ref, b_ref, v_ref, o_ref, state_ref, new_state_ref, *, hdk, hdv):
    new_state_ref[...] = state_ref[...]
    row_c = lax.broadcasted_iota(jnp.int32, (GLA_CHUNK, 1), 0)
    lane_c = lax.broadcasted_iota(jnp.int32, (GLA_CHUNK, GLA_CHUNK), 1)

    @pl.loop(0, SEQ_TILE // GLA_CHUNK)
    def _(c):
        r0 = pl.multiple_of(c * GLA_CHUNK, GLA_CHUNK)
        rows = pl.ds(r0, GLA_CHUNK)
        qs_c = qs_ref[rows, :]
        b_c = b_ref[rows, :]

        def col_body(j, a_heads):
            kj = k_ref[pl.ds(r0 + j, 1), :]
            bj = b_ref[pl.ds(r0 + j, 1), :]
            p = qs_c * kj * jnp.exp(jnp.minimum(b_c - bj, 0.0))
            p = jnp.where(row_c >= j, p, 0.0)
            new = []
            for h in range(GLA_HEADS):
                s = jnp.sum(p[:, h * hdk:(h + 1) * hdk], axis=-1, keepdims=True)
                new.append(jnp.where(lane_c == j, s, a_heads[h]))
            return tuple(new)

        a_heads = lax.fori_loop(
            0, GLA_CHUNK, col_body,
            tuple(jnp.zeros((GLA_CHUNK, GLA_CHUNK), F32) for _ in range(GLA_HEADS)))

        b_last = b_ref[pl.ds(r0 + GLA_CHUNK - 1, 1), :]
        qd = (qs_c * jnp.exp(b_c)).astype(BF16)
        kd = k_ref[rows, :] * jnp.exp(b_last - b_c)
        e_last = jnp.exp(b_last)
        for h in range(GLA_HEADS):
            ks = slice(h * hdk, (h + 1) * hdk)
            vs = slice(h * hdv, (h + 1) * hdv)
            v_ch = v_ref[rows, vs]
            st = new_state_ref[h]
            o_ref[rows, vs] = (_dot(qd[:, ks], st.astype(BF16))
                               + _dot(a_heads[h].astype(BF16), v_ch))
            kd_t = jnp.transpose(kd[:, ks]).astype(BF16)
            new_state_ref[h] = st * _row_to_col(e_last[:, ks]) + _dot(kd_t, v_ch)


def _load_weights(wint_hbm, wa_hbm, wb_hbm, wo_hbm, poolw_hbm,
                  wbig_ref, wa_ref, wb_ref, wo_ref, poolw_ref,
                  stage_ref, gate_stage_ref, pool_stage_ref, sems, *, n_before_gate, rank):
    rows = WEIGHT_LOAD_ROWS
    d_in, d_model = wint_hbm.shape
    assert n_before_gate % rows == 0 and (d_in - rank) % rows == 0 and d_model % rows == 0
    jobs = []
    for j in range((d_in - rank) // rows):
        row0 = j * rows if j * rows < n_before_gate else j * rows + rank
        jobs.append((wint_hbm, row0, wbig_ref.at[:, GATE_PAD + j * rows:GATE_PAD + (j + 1) * rows]))
    for src, dst in ((wa_hbm, wa_ref), (wb_hbm, wb_ref), (wo_hbm, wo_ref)):
        for r0 in range(0, d_model, rows):
            jobs.append((src, r0, dst.at[r0:r0 + rows, :]))

    slots = stage_ref.shape[0]

    def staged_copy(i):
        src, row0, _ = jobs[i]
        return pltpu.make_async_copy(src.at[pl.ds(row0, rows), :], stage_ref.at[i % slots], sems.at[i % slots])

    gate_copy = pltpu.make_async_copy(wint_hbm.at[pl.ds(n_before_gate, rank), :], gate_stage_ref, sems.at[slots])
    pool_copy = pltpu.make_async_copy(poolw_hbm, pool_stage_ref, sems.at[slots + 1])
    gate_copy.start()
    pool_copy.start()
    ahead = slots - 1
    for i in range(min(ahead, len(jobs))):
        staged_copy(i).start()
    for i, (src, _, dst) in enumerate(jobs):
        if i + ahead < len(jobs):
            staged_copy(i + ahead).start()
        staged_copy(i).wait()
        block = stage_ref[i % slots]
        dst[...] = (jnp.transpose(block) if src is wint_hbm else block).astype(BF16)
    gate_copy.wait()
    wbig_ref[:, 0:GATE_PAD] = jnp.zeros((d_model, GATE_PAD), BF16)
    wbig_ref[:, 0:rank] = jnp.transpose(gate_stage_ref[...]).astype(BF16)
    pool_copy.wait()
    poolw_ref[...] = pool_stage_ref[...].astype(BF16)


def _block_kernel(x_ref, wint_hbm, wa_hbm, wb_hbm, wo_hbm, poolw_hbm,
                  wgu_ref, bg_ref, gnw_ref, poolb_ref, pools_ref, lnw_ref, lnb_ref,
                  tri_ref,
                  out_ref,
                  qs_ref, k_ref, b_ref, v_ref, o_ref, state_ref, mid_state_ref, new_state_ref, ulast_ref,
                  ya_scale_ref, gate_a_ref, merged_b_ref, resid_ref, la_ref, factorisable_ref,
                  wbig_ref, wa_ref, wb_ref, wo_ref, poolw_ref,
                  stage_ref, gate_stage_ref, pool_stage_ref, load_sems,
                  *, d_model, dk, dv, alpha, tiles_per_seq, n_tiles):
    ts = SEQ_TILE
    hdk = dk // GLA_HEADS
    hdv = dv // GLA_HEADS
    s = pl.program_id(0)
    t = lax.rem(jnp.minimum(s, n_tiles - 1), tiles_per_seq)
    fast_chunks = [slice(r0, r0 + GLA_FAST_CHUNK) for r0 in range(0, ts, GLA_FAST_CHUNK)]
    assert len(fast_chunks) == 2

    @pl.when(s == 0)
    def _():
        _load_weights(wint_hbm, wa_hbm, wb_hbm, wo_hbm, poolw_hbm,
                      wbig_ref, wa_ref, wb_ref, wo_ref, poolw_ref,
                      stage_ref, gate_stage_ref, pool_stage_ref, load_sems,
                      n_before_gate=2 * dk + 2 * dv, rank=GLA_GATE_RANK)
        for ref in (o_ref, ya_scale_ref, gate_a_ref, merged_b_ref, resid_ref):
            @pl.loop(0, ts // GLA_CHUNK)
            def _(i, ref=ref):
                ref[pl.ds(pl.multiple_of(i * GLA_CHUNK, GLA_CHUNK), GLA_CHUNK), :] = jnp.zeros(
                    (GLA_CHUNK, ref.shape[1]), ref.dtype)

    @pl.when(t == 0)
    def _():
        state_ref[...] = jnp.zeros_like(state_ref)
        ulast_ref[...] = jnp.zeros_like(ulast_ref)

    def step(do_head, do_tail):
        if do_head:
            xf = x_ref[...]
            xb = xf.astype(BF16)

            def proj(n):
                return _dot(xb, wbig_ref[:, GATE_PAD + n * d_model:GATE_PAD + (n + 1) * d_model])

        if do_tail:
            o = o_ref[...]
            y_a_parts = []
            for h in range(GLA_HEADS):
                oh = o[:, h * hdv:(h + 1) * hdv]
                y_a_parts.append(oh * lax.rsqrt(jnp.mean(oh * oh, axis=-1, keepdims=True) + NORM_EPS))
            y_a = (jnp.concatenate(y_a_parts, axis=-1) * ya_scale_ref[...]).astype(BF16)

        if do_head:
            gate_qk = _dot(xb, wbig_ref[:, 0:GATE_PAD + d_model])
            a_low = gate_qk[:, :GLA_GATE_RANK]
            qk = gate_qk[:, GATE_PAD:]
            qs_ref[...] = qk[:, :dk] * (hdk ** -0.5)
            k_ref[...] = qk[:, dk:]
            gate_pre = _dot(a_low.astype(BF16), wgu_ref[...].astype(BF16)) + bg_ref[...]
            log_a = _log_sigmoid(gate_pre) * (1.0 / GLA_GATE_TAU)
            la_ref[...] = log_a

        if do_tail:
            merged = (gate_a_ref[...] * _dot(y_a, wa_ref[...]) + merged_b_ref[...]).astype(BF16)

        if do_head:
            for rows in fast_chunks:
                b_ref[rows, :] = _cumsum_rows(log_a[rows, :])
            min_b = functools.reduce(
                jnp.minimum, [jnp.min(b_ref[rows.stop - 1:rows.stop, :]) for rows in fast_chunks])
            factorisable_ref[0] = (min_b >= -GLA_FACTOR_MAX_DECAY).astype(jnp.int32)
            v_ref[...] = proj(_V).astype(BF16)
            gla_operands = [_gla_factor_operands(qs_ref, k_ref, b_ref, rows) for rows in fast_chunks]

        if do_tail:
            out_ref[...] = resid_ref[...] + _dot(merged, wo_ref[...])

        if do_head:
            u = proj(_U)
            t_abs = t * ts + lax.broadcasted_iota(jnp.int32, (ts, 1), 0)
            gdim = d_model // len(POOL_WINDOWS)
            gslices = [slice(gi * gdim, (gi + 1) * gdim) for gi in range(len(POOL_WINDOWS))]
            pooled_in = []
            for gi, gs in enumerate(gslices):
                win = jnp.concatenate([ulast_ref[:, gs], u[:, gs]], axis=0)
                span = 1
                while span < POOL_WINDOWS[gi]:
                    win = win + pltpu.roll(win, span, axis=0)
                    span *= 2
                count = jnp.minimum(t_abs + 1, POOL_WINDOWS[gi]).astype(F32)
                pooled_in.append((win[POOL_LOOKBACK:] / count - u[:, gs]).astype(BF16))
            ulast_ref[...] = u[ts - POOL_LOOKBACK:, :]

        if do_head:
            def gate_factor_columns(h):
                def issue():
                    cols = slice(h * hdv, (h + 1) * hdv)
                    g0 = GATE_PAD + _G * d_model + h * hdv
                    g = _dot(xb, wbig_ref[:, g0:g0 + hdv])
                    ya_scale_ref[:, cols] = gnw_ref[:, cols] * (g * _sigmoid(g))
                return issue

            def merge_gate_columns(h):
                def issue():
                    c0 = GATE_PAD + _GATE_A * d_model + h * hdv
                    cols = slice(c0, c0 + hdv)
                    gate_a_ref[:, h * hdv:(h + 1) * hdv] = _sigmoid(_dot(xb, wbig_ref[:, cols]))
                return issue

            chunk_states = (state_ref, mid_state_ref, new_state_ref)
            chunk_fillers = (gate_factor_columns, merge_gate_columns)
            for c, rows in enumerate(fast_chunks):
                _gla_chunk_factorised(gla_operands[c], v_ref, o_ref, rows, chunk_states[c], chunk_states[c + 1],
                                      hdk=hdk, hdv=hdv,
                                      fillers=[chunk_fillers[c](h) for h in range(GLA_HEADS)])

        if do_head:
            z = proj(_Z)
            silu_z = z * _sigmoid(z)

        if do_tail:
            r = out_ref[...]
            mu = jnp.mean(r, axis=-1, keepdims=True)
            rc = r - mu
            var = jnp.mean(rc * rc, axis=-1, keepdims=True)
            out_ref[...] = (rc * lax.rsqrt(var + NORM_EPS) * lnw_ref[...] + lnb_ref[...]).astype(out_ref.dtype)

        if do_head:
            p_parts =[_dot(pooled_in[gi], poolw_ref[gs, :]) for gi, gs in enumerate(gslices)]
            gate_b = _sigmoid(proj(_GATE_B))
            y_b = (jnp.concatenate(p_parts, axis=-1) + poolb_ref[...]) * pools_ref[...] * silu_z
            resid_ref[...] = alpha * xf
            merged_b_ref[...] = gate_b * _dot(y_b.astype(BF16), wb_ref[...])

    pl.when(s < n_tiles)(functools.partial(step, True, True))
    pl.when(s == n_tiles)(functools.partial(step, False, True))

    @pl.when(s < n_tiles)
    def _():
        @pl.when(factorisable_ref[0] == 0)
        def _():
            span = GLA_FAST_CHUNK
            row = lax.broadcasted_iota(jnp.int32, (span, span), 0)
            col = lax.broadcasted_iota(jnp.int32, (span, span), 1)
            tri_chunk = jnp.where(row // GLA_CHUNK == col // GLA_CHUNK, tri_ref[...], jnp.zeros((), BF16))
            la_hi, la_lo = _hi_lo(la_ref[...])
            for rows in fast_chunks:
                b_ref[rows, :] = _dot(tri_chunk, la_hi[rows, :]) + _dot(tri_chunk, la_lo[rows, :])
            _gla_tile_direct(qs_ref, k_ref, b_ref, v_ref, o_ref, state_ref, new_state_ref, hdk=hdk, hdv=hdv)

        state_ref[...] = new_state_ref[...]


def _const_spec(shape):
    nd = len(shape)
    return pl.BlockSpec(shape, lambda s: (0,) * nd, pipeline_mode=pl.Buffered(1))


def _tri_matrix():
    r = np.arange(GLA_FAST_CHUNK)
    return jnp.asarray((r[None, :] <= r[:, None]).astype(np.float32), dtype=BF16)


def _layer(x, w_in_t, w_gate_up, b_gate, gn_w, pool_w, pool_b, pool_scale, w_a, w_b, w_o, ln_w, ln_b,
           *, alpha):
    bsz, seq, d_model = x.shape
    rank, dk = w_gate_up.shape
    dv = gn_w.shape[0]
    assert seq % SEQ_TILE == 0 and SEQ_TILE % GLA_FAST_CHUNK == 0 and GLA_FAST_CHUNK % GLA_CHUNK == 0
    assert dv == d_model and 2 * dk == d_model and rank == GLA_GATE_RANK
    assert pool_w.shape[0] == len(POOL_WINDOWS)
    assert all(w & (w - 1) == 0 and w <= POOL_LOOKBACK for w in POOL_WINDOWS)
    d_in = w_in_t.shape[0]
    gdim = pool_w.shape[1]
    row = lambda a: a.reshape(1, -1)
    hbm_weights = (w_in_t, w_a, w_b, w_o, pool_w.reshape(d_model, gdim))
    small = (w_gate_up, row(b_gate), row(gn_w), row(pool_b), row(pool_scale), row(ln_w), row(ln_b),
             _tri_matrix())
    operands = (x, *hbm_weights, *small)
    tiles_per_seq = seq // SEQ_TILE
    n_tiles = bsz * tiles_per_seq

    def tile_block(tile):
        return (tile // tiles_per_seq, tile % tiles_per_seq, 0)

    in_specs = [pl.BlockSpec((None, SEQ_TILE, d_model), lambda s: tile_block(jnp.minimum(s, n_tiles - 1)))]
    in_specs += [pl.BlockSpec(memory_space=pl.ANY) for _ in hbm_weights]
    in_specs += [_const_spec(a.shape) for a in small]
    kernel = functools.partial(_block_kernel, d_model=d_model, dk=dk, dv=dv, alpha=alpha,
                               tiles_per_seq=tiles_per_seq, n_tiles=n_tiles)
    state_shape = (GLA_HEADS, dk // GLA_HEADS, dv // GLA_HEADS)
    return pl.pallas_call(
        kernel,
        grid=(n_tiles + 1,),
        in_specs=in_specs,
        out_specs=pl.BlockSpec((None, SEQ_TILE, d_model), lambda s: tile_block(jnp.maximum(s - 1, 0))),
        out_shape=jax.ShapeDtypeStruct(x.shape, x.dtype),
        scratch_shapes=[
            pltpu.VMEM((SEQ_TILE, dk), F32),
            pltpu.VMEM((SEQ_TILE, dk), F32),
            pltpu.VMEM((SEQ_TILE, dk), F32),
            pltpu.VMEM((SEQ_TILE, dv), BF16),
            pltpu.VMEM((SEQ_TILE, dv), F32),
            pltpu.VMEM(state_shape, F32),
            pltpu.VMEM(state_shape, F32),
            pltpu.VMEM(state_shape, F32),
            pltpu.VMEM((POOL_LOOKBACK, d_model), F32),
            pltpu.VMEM((SEQ_TILE, dv), F32),
            pltpu.VMEM((SEQ_TILE, d_model), F32),
            pltpu.VMEM((SEQ_TILE, d_model), F32),
            pltpu.VMEM((SEQ_TILE, d_model), F32),
            pltpu.VMEM((SEQ_TILE, dk), F32),
            pltpu.SMEM((1,), jnp.int32),
            pltpu.VMEM((d_model, GATE_PAD + d_in - rank), BF16),
            pltpu.VMEM(w_a.shape, BF16),
            pltpu.VMEM(w_b.shape, BF16),
            pltpu.VMEM(w_o.shape, BF16),
            pltpu.VMEM((d_model, gdim), BF16),
            pltpu.VMEM((WEIGHT_STAGE_SLOTS, WEIGHT_LOAD_ROWS, d_model), F32),
            pltpu.VMEM((rank, d_model), F32),
            pltpu.VMEM((d_model, gdim), F32),
            pltpu.SemaphoreType.DMA((WEIGHT_STAGE_SLOTS + 2,)),
        ],
        compiler_params=pltpu.CompilerParams(
            dimension_semantics=("arbitrary",),
            vmem_limit_bytes=VMEM_LIMIT_BYTES),
        name="hybrid_gla_pool_layer",
    )(*operands)


def kernel(x, w_in, w_gate_up, b_gate, gn_w, pool_w, pool_b, pool_scale, w_a, w_b, w_o, ln_w, ln_b):
    depth = w_in.shape[0]
    alpha = (2.0 * depth) ** 0.25
    w_in_t = jnp.swapaxes(w_in, 1, 2)
    for l in range(depth):
        x = _layer(x, w_in_t[l], w_gate_up[l], b_gate[l], gn_w[l], pool_w[l], pool_b[l], pool_scale[l],
                   w_a[l], w_b[l], w_o[l], ln_w[l], ln_b[l], alpha=alpha)
    return x
```

```python
import functools

import numpy as np
import jax
import jax.numpy as jnp
from jax import lax
from jax.experimental import pallas as pl
from jax.experimental.pallas import tpu as pltpu

GLA_HEADS = 4
GLA_GATE_RANK = 16
GLA_GATE_TAU = 16.0
POOL_WINDOWS = (2, 4, 8, 16)
POOL_LOOKBACK = 16
NORM_EPS = 1e-5

SEQ_TILE = 512
GLA_FAST_CHUNK = 256
GLA_CHUNK = 64
WEIGHT_LOAD_ROWS = 256
WEIGHT_STAGE_SLOTS = 6
VMEM_LIMIT_BYTES = 60 * 1024 * 1024
GLA_FACTOR_MAX_DECAY = 50.0

_QK, _V, _G, _U, _Z, _GATE_A, _GATE_B = range(7)
GATE_PAD = 128

F32 = jnp.float32
BF16 = jnp.bfloat16
F32_SUBLANES = 8


def _dot(a, b):
    return jnp.dot(a, b, preferred_element_type=F32)


def _sigmoid(x):
    return 1.0 / (1.0 + jnp.exp(-x))


def _log_sigmoid(x):
    return jnp.minimum(x, 0.0) - jnp.log(1.0 + jnp.exp(-jnp.abs(x)))


def _hi_lo(x):
    hi = x.astype(BF16)
    return hi, (x - hi.astype(F32)).astype(BF16)


def _row_to_col(row):
    return jnp.transpose(jnp.broadcast_to(row, (F32_SUBLANES, row.shape[1])))[:, 0:1]


def _gla_factor_operands(qs_ref, k_ref, b_ref, rows):
    b = b_ref[rows, :]
    e_pos = jnp.exp(b)
    e_last = e_pos[GLA_FAST_CHUNK - 1:GLA_FAST_CHUNK, :]
    k_neg = k_ref[rows, :] * jnp.exp(-b)
    qd = (qs_ref[rows, :] * e_pos).astype(BF16)
    kn = k_neg.astype(BF16)
    kd = k_neg * e_last
    return qd, kn, kd, e_last


def _gla_chunk_factorised(operands, v_ref, o_ref, rows, state_ref, new_state_ref, *, hdk, hdv, fillers):
    ts = GLA_FAST_CHUNK
    qd, kn, kd, e_last = operands
    heads = range(GLA_HEADS)
    ks = [slice(h * hdk, (h + 1) * hdk) for h in heads]
    vs = [slice(h * hdv, (h + 1) * hdv) for h in heads]
    causal = (lax.broadcasted_iota(jnp.int32, (ts, ts), 0)
              >= lax.broadcasted_iota(jnp.int32, (ts, ts), 1))
    masked = []
    for h in heads:
        scores = lax.dot_general(qd[:, ks[h]], kn[:, ks[h]], (((1,), (1,)), ((), ())),
                                 preferred_element_type=F32)
        fillers[h]()
        masked.append(jnp.where(causal, scores, 0.0).astype(BF16))
    for h in heads:
        kd_t = jnp.transpose(kd[:, ks[h]]).astype(BF16)
        new_state_ref[h] = state_ref[h] * _row_to_col(e_last[:, ks[h]]) + _dot(kd_t, v_ref[rows, vs[h]])
    for h in heads:
        o_ref[rows, vs[h]] = (_dot(qd[:, ks[h]], state_ref[h].astype(BF16))
                              + _dot(masked[h], v_ref[rows, vs[h]]))


def _gla_tile_direct(qs_ref, k_ref, b_ref, v_ref, o_ref, state_ref, new_state_ref, *, hdk, hdv):
    new_state_ref[...] = state_ref[...]
    row_c = lax.broadcasted_iota(jnp.int32, (GLA_CHUNK, 1), 0)
    lane_c = lax.broadcasted_iota(jnp.int32, (GLA_CHUNK, GLA_CHUNK), 1)

    @pl.loop(0, SEQ_TILE // GLA_CHUNK)
    def _(c):
        r0 = pl.multiple_of(c * GLA_CHUNK, GLA_CHUNK)
        rows = pl.ds(r0, GLA_CHUNK)
        qs_c = qs_ref[rows, :]
        b_c = b_ref[rows, :]

        def col_body(j, a_heads):
            kj = k_ref[pl.ds(r0 + j, 1), :]
            bj = b_ref[pl.ds(r0 + j, 1), :]
            p = qs_c * kj * jnp.exp(jnp.minimum(b_c - bj, 0.0))
            p = jnp.where(row_c >= j, p, 0.0)
            new = []
            for h in range(GLA_HEADS):
                s = jnp.sum(p[:, h * hdk:(h + 1) * hdk], axis=-1, keepdims=True)
                new.append(jnp.where(lane_c == j, s, a_heads[h]))
            return tuple(new)

        a_heads = lax.fori_loop(
            0, GLA_CHUNK, col_body,
            tuple(jnp.zeros((GLA_CHUNK, GLA_CHUNK), F32) for _ in range(GLA_HEADS)))

        b_last = b_ref[pl.ds(r0 + GLA_CHUNK - 1, 1), :]
        qd = (qs_c * jnp.exp(b_c)).astype(BF16)
        kd = k_ref[rows, :] * jnp.exp(b_last - b_c)
        e_last = jnp.exp(b_last)
        for h in range(GLA_HEADS):
            ks = slice(h * hdk, (h + 1) * hdk)
            vs = slice(h * hdv, (h + 1) * hdv)
            v_ch = v_ref[rows, vs]
            st = new_state_ref[h]
            o_ref[rows, vs] = (_dot(qd[:, ks], st.astype(BF16))
                               + _dot(a_heads[h].astype(BF16), v_ch))
            kd_t = jnp.transpose(kd[:, ks]).astype(BF16)
            new_state_ref[h] = st * _row_to_col(e_last[:, ks]) + _dot(kd_t, v_ch)


def _load_weights(wint_hbm, wa_hbm, wb_hbm, wo_hbm, poolw_hbm,
                  wbig_ref, wa_ref, wb_ref, wo_ref, poolw_ref,
                  stage_ref, gate_stage_ref, pool_stage_ref, sems, *, n_before_gate, rank):
    rows = WEIGHT_LOAD_ROWS
    d_in, d_model = wint_hbm.shape
    assert n_before_gate % rows == 0 and (d_in - rank) % rows == 0 and d_model % rows == 0
    jobs = []
    for j in range((d_in - rank) // rows):
        row0 = j * rows if j * rows < n_before_gate else j * rows + rank
        jobs.append((wint_hbm, row0, wbig_ref.at[:, GATE_PAD + j * rows:GATE_PAD + (j + 1) * rows]))
    for src, dst in ((wa_hbm, wa_ref), (wb_hbm, wb_ref), (wo_hbm, wo_ref)):
        for r0 in range(0, d_model, rows):
            jobs.append((src, r0, dst.at[r0:r0 + rows, :]))

    slots = stage_ref.shape[0]

    def staged_copy(i):
        src, row0, _ = jobs[i]
        return pltpu.make_async_copy(src.at[pl.ds(row0, rows), :], stage_ref.at[i % slots], sems.at[i % slots])

    gate_copy = pltpu.make_async_copy(wint_hbm.at[pl.ds(n_before_gate, rank), :], gate_stage_ref, sems.at[slots])
    pool_copy = pltpu.make_async_copy(poolw_hbm, pool_stage_ref, sems.at[slots + 1])
    gate_copy.start()
    pool_copy.start()
    ahead = slots - 1
    for i in range(min(ahead, len(jobs))):
        staged_copy(i).start()
    for i, (src, _, dst) in enumerate(jobs):
        if i + ahead < len(jobs):
            staged_copy(i + ahead).start()
        staged_copy(i).wait()
        block = stage_ref[i % slots]
        dst[...] = (jnp.transpose(block) if src is wint_hbm else block).astype(BF16)
    gate_copy.wait()
    wbig_ref[:, 0:GATE_PAD] = jnp.zeros((d_model, GATE_PAD), BF16)
    wbig_ref[:, 0:rank] = jnp.transpose(gate_stage_ref[...]).astype(BF16)
    pool_copy.wait()
    poolw_ref[...] = pool_stage_ref[...].astype(BF16)


def _block_kernel(x_ref, wint_hbm, wa_hbm, wb_hbm, wo_hbm, poolw_hbm,
                  wgu_ref, bg_ref, gnw_ref, poolb_ref, pools_ref, lnw_ref, lnb_ref,
                  tri_ref,
                  out_ref,
                  qs_ref, k_ref, b_ref, v_ref, o_ref, state_ref, mid_state_ref, new_state_ref, ulast_ref,
                  ya_scale_ref, gate_a_ref, merged_b_ref, resid_ref, la_ref, factorisable_ref,
                  wbig_ref, wa_ref, wb_ref, wo_ref, poolw_ref,
                  stage_ref, gate_stage_ref, pool_stage_ref, load_sems,
                  *, d_model, dk, dv, alpha, tiles_per_seq, n_tiles):
    ts = SEQ_TILE
    hdk = dk // GLA_HEADS
    hdv = dv // GLA_HEADS
    s = pl.program_id(0)
    t = lax.rem(jnp.minimum(s, n_tiles - 1), tiles_per_seq)
    fast_chunks = [slice(r0, r0 + GLA_FAST_CHUNK) for r0 in range(0, ts, GLA_FAST_CHUNK)]
    assert len(fast_chunks) == 2

    @pl.when(s == 0)
    def _():
        _load_weights(wint_hbm, wa_hbm, wb_hbm, wo_hbm, poolw_hbm,
                      wbig_ref, wa_ref, wb_ref, wo_ref, poolw_ref,
                      stage_ref, gate_stage_ref, pool_stage_ref, load_sems,
                      n_before_gate=2 * dk + 2 * dv, rank=GLA_GATE_RANK)
        for ref in (o_ref, ya_scale_ref, gate_a_ref, merged_b_ref, resid_ref):
            @pl.loop(0, ts // GLA_CHUNK)
            def _(i, ref=ref):
                ref[pl.ds(pl.multiple_of(i * GLA_CHUNK, GLA_CHUNK), GLA_CHUNK), :] = jnp.zeros(
                    (GLA_CHUNK, ref.shape[1]), ref.dtype)

    @pl.when(t == 0)
    def _():
        state_ref[...] = jnp.zeros_like(state_ref)
        ulast_ref[...] = jnp.zeros_like(ulast_ref)

    def step(do_head, do_tail):
        if do_head:
            xf = x_ref[...]
            xb = xf.astype(BF16)

            def proj(n):
                return _dot(xb, wbig_ref[:, GATE_PAD + n * d_model:GATE_PAD + (n + 1) * d_model])

        if do_tail:
            o = o_ref[...]
            y_a_parts = []
            for h in range(GLA_HEADS):
                oh = o[:, h * hdv:(h + 1) * hdv]
                y_a_parts.append(oh * lax.rsqrt(jnp.mean(oh * oh, axis=-1, keepdims=True) + NORM_EPS))
            y_a = (jnp.concatenate(y_a_parts, axis=-1) * ya_scale_ref[...]).astype(BF16)

        if do_head:
            gate_qk = _dot(xb, wbig_ref[:, 0:GATE_PAD + d_model])
            a_low = gate_qk[:, :GLA_GATE_RANK]
            qk = gate_qk[:, GATE_PAD:]
            qs_ref[...] = qk[:, :dk] * (hdk ** -0.5)
            k_ref[...] = qk[:, dk:]
            gate_pre = _dot(a_low.astype(BF16), wgu_ref[...].astype(BF16)) + bg_ref[...]
            log_a = _log_sigmoid(gate_pre) * (1.0 / GLA_GATE_TAU)
            la_ref[...] = log_a
            la_hi, la_lo = _hi_lo(log_a)

        if do_tail:
            merged = (gate_a_ref[...] * _dot(y_a, wa_ref[...]) + merged_b_ref[...]).astype(BF16)

        if do_head:
            for rows in fast_chunks:
                b_ref[rows, :] = _dot(tri_ref[...], la_hi[rows, :]) + _dot(tri_ref[...], la_lo[rows, :])
            min_b = functools.reduce(
                jnp.minimum, [jnp.min(b_ref[rows.stop - 1:rows.stop, :]) for rows in fast_chunks])
            factorisable_ref[0] = (min_b >= -GLA_FACTOR_MAX_DECAY).astype(jnp.int32)
            v_ref[...] = proj(_V).astype(BF16)
            gla_operands = [_gla_factor_operands(qs_ref, k_ref, b_ref, rows) for rows in fast_chunks]

        if do_tail:
            out_ref[...] = resid_ref[...] + _dot(merged, wo_ref[...])

        if do_head:
            u = proj(_U)
            t_abs = t * ts + lax.broadcasted_iota(jnp.int32, (ts, 1), 0)
            gdim = d_model // len(POOL_WINDOWS)
            gslices = [slice(gi * gdim, (gi + 1) * gdim) for gi in range(len(POOL_WINDOWS))]
            pooled_in = []
            for gi, gs in enumerate(gslices):
                win = jnp.concatenate([ulast_ref[:, gs], u[:, gs]], axis=0)
                span = 1
                while span < POOL_WINDOWS[gi]:
                    win = win + pltpu.roll(win, span, axis=0)
                    span *= 2
                count = jnp.minimum(t_abs + 1, POOL_WINDOWS[gi]).astype(F32)
                pooled_in.append((win[POOL_LOOKBACK:] / count - u[:, gs]).astype(BF16))
            ulast_ref[...] = u[ts - POOL_LOOKBACK:, :]

        if do_head:
            def gate_factor_columns(h):
                def issue():
                    cols = slice(h * hdv, (h + 1) * hdv)
                    g0 = GATE_PAD + _G * d_model + h * hdv
                    g = _dot(xb, wbig_ref[:, g0:g0 + hdv])
                    ya_scale_ref[:, cols] = gnw_ref[:, cols] * (g * _sigmoid(g))
                return issue

            def merge_gate_columns(h):
                def issue():
                    c0 = GATE_PAD + _GATE_A * d_model + h * hdv
                    cols = slice(c0, c0 + hdv)
                    gate_a_ref[:, h * hdv:(h + 1) * hdv] = _sigmoid(_dot(xb, wbig_ref[:, cols]))
                return issue

            chunk_states = (state_ref, mid_state_ref, new_state_ref)
            chunk_fillers = (gate_factor_columns, merge_gate_columns)
            for c, rows in enumerate(fast_chunks):
                _gla_chunk_factorised(gla_operands[c], v_ref, o_ref, rows, chunk_states[c], chunk_states[c + 1],
                                      hdk=hdk, hdv=hdv,
                                      fillers=[chunk_fillers[c](h) for h in range(GLA_HEADS)])

        if do_head:
            z = proj(_Z)
            silu_z = z * _sigmoid(z)

        if do_tail:
            r = out_ref[...]
            mu = jnp.mean(r, axis=-1, keepdims=True)
            rc = r - mu
            var = jnp.mean(rc * rc, axis=-1, keepdims=True)
            out_ref[...] = (rc * lax.rsqrt(var + NORM_EPS) * lnw_ref[...] + lnb_ref[...]).astype(out_ref.dtype)

        if do_head:
            p_parts =[_dot(pooled_in[gi], poolw_ref[gs, :]) for gi, gs in enumerate(gslices)]
            gate_b = _sigmoid(proj(_GATE_B))
            y_b = (jnp.concatenate(p_parts, axis=-1) + poolb_ref[...]) * pools_ref[...] * silu_z
            resid_ref[...] = alpha * xf
            merged_b_ref[...] = gate_b * _dot(y_b.astype(BF16), wb_ref[...])

    pl.when(s < n_tiles)(functools.partial(step, True, True))
    pl.when(s == n_tiles)(functools.partial(step, False, True))

    @pl.when(s < n_tiles)
    def _():
        @pl.when(factorisable_ref[0] == 0)
        def _():
            span = GLA_FAST_CHUNK
            row = lax.broadcasted_iota(jnp.int32, (span, span), 0)
            col = lax.broadcasted_iota(jnp.int32, (span, span), 1)
            tri_chunk = jnp.where(row // GLA_CHUNK == col // GLA_CHUNK, tri_ref[...], jnp.zeros((), BF16))
            la_hi, la_lo = _hi_lo(la_ref[...])
            for rows in fast_chunks:
                b_ref[rows, :] = _dot(tri_chunk, la_hi[rows, :]) + _dot(tri_chunk, la_lo[rows, :])
            _gla_tile_direct(qs_ref, k_ref, b_ref, v_ref, o_ref, state_ref, new_state_ref, hdk=hdk, hdv=hdv)

        state_ref[...] = new_state_ref[...]


def _const_spec(shape):
    nd = len(shape)
    return pl.BlockSpec(shape, lambda s: (0,) * nd, pipeline_mode=pl.Buffered(1))


def _tri_matrix():
    r = np.arange(GLA_FAST_CHUNK)
    return jnp.asarray((r[None, :] <= r[:, None]).astype(np.float32), dtype=BF16)


def _layer(x, w_in_t, w_gate_up, b_gate, gn_w, pool_w, pool_b, pool_scale, w_a, w_b, w_o, ln_w, ln_b,
           *, alpha):
    bsz, seq, d_model = x.shape
    rank, dk = w_gate_up.shape
    dv = gn_w.shape[0]
    assert seq % SEQ_TILE == 0 and SEQ_TILE % GLA_FAST_CHUNK == 0 and GLA_FAST_CHUNK % GLA_CHUNK == 0
    assert dv == d_model and 2 * dk == d_model and rank == GLA_GATE_RANK
    assert pool_w.shape[0] == len(POOL_WINDOWS)
    assert all(w & (w - 1) == 0 and w <= POOL_LOOKBACK for w in POOL_WINDOWS)
    d_in = w_in_t.shape[0]
    gdim = pool_w.shape[1]
    row = lambda a: a.reshape(1, -1)
    hbm_weights = (w_in_t, w_a, w_b, w_o, pool_w.reshape(d_model, gdim))
    small = (w_gate_up, row(b_gate), row(gn_w), row(pool_b), row(pool_scale), row(ln_w), row(ln_b),
             _tri_matrix())
    operands = (x, *hbm_weights, *small)
    tiles_per_seq = seq // SEQ_TILE
    n_tiles = bsz * tiles_per_seq

    def tile_block(tile):
        return (tile // tiles_per_seq, tile % tiles_per_seq, 0)

    in_specs = [pl.BlockSpec((None, SEQ_TILE, d_model), lambda s: tile_block(jnp.minimum(s, n_tiles - 1)))]
    in_specs += [pl.BlockSpec(memory_space=pl.ANY) for _ in hbm_weights]
    in_specs += [_const_spec(a.shape) for a in small]
    kernel = functools.partial(_block_kernel, d_model=d_model, dk=dk, dv=dv, alpha=alpha,
                               tiles_per_seq=tiles_per_seq, n_tiles=n_tiles)
    state_shape = (GLA_HEADS, dk // GLA_HEADS, dv // GLA_HEADS)
    return pl.pallas_call(
        kernel,
        grid=(n_tiles + 1,),
        in_specs=in_specs,
        out_specs=pl.BlockSpec((None, SEQ_TILE, d_model), lambda s: tile_block(jnp.maximum(s - 1, 0))),
        out_shape=jax.ShapeDtypeStruct(x.shape, x.dtype),
        scratch_shapes=[
            pltpu.VMEM((SEQ_TILE, dk), F32),
            pltpu.VMEM((SEQ_TILE, dk), F32),
            pltpu.VMEM((SEQ_TILE, dk), F32),
            pltpu.VMEM((SEQ_TILE, dv), BF16),
            pltpu.VMEM((SEQ_TILE, dv), F32),
            pltpu.VMEM(state_shape, F32),
            pltpu.VMEM(state_shape, F32),
            pltpu.VMEM(state_shape, F32),
            pltpu.VMEM((POOL_LOOKBACK, d_model), F32),
            pltpu.VMEM((SEQ_TILE, dv), F32),
            pltpu.VMEM((SEQ_TILE, d_model), F32),
            pltpu.VMEM((SEQ_TILE, d_model), F32),
            pltpu.VMEM((SEQ_TILE, d_model), F32),
            pltpu.VMEM((SEQ_TILE, dk), F32),
            pltpu.SMEM((1,), jnp.int32),
            pltpu.VMEM((d_model, GATE_PAD + d_in - rank), BF16),
            pltpu.VMEM(w_a.shape, BF16),
            pltpu.VMEM(w_b.shape, BF16),
            pltpu.VMEM(w_o.shape, BF16),
            pltpu.VMEM((d_model, gdim), BF16),
            pltpu.VMEM((WEIGHT_STAGE_SLOTS, WEIGHT_LOAD_ROWS, d_model), F32),
            pltpu.VMEM((rank, d_model), F32),
            pltpu.VMEM((d_model, gdim), F32),
            pltpu.SemaphoreType.DMA((WEIGHT_STAGE_SLOTS + 2,)),
        ],
        compiler_params=pltpu.CompilerParams(
            dimension_semantics=("arbitrary",),
            vmem_limit_bytes=VMEM_LIMIT_BYTES),
        name="hybrid_gla_pool_layer",
    )(*operands)


def kernel(x, w_in, w_gate_up, b_gate, gn_w, pool_w, pool_b, pool_scale, w_a, w_b, w_o, ln_w, ln_b):
    depth = w_in.shape[0]
    alpha = (2.0 * depth) ** 0.25
    w_in_t = jnp.swapaxes(w_in, 1, 2)
    for l in range(depth):
        x = _layer(x, w_in_t[l], w_gate_up[l], b_gate[l], gn_w[l], pool_w[l], pool_b[l], pool_scale[l],
                   w_a[l], w_b[l], w_o[l], ln_w[l], ln_b[l], alpha=alpha)
    return x
```

```python
import functools

import numpy as np
import jax
import jax.numpy as jnp
from jax import lax
from jax.experimental import pallas as pl
from jax.experimental.pallas import tpu as pltpu

GLA_HEADS = 4
GLA_GATE_RANK = 16
GLA_GATE_TAU = 16.0
POOL_WINDOWS = (2, 4, 8, 16)
POOL_LOOKBACK = 16
NORM_EPS = 1e-5

SEQ_TILE = 512
GLA_FAST_CHUNK = 256
GLA_CHUNK = 64
WEIGHT_LOAD_ROWS = 256
WEIGHT_STAGE_SLOTS = 6
VMEM_LIMIT_BYTES = 60 * 1024 * 1024
GLA_FACTOR_MAX_DECAY = 50.0

_QK, _V, _G, _U, _Z, _GATE_A, _GATE_B = range(7)
GATE_PAD = 128

F32 = jnp.float32
BF16 = jnp.bfloat16
F32_SUBLANES = 8


def _dot(a, b):
    return jnp.dot(a, b, preferred_element_type=F32)


def _sigmoid(x):
    return 1.0 / (1.0 + jnp.exp(-x))


def _log_sigmoid(x):
    return jnp.minimum(x, 0.0) - jnp.log(1.0 + jnp.exp(-jnp.abs(x)))


def _hi_lo(x):
    hi = x.astype(BF16)
    return hi, (x - hi.astype(F32)).astype(BF16)


def _row_to_col(row):
    return jnp.transpose(jnp.broadcast_to(row, (F32_SUBLANES, row.shape[1])))[:, 0:1]


def _gla_factor_operands(qs_ref, k_ref, b_ref, rows):
    b = b_ref[rows, :]
    e_pos = jnp.exp(b)
    e_last = e_pos[GLA_FAST_CHUNK - 1:GLA_FAST_CHUNK, :]
    k_neg = k_ref[rows, :] * jnp.exp(-b)
    qd = (qs_ref[rows, :] * e_pos).astype(BF16)
    kn = k_neg.astype(BF16)
    kd = k_neg * e_last
    return qd, kn, kd, e_last


def _gla_chunk_factorised(operands, v_ref, o_ref, rows, state_ref, new_state_ref, *, hdk, hdv, fillers):
    ts = GLA_FAST_CHUNK
    qd, kn, kd, e_last = operands
    heads = range(GLA_HEADS)
    ks = [slice(h * hdk, (h + 1) * hdk) for h in heads]
    vs = [slice(h * hdv, (h + 1) * hdv) for h in heads]
    causal = (lax.broadcasted_iota(jnp.int32, (ts, ts), 0)
              >= lax.broadcasted_iota(jnp.int32, (ts, ts), 1))
    masked = []
    for h in heads:
        scores = lax.dot_general(qd[:, ks[h]], kn[:, ks[h]], (((1,), (1,)), ((), ())),
                                 preferred_element_type=F32)
        fillers[h]()
        masked.append(jnp.where(causal, scores, 0.0).astype(BF16))
    for h in heads:
        kd_t = jnp.transpose(kd[:, ks[h]]).astype(BF16)
        new_state_ref[h] = state_ref[h] * _row_to_col(e_last[:, ks[h]]) + _dot(kd_t, v_ref[rows, vs[h]])
    for h in heads:
        o_ref[rows, vs[h]] = (_dot(qd[:, ks[h]], state_ref[h].astype(BF16))
                              + _dot(masked[h], v_ref[rows, vs[h]]))


def _gla_tile_direct(qs_ref, k_ref, b_ref, v_ref, o_ref, state_ref, new_state_ref, *, hdk, hdv):
    new_state_ref[...] = state_ref[...]
    row_c = lax.broadcasted_iota(jnp.int32, (GLA_CHUNK, 1), 0)
    lane_c = lax.broadcasted_iota(jnp.int32, (GLA_CHUNK, GLA_CHUNK), 1)

    @pl.loop(0, SEQ_TILE // GLA_CHUNK)
    def _(c):
        r0 = pl.multiple_of(c * GLA_CHUNK, GLA_CHUNK)
        rows = pl.ds(r0, GLA_CHUNK)
        qs_c = qs_ref[rows, :]
        b_c = b_ref[rows, :]

        def col_body(j, a_heads):
            kj = k_ref[pl.ds(r0 + j, 1), :]
            bj = b_ref[pl.ds(r0 + j, 1), :]
            p = qs_c * kj * jnp.exp(jnp.minimum(b_c - bj, 0.0))
            p = jnp.where(row_c >= j, p, 0.0)
            new = []
            for h in range(GLA_HEADS):
                s = jnp.sum(p[:, h * hdk:(h + 1) * hdk], axis=-1, keepdims=True)
                new.append(jnp.where(lane_c == j, s, a_heads[h]))
            return tuple(new)

        a_heads = lax.fori_loop(
            0, GLA_CHUNK, col_body,
            tuple(jnp.zeros((GLA_CHUNK, GLA_CHUNK), F32) for _ in range(GLA_HEADS)))

        b_last = b_ref[pl.ds(r0 + GLA_CHUNK - 1, 1), :]
        qd = (qs_c * jnp.exp(b_c)).astype(BF16)
        kd = k_ref[rows, :] * jnp.exp(b_last - b_c)
        e_last = jnp.exp(b_last)
        for h in range(GLA_HEADS):
            ks = slice(h * hdk, (h + 1) * hdk)
            vs = slice(h * hdv, (h + 1) * hdv)
            v_ch = v_ref[rows, vs]
            st = new_state_ref[h]
            o_ref[rows, vs] = (_dot(qd[:, ks], st.astype(BF16))
                               + _dot(a_heads[h].astype(BF16), v_ch))
            kd_t = jnp.transpose(kd[:, ks]).astype(BF16)
            new_state_ref[h] = st * _row_to_col(e_last[:, ks]) + _dot(kd_t, v_ch)


def _load_weights(wint_hbm, wa_hbm, wb_hbm, wo_hbm, poolw_hbm,
                  wbig_ref, wa_ref, wb_ref, wo_ref, poolw_ref,
                  stage_ref, gate_stage_ref, pool_stage_ref, sems, *, n_before_gate, rank):
    rows = WEIGHT_LOAD_ROWS
    d_in, d_model = wint_hbm.shape
    assert n_before_gate % rows == 0 and (d_in - rank) % rows == 0 and d_model % rows == 0
    jobs = []
    for j in range((d_in - rank) // rows):
        row0 = j * rows if j * rows < n_before_gate else j * rows + rank
        jobs.append((wint_hbm, row0, wbig_ref.at[:, GATE_PAD + j * rows:GATE_PAD + (j + 1) * rows]))
    for src, dst in ((wa_hbm, wa_ref), (wb_hbm, wb_ref), (wo_hbm, wo_ref)):
        for r0 in range(0, d_model, rows):
            jobs.append((src, r0, dst.at[r0:r0 + rows, :]))

    slots = stage_ref.shape[0]

    def staged_copy(i):
        src, row0, _ = jobs[i]
        return pltpu.make_async_copy(src.at[pl.ds(row0, rows), :], stage_ref.at[i % slots], sems.at[i % slots])

    gate_copy = pltpu.make_async_copy(wint_hbm.at[pl.ds(n_before_gate, rank), :], gate_stage_ref, sems.at[slots])
    pool_copy = pltpu.make_async_copy(poolw_hbm, pool_stage_ref, sems.at[slots + 1])
    gate_copy.start()
    pool_copy.start()
    ahead = slots - 1
    for i in range(min(ahead, len(jobs))):
        staged_copy(i).start()
    for i, (src, _, dst) in enumerate(jobs):
        if i + ahead < len(jobs):
            staged_copy(i + ahead).start()
        staged_copy(i).wait()
        block = stage_ref[i % slots]
        dst[...] = (jnp.transpose(block) if src is wint_hbm else block).astype(BF16)
    gate_copy.wait()
    wbig_ref[:, 0:GATE_PAD] = jnp.zeros((d_model, GATE_PAD), BF16)
    wbig_ref[:, 0:rank] = jnp.transpose(gate_stage_ref[...]).astype(BF16)
    pool_copy.wait()
    poolw_ref[...] = pool_stage_ref[...].astype(BF16)


def _block_kernel(x_ref, wint_hbm, wa_hbm, wb_hbm, wo_hbm, poolw_hbm,
                  wgu_ref, bg_ref, gnw_ref, poolb_ref, pools_ref, lnw_ref, lnb_ref,
                  tri_ref,
                  out_ref,
                  qs_ref, k_ref, b_ref, v_ref, o_ref, state_ref, mid_state_ref, new_state_ref, ulast_ref,
                  ya_scale_ref, gate_a_ref, merged_b_ref, resid_ref, la_ref, factorisable_ref,
                  wbig_ref, wa_ref, wb_ref, wo_ref, poolw_ref,
                  stage_ref, gate_stage_ref, pool_stage_ref, load_sems,
                  *, d_model, dk, dv, alpha, tiles_per_seq, n_tiles):
    ts = SEQ_TILE
    hdk = dk // GLA_HEADS
    hdv = dv // GLA_HEADS
    s = pl.program_id(0)
    t = lax.rem(jnp.minimum(s, n_tiles - 1), tiles_per_seq)
    fast_chunks = [slice(r0, r0 + GLA_FAST_CHUNK) for r0 in range(0, ts, GLA_FAST_CHUNK)]
    assert len(fast_chunks) == 2

    @pl.when(s == 0)
    def _():
        _load_weights(wint_hbm, wa_hbm, wb_hbm, wo_hbm, poolw_hbm,
                      wbig_ref, wa_ref, wb_ref, wo_ref, poolw_ref,
                      stage_ref, gate_stage_ref, pool_stage_ref, load_sems,
                      n_before_gate=2 * dk + 2 * dv, rank=GLA_GATE_RANK)
        for ref in (o_ref, ya_scale_ref, gate_a_ref, merged_b_ref, resid_ref):
            @pl.loop(0, ts // GLA_CHUNK)
            def _(i, ref=ref):
                ref[pl.ds(pl.multiple_of(i * GLA_CHUNK, GLA_CHUNK), GLA_CHUNK), :] = jnp.zeros(
                    (GLA_CHUNK, ref.shape[1]), ref.dtype)

    @pl.when(t == 0)
    def _():
        state_ref[...] = jnp.zeros_like(state_ref)
        ulast_ref[...] = jnp.zeros_like(ulast_ref)

    def step(do_head, do_tail):
        if do_head:
            xf = x_ref[...]
            xb = xf.astype(BF16)

            def proj(n):
                return _dot(xb, wbig_ref[:, GATE_PAD + n * d_model:GATE_PAD + (n + 1) * d_model])

        if do_tail:
            o = o_ref[...]
            y_a_parts = []
            for h in range(GLA_HEADS):
                oh = o[:, h * hdv:(h + 1) * hdv]
                y_a_parts.append(oh * lax.rsqrt(jnp.mean(oh * oh, axis=-1, keepdims=True) + NORM_EPS))
            y_a = (jnp.concatenate(y_a_parts, axis=-1) * ya_scale_ref[...]).astype(BF16)

        if do_head:
            gate_qk = _dot(xb, wbig_ref[:, 0:GATE_PAD + d_model])
            a_low = gate_qk[:, :GLA_GATE_RANK]
            qk = gate_qk[:, GATE_PAD:]
            qs_ref[...] = qk[:, :dk] * (hdk ** -0.5)
            k_ref[...] = qk[:, dk:]
            gate_pre = _dot(a_low.astype(BF16), wgu_ref[...].astype(BF16)) + bg_ref[...]
            log_a = _log_sigmoid(gate_pre) * (1.0 / GLA_GATE_TAU)
            la_ref[...] = log_a
            la_hi, la_lo = _hi_lo(log_a)

        if do_tail:
            merged = (gate_a_ref[...] * _dot(y_a, wa_ref[...]) + merged_b_ref[...]).astype(BF16)

        if do_head:
            for rows in fast_chunks:
                b_ref[rows, :] = _dot(tri_ref[...], la_hi[rows, :]) + _dot(tri_ref[...], la_lo[rows, :])
            min_b = functools.reduce(
                jnp.minimum, [jnp.min(b_ref[rows.stop - 1:rows.stop, :]) for rows in fast_chunks])
            factorisable_ref[0] = (min_b >= -GLA_FACTOR_MAX_DECAY).astype(jnp.int32)
            v_ref[...] = proj(_V).astype(BF16)
            gla_operands = [_gla_factor_operands(qs_ref, k_ref, b_ref, rows) for rows in fast_chunks]

        if do_tail:
            for rows in fast_chunks:
                out_ref[rows, :] = resid_ref[rows, :] + _dot(merged[rows, :], wo_ref[...])

        if do_head:
            u = proj(_U)
            t_abs = t * ts + lax.broadcasted_iota(jnp.int32, (ts, 1), 0)
            gdim = d_model // len(POOL_WINDOWS)
            gslices = [slice(gi * gdim, (gi + 1) * gdim) for gi in range(len(POOL_WINDOWS))]
            pooled_in = []
            for gi, gs in enumerate(gslices):
                win = jnp.concatenate([ulast_ref[:, gs], u[:, gs]], axis=0)
                span = 1
                while span < POOL_WINDOWS[gi]:
                    win = win + pltpu.roll(win, span, axis=0)
                    span *= 2
                count = jnp.minimum(t_abs + 1, POOL_WINDOWS[gi]).astype(F32)
                pooled_in.append((win[POOL_LOOKBACK:] / count - u[:, gs]).astype(BF16))
            ulast_ref[...] = u[ts - POOL_LOOKBACK:, :]

        if do_head:
            def gate_factor_columns(h):
                def issue():
                    cols = slice(h * hdv, (h + 1) * hdv)
                    g0 = GATE_PAD + _G * d_model + h * hdv
                    g = _dot(xb, wbig_ref[:, g0:g0 + hdv])
                    ya_scale_ref[:, cols] = gnw_ref[:, cols] * (g * _sigmoid(g))
                return issue

            def merge_gate_columns(h):
                def issue():
                    c0 = GATE_PAD + _GATE_A * d_model + h * hdv
                    cols = slice(c0, c0 + hdv)
                    gate_a_ref[:, h * hdv:(h + 1) * hdv] = _sigmoid(_dot(xb, wbig_ref[:, cols]))
                return issue

            chunk_states = (state_ref, mid_state_ref, new_state_ref)
            chunk_fillers = (gate_factor_columns, merge_gate_columns)
            for c, rows in enumerate(fast_chunks):
                _gla_chunk_factorised(gla_operands[c], v_ref, o_ref, rows, chunk_states[c], chunk_states[c + 1],
                                      hdk=hdk, hdv=hdv,
                                      fillers=[chunk_fillers[c](h) for h in range(GLA_HEADS)])

        if do_head:
            z = proj(_Z)
            silu_z = z * _sigmoid(z)

        if do_tail:
            for rows in fast_chunks:
                r = out_ref[rows, :]
                mu = jnp.mean(r, axis=-1, keepdims=True)
                rc = r - mu
                var = jnp.mean(rc * rc, axis=-1, keepdims=True)
                out_ref[rows, :] = (rc * lax.rsqrt(var + NORM_EPS) * lnw_ref[...]
                                    + lnb_ref[...]).astype(out_ref.dtype)

        if do_head:
            p_parts =[_dot(pooled_in[gi], poolw_ref[gs, :]) for gi, gs in enumerate(gslices)]
            gate_b = _sigmoid(proj(_GATE_B))
            y_b = (jnp.concatenate(p_parts, axis=-1) + poolb_ref[...]) * pools_ref[...] * silu_z
            resid_ref[...] = alpha * xf
            merged_b_ref[...] = gate_b * _dot(y_b.astype(BF16), wb_ref[...])

    pl.when(s < n_tiles)(functools.partial(step, True, True))
    pl.when(s == n_tiles)(functools.partial(step, False, True))

    @pl.when(s < n_tiles)
    def _():
        @pl.when(factorisable_ref[0] == 0)
        def _():
            span = GLA_FAST_CHUNK
            row = lax.broadcasted_iota(jnp.int32, (span, span), 0)
            col = lax.broadcasted_iota(jnp.int32, (span, span), 1)
            tri_chunk = jnp.where(row // GLA_CHUNK == col // GLA_CHUNK, tri_ref[...], jnp.zeros((), BF16))
            la_hi, la_lo = _hi_lo(la_ref[...])
            for rows in fast_chunks:
                b_ref[rows, :] = _dot(tri_chunk, la_hi[rows, :]) + _dot(tri_chunk, la_lo[rows, :])
            _gla_tile_direct(qs_ref, k_ref, b_ref, v_ref, o_ref, state_ref, new_state_ref, hdk=hdk, hdv=hdv)

        state_ref[...] = new_state_ref[...]


def _const_spec(shape):
    nd = len(shape)
    return pl.BlockSpec(shape, lambda s: (0,) * nd, pipeline_mode=pl.Buffered(1))


def _tri_matrix():
    r = np.arange(GLA_FAST_CHUNK)
    return jnp.asarray((r[None, :] <= r[:, None]).astype(np.float32), dtype=BF16)


def _layer(x, w_in_t, w_gate_up, b_gate, gn_w, pool_w, pool_b, pool_scale, w_a, w_b, w_o, ln_w, ln_b,
           *, alpha):
    bsz, seq, d_model = x.shape
    rank, dk = w_gate_up.shape
    dv = gn_w.shape[0]
    assert seq % SEQ_TILE == 0 and SEQ_TILE % GLA_FAST_CHUNK == 0 and GLA_FAST_CHUNK % GLA_CHUNK == 0
    assert dv == d_model and 2 * dk == d_model and rank == GLA_GATE_RANK
    assert pool_w.shape[0] == len(POOL_WINDOWS)
    assert all(w & (w - 1) == 0 and w <= POOL_LOOKBACK for w in POOL_WINDOWS)
    d_in = w_in_t.shape[0]
    gdim = pool_w.shape[1]
    row = lambda a: a.reshape(1, -1)
    hbm_weights = (w_in_t, w_a, w_b, w_o, pool_w.reshape(d_model, gdim))
    small = (w_gate_up, row(b_gate), row(gn_w), row(pool_b), row(pool_scale), row(ln_w), row(ln_b),
             _tri_matrix())
    operands = (x, *hbm_weights, *small)
    tiles_per_seq = seq // SEQ_TILE
    n_tiles = bsz * tiles_per_seq

    def tile_block(tile):
        return (tile // tiles_per_seq, tile % tiles_per_seq, 0)

    in_specs = [pl.BlockSpec((None, SEQ_TILE, d_model), lambda s: tile_block(jnp.minimum(s, n_tiles - 1)))]
    in_specs += [pl.BlockSpec(memory_space=pl.ANY) for _ in hbm_weights]
    in_specs += [_const_spec(a.shape) for a in small]
    kernel = functools.partial(_block_kernel, d_model=d_model, dk=dk, dv=dv, alpha=alpha,
                               tiles_per_seq=tiles_per_seq, n_tiles=n_tiles)
    state_shape = (GLA_HEADS, dk // GLA_HEADS, dv // GLA_HEADS)
    return pl.pallas_call(
        kernel,
        grid=(n_tiles + 1,),
        in_specs=in_specs,
        out_specs=pl.BlockSpec((None, SEQ_TILE, d_model), lambda s: tile_block(jnp.maximum(s - 1, 0))),
        out_shape=jax.ShapeDtypeStruct(x.shape, x.dtype),
        scratch_shapes=[
            pltpu.VMEM((SEQ_TILE, dk), F32),
            pltpu.VMEM((SEQ_TILE, dk), F32),
            pltpu.VMEM((SEQ_TILE, dk), F32),
            pltpu.VMEM((SEQ_TILE, dv), BF16),
            pltpu.VMEM((SEQ_TILE, dv), F32),
            pltpu.VMEM(state_shape, F32),
            pltpu.VMEM(state_shape, F32),
            pltpu.VMEM(state_shape, F32),
            pltpu.VMEM((POOL_LOOKBACK, d_model), F32),
            pltpu.VMEM((SEQ_TILE, dv), F32),
            pltpu.VMEM((SEQ_TILE, d_model), F32),
            pltpu.VMEM((SEQ_TILE, d_model), F32),
            pltpu.VMEM((SEQ_TILE, d_model), F32),
            pltpu.VMEM((SEQ_TILE, dk), F32),
            pltpu.SMEM((1,), jnp.int32),
            pltpu.VMEM((d_model, GATE_PAD + d_in - rank), BF16),
            pltpu.VMEM(w_a.shape, BF16),
            pltpu.VMEM(w_b.shape, BF16),
            pltpu.VMEM(w_o.shape, BF16),
            pltpu.VMEM((d_model, gdim), BF16),
            pltpu.VMEM((WEIGHT_STAGE_SLOTS, WEIGHT_LOAD_ROWS, d_model), F32),
            pltpu.VMEM((rank, d_model), F32),
            pltpu.VMEM((d_model, gdim), F32),
            pltpu.SemaphoreType.DMA((WEIGHT_STAGE_SLOTS + 2,)),
        ],
        compiler_params=pltpu.CompilerParams(
            dimension_semantics=("arbitrary",),
            vmem_limit_bytes=VMEM_LIMIT_BYTES),
        name="hybrid_gla_pool_layer",
    )(*operands)


def kernel(x, w_in, w_gate_up, b_gate, gn_w, pool_w, pool_b, pool_scale, w_a, w_b, w_o, ln_w, ln_b):
    depth = w_in.shape[0]
    alpha = (2.0 * depth) ** 0.25
    w_in_t = jnp.swapaxes(w_in, 1, 2)
    for l in range(depth):
        x = _layer(x, w_in_t[l], w_gate_up[l], b_gate[l], gn_w[l], pool_w[l], pool_b[l], pool_scale[l],
                   w_a[l], w_b[l], w_o[l], ln_w[l], ln_b[l], alpha=alpha)
    return x
```

```python
import functools

import numpy as np
import jax
import jax.numpy as jnp
from jax import lax
from jax.experimental import pallas as pl
from jax.experimental.pallas import tpu as pltpu

GLA_HEADS = 4
GLA_GATE_RANK = 16
GLA_GATE_TAU = 16.0
POOL_WINDOWS = (2, 4, 8, 16)
POOL_LOOKBACK = 16
NORM_EPS = 1e-5

SEQ_TILE = 512
GLA_FAST_CHUNK = 256
GLA_CHUNK = 64
WEIGHT_LOAD_ROWS = 256
WEIGHT_STAGE_SLOTS = 6
VMEM_LIMIT_BYTES = 60 * 1024 * 1024
GLA_FACTOR_MAX_DECAY = 50.0

_QK, _V, _G, _U, _Z, _GATE_A, _GATE_B = range(7)
GATE_PAD = 128

F32 = jnp.float32
BF16 = jnp.bfloat16
F32_SUBLANES = 8


def _dot(a, b):
    return jnp.dot(a, b, preferred_element_type=F32)


def _sigmoid(x):
    return 1.0 / (1.0 + jnp.exp(-x))


def _log_sigmoid(x):
    return jnp.minimum(x, 0.0) - jnp.log(1.0 + jnp.exp(-jnp.abs(x)))


def _hi_lo(x):
    hi = x.astype(BF16)
    return hi, (x - hi.astype(F32)).astype(BF16)


def _row_to_col(row):
    return jnp.transpose(jnp.broadcast_to(row, (F32_SUBLANES, row.shape[1])))[:, 0:1]


def _gla_factor_operands(qs_ref, k_ref, b_ref, rows):
    b = b_ref[rows, :]
    e_pos = jnp.exp(b)
    e_last = e_pos[GLA_FAST_CHUNK - 1:GLA_FAST_CHUNK, :]
    k_neg = k_ref[rows, :] * jnp.exp(-b)
    qd = (qs_ref[rows, :] * e_pos).astype(BF16)
    kn = k_neg.astype(BF16)
    kd = k_neg * e_last
    return qd, kn, kd, e_last


def _gla_chunk_factorised(operands, v_ref, o_ref, rows, state_ref, new_state_ref, *, hdk, hdv, fillers):
    ts = GLA_FAST_CHUNK
    qd, kn, kd, e_last = operands
    heads = range(GLA_HEADS)
    ks = [slice(h * hdk, (h + 1) * hdk) for h in heads]
    vs = [slice(h * hdv, (h + 1) * hdv) for h in heads]
    causal = (lax.broadcasted_iota(jnp.int32, (ts, ts), 0)
              >= lax.broadcasted_iota(jnp.int32, (ts, ts), 1))
    masked = []
    for h in heads:
        scores = lax.dot_general(qd[:, ks[h]], kn[:, ks[h]], (((1,), (1,)), ((), ())),
                                 preferred_element_type=F32)
        fillers[h]()
        masked.append(jnp.where(causal, scores, 0.0).astype(BF16))
    for h in heads:
        kd_t = jnp.transpose(kd[:, ks[h]]).astype(BF16)
        new_state_ref[h] = state_ref[h] * _row_to_col(e_last[:, ks[h]]) + _dot(kd_t, v_ref[rows, vs[h]])
    for h in heads:
        o_ref[rows, vs[h]] = (_dot(qd[:, ks[h]], state_ref[h].astype(BF16))
                              + _dot(masked[h], v_ref[rows, vs[h]]))


def _gla_tile_direct(qs_ref, k_ref, b_ref, v_ref, o_ref, state_ref, new_state_ref, *, hdk, hdv):
    new_state_ref[...] = state_ref[...]
    row_c = lax.broadcasted_iota(jnp.int32, (GLA_CHUNK, 1), 0)
    lane_c = lax.broadcasted_iota(jnp.int32, (GLA_CHUNK, GLA_CHUNK), 1)

    @pl.loop(0, SEQ_TILE // GLA_CHUNK)
    def _(c):
        r0 = pl.multiple_of(c * GLA_CHUNK, GLA_CHUNK)
        rows = pl.ds(r0, GLA_CHUNK)
        qs_c = qs_ref[rows, :]
        b_c = b_ref[rows, :]

        def col_body(j, a_heads):
            kj = k_ref[pl.ds(r0 + j, 1), :]
            bj = b_ref[pl.ds(r0 + j, 1), :]
            p = qs_c * kj * jnp.exp(jnp.minimum(b_c - bj, 0.0))
            p = jnp.where(row_c >= j, p, 0.0)
            new = []
            for h in range(GLA_HEADS):
                s = jnp.sum(p[:, h * hdk:(h + 1) * hdk], axis=-1, keepdims=True)
                new.append(jnp.where(lane_c == j, s, a_heads[h]))
            return tuple(new)

        a_heads = lax.fori_loop(
            0, GLA_CHUNK, col_body,
            tuple(jnp.zeros((GLA_CHUNK, GLA_CHUNK), F32) for _ in range(GLA_HEADS)))

        b_last = b_ref[pl.ds(r0 + GLA_CHUNK - 1, 1), :]
        qd = (qs_c * jnp.exp(b_c)).astype(BF16)
        kd = k_ref[rows, :] * jnp.exp(b_last - b_c)
        e_last = jnp.exp(b_last)
        for h in range(GLA_HEADS):
            ks = slice(h * hdk, (h + 1) * hdk)
            vs = slice(h * hdv, (h + 1) * hdv)
            v_ch = v_ref[rows, vs]
            st = new_state_ref[h]
            o_ref[rows, vs] = (_dot(qd[:, ks], st.astype(BF16))
                               + _dot(a_heads[h].astype(BF16), v_ch))
            kd_t = jnp.transpose(kd[:, ks]).astype(BF16)
            new_state_ref[h] = st * _row_to_col(e_last[:, ks]) + _dot(kd_t, v_ch)


def _load_weights(wint_hbm, wa_hbm, wb_hbm, wo_hbm, poolw_hbm,
                  wbig_ref, wa_ref, wb_ref, wo_ref, poolw_ref,
                  stage_ref, gate_stage_ref, pool_stage_ref, sems, *, n_before_gate, rank):
    rows = WEIGHT_LOAD_ROWS
    d_in, d_model = wint_hbm.shape
    assert n_before_gate % rows == 0 and (d_in - rank) % rows == 0 and d_model % rows == 0
    jobs = []
    for j in range((d_in - rank) // rows):
        row0 = j * rows if j * rows < n_before_gate else j * rows + rank
        jobs.append((wint_hbm, row0, wbig_ref.at[:, GATE_PAD + j * rows:GATE_PAD + (j + 1) * rows]))
    for src, dst in ((wa_hbm, wa_ref), (wb_hbm, wb_ref), (wo_hbm, wo_ref)):
        for r0 in range(0, d_model, rows):
            jobs.append((src, r0, dst.at[r0:r0 + rows, :]))

    slots = stage_ref.shape[0]

    def staged_copy(i):
        src, row0, _ = jobs[i]
        return pltpu.make_async_copy(src.at[pl.ds(row0, rows), :], stage_ref.at[i % slots], sems.at[i % slots])

    gate_copy = pltpu.make_async_copy(wint_hbm.at[pl.ds(n_before_gate, rank), :], gate_stage_ref, sems.at[slots])
    pool_copy = pltpu.make_async_copy(poolw_hbm, pool_stage_ref, sems.at[slots + 1])
    gate_copy.start()
    pool_copy.start()
    ahead = slots - 1
    for i in range(min(ahead, len(jobs))):
        staged_copy(i).start()
    for i, (src, _, dst) in enumerate(jobs):
        if i + ahead < len(jobs):
            staged_copy(i + ahead).start()
        staged_copy(i).wait()
        block = stage_ref[i % slots]
        dst[...] = (jnp.transpose(block) if src is wint_hbm else block).astype(BF16)
    gate_copy.wait()
    wbig_ref[:, 0:GATE_PAD] = jnp.zeros((d_model, GATE_PAD), BF16)
    wbig_ref[:, 0:rank] = jnp.transpose(gate_stage_ref[...]).astype(BF16)
    pool_copy.wait()
    poolw_ref[...] = pool_stage_ref[...].astype(BF16)


def _block_kernel(x_ref, wint_hbm, wa_hbm, wb_hbm, wo_hbm, poolw_hbm,
                  wgu_ref, bg_ref, gnw_ref, poolb_ref, pools_ref, lnw_ref, lnb_ref,
                  tri_ref,
                  out_ref,
                  qs_ref, k_ref, b_ref, v_ref, o_ref, state_ref, mid_state_ref, new_state_ref, ulast_ref,
                  ya_scale_ref, gate_a_ref, merged_b_ref, resid_ref, la_ref, factorisable_ref,
                  wbig_ref, wa_ref, wb_ref, wo_ref, poolw_ref,
                  stage_ref, gate_stage_ref, pool_stage_ref, load_sems,
                  *, d_model, dk, dv, alpha, tiles_per_seq, n_tiles):
    ts = SEQ_TILE
    hdk = dk // GLA_HEADS
    hdv = dv // GLA_HEADS
    s = pl.program_id(0)
    t = lax.rem(jnp.minimum(s, n_tiles - 1), tiles_per_seq)
    fast_chunks = [slice(r0, r0 + GLA_FAST_CHUNK) for r0 in range(0, ts, GLA_FAST_CHUNK)]
    assert len(fast_chunks) == 2

    @pl.when(s == 0)
    def _():
        _load_weights(wint_hbm, wa_hbm, wb_hbm, wo_hbm, poolw_hbm,
                      wbig_ref, wa_ref, wb_ref, wo_ref, poolw_ref,
                      stage_ref, gate_stage_ref, pool_stage_ref, load_sems,
                      n_before_gate=2 * dk + 2 * dv, rank=GLA_GATE_RANK)
        for ref in (o_ref, ya_scale_ref, gate_a_ref, merged_b_ref, resid_ref):
            @pl.loop(0, ts // GLA_CHUNK)
            def _(i, ref=ref):
                ref[pl.ds(pl.multiple_of(i * GLA_CHUNK, GLA_CHUNK), GLA_CHUNK), :] = jnp.zeros(
                    (GLA_CHUNK, ref.shape[1]), ref.dtype)

    @pl.when(t == 0)
    def _():
        state_ref[...] = jnp.zeros_like(state_ref)
        ulast_ref[...] = jnp.zeros_like(ulast_ref)

    def step(do_head, do_tail):
        if do_head:
            xf = x_ref[...]
            xb = xf.astype(BF16)

            def proj(n):
                return _dot(xb, wbig_ref[:, GATE_PAD + n * d_model:GATE_PAD + (n + 1) * d_model])

        if do_tail:
            o = o_ref[...]
            y_a_parts = []
            for h in range(GLA_HEADS):
                oh = o[:, h * hdv:(h + 1) * hdv]
                y_a_parts.append(oh * lax.rsqrt(jnp.mean(oh * oh, axis=-1, keepdims=True) + NORM_EPS))
            y_a = (jnp.concatenate(y_a_parts, axis=-1) * ya_scale_ref[...]).astype(BF16)

        if do_head:
            gate_qk = _dot(xb, wbig_ref[:, 0:GATE_PAD + d_model])
            a_low = gate_qk[:, :GLA_GATE_RANK]
            qk = gate_qk[:, GATE_PAD:]
            qs_ref[...] = qk[:, :dk] * (hdk ** -0.5)
            k_ref[...] = qk[:, dk:]
            gate_pre = _dot(a_low.astype(BF16), wgu_ref[...].astype(BF16)) + bg_ref[...]
            log_a = _log_sigmoid(gate_pre) * (1.0 / GLA_GATE_TAU)
            la_ref[...] = log_a
            la_hi, la_lo = _hi_lo(log_a)

        if do_tail:
            merged = [(gate_a_ref[rows, :] * _dot(y_a[rows, :], wa_ref[...]) + merged_b_ref[rows, :]).astype(BF16)
                      for rows in fast_chunks]

        if do_head:
            for rows in fast_chunks:
                b_ref[rows, :] = _dot(tri_ref[...], la_hi[rows, :]) + _dot(tri_ref[...], la_lo[rows, :])
            min_b = functools.reduce(
                jnp.minimum, [jnp.min(b_ref[rows.stop - 1:rows.stop, :]) for rows in fast_chunks])
            factorisable_ref[0] = (min_b >= -GLA_FACTOR_MAX_DECAY).astype(jnp.int32)
            v_ref[...] = proj(_V).astype(BF16)
            gla_operands = [_gla_factor_operands(qs_ref, k_ref, b_ref, rows) for rows in fast_chunks]

        if do_tail:
            for c, rows in enumerate(fast_chunks):
                out_ref[rows, :] = resid_ref[rows, :] + _dot(merged[c], wo_ref[...])

        if do_head:
            u = proj(_U)
            t_abs = t * ts + lax.broadcasted_iota(jnp.int32, (ts, 1), 0)
            gdim = d_model // len(POOL_WINDOWS)
            gslices = [slice(gi * gdim, (gi + 1) * gdim) for gi in range(len(POOL_WINDOWS))]
            pooled_in = []
            for gi, gs in enumerate(gslices):
                win = jnp.concatenate([ulast_ref[:, gs], u[:, gs]], axis=0)
                span = 1
                while span < POOL_WINDOWS[gi]:
                    win = win + pltpu.roll(win, span, axis=0)
                    span *= 2
                count = jnp.minimum(t_abs + 1, POOL_WINDOWS[gi]).astype(F32)
                pooled_in.append((win[POOL_LOOKBACK:] / count - u[:, gs]).astype(BF16))
            ulast_ref[...] = u[ts - POOL_LOOKBACK:, :]

        if do_head:
            def gate_factor_columns(h):
                def issue():
                    cols = slice(h * hdv, (h + 1) * hdv)
                    g0 = GATE_PAD + _G * d_model + h * hdv
                    g = _dot(xb, wbig_ref[:, g0:g0 + hdv])
                    ya_scale_ref[:, cols] = gnw_ref[:, cols] * (g * _sigmoid(g))
                return issue

            def merge_gate_columns(h):
                def issue():
                    c0 = GATE_PAD + _GATE_A * d_model + h * hdv
                    cols = slice(c0, c0 + hdv)
                    gate_a_ref[:, h * hdv:(h + 1) * hdv] = _sigmoid(_dot(xb, wbig_ref[:, cols]))
                return issue

            chunk_states = (state_ref, mid_state_ref, new_state_ref)
            chunk_fillers = (gate_factor_columns, merge_gate_columns)
            for c, rows in enumerate(fast_chunks):
                _gla_chunk_factorised(gla_operands[c], v_ref, o_ref, rows, chunk_states[c], chunk_states[c + 1],
                                      hdk=hdk, hdv=hdv,
                                      fillers=[chunk_fillers[c](h) for h in range(GLA_HEADS)])

        if do_head:
            z = proj(_Z)
            silu_z = z * _sigmoid(z)

        if do_tail:
            for rows in fast_chunks:
                r = out_ref[rows, :]
                mu = jnp.mean(r, axis=-1, keepdims=True)
                rc = r - mu
                var = jnp.mean(rc * rc, axis=-1, keepdims=True)
                out_ref[rows, :] = (rc * lax.rsqrt(var + NORM_EPS) * lnw_ref[...]
                                    + lnb_ref[...]).astype(out_ref.dtype)

        if do_head:
            p_parts =[_dot(pooled_in[gi], poolw_ref[gs, :]) for gi, gs in enumerate(gslices)]
            gate_b = _sigmoid(proj(_GATE_B))
            y_b = (jnp.concatenate(p_parts, axis=-1) + poolb_ref[...]) * pools_ref[...] * silu_z
            resid_ref[...] = alpha * xf
            merged_b_ref[...] = gate_b * _dot(y_b.astype(BF16), wb_ref[...])

    pl.when(s < n_tiles)(functools.partial(step, True, True))
    pl.when(s == n_tiles)(functools.partial(step, False, True))

    @pl.when(s < n_tiles)
    def _():
        @pl.when(factorisable_ref[0] == 0)
        def _():
            span = GLA_FAST_CHUNK
            row = lax.broadcasted_iota(jnp.int32, (span, span), 0)
            col = lax.broadcasted_iota(jnp.int32, (span, span), 1)
            tri_chunk = jnp.where(row // GLA_CHUNK == col // GLA_CHUNK, tri_ref[...], jnp.zeros((), BF16))
            la_hi, la_lo = _hi_lo(la_ref[...])
            for rows in fast_chunks:
                b_ref[rows, :] = _dot(tri_chunk, la_hi[rows, :]) + _dot(tri_chunk, la_lo[rows, :])
            _gla_tile_direct(qs_ref, k_ref, b_ref, v_ref, o_ref, state_ref, new_state_ref, hdk=hdk, hdv=hdv)

        state_ref[...] = new_state_ref[...]


def _const_spec(shape):
    nd = len(shape)
    return pl.BlockSpec(shape, lambda s: (0,) * nd, pipeline_mode=pl.Buffered(1))


def _tri_matrix():
    r = np.arange(GLA_FAST_CHUNK)
    return jnp.asarray((r[None, :] <= r[:, None]).astype(np.float32), dtype=BF16)


def _layer(x, w_in_t, w_gate_up, b_gate, gn_w, pool_w, pool_b, pool_scale, w_a, w_b, w_o, ln_w, ln_b,
           *, alpha):
    bsz, seq, d_model = x.shape
    rank, dk = w_gate_up.shape
    dv = gn_w.shape[0]
    assert seq % SEQ_TILE == 0 and SEQ_TILE % GLA_FAST_CHUNK == 0 and GLA_FAST_CHUNK % GLA_CHUNK == 0
    assert dv == d_model and 2 * dk == d_model and rank == GLA_GATE_RANK
    assert pool_w.shape[0] == len(POOL_WINDOWS)
    assert all(w & (w - 1) == 0 and w <= POOL_LOOKBACK for w in POOL_WINDOWS)
    d_in = w_in_t.shape[0]
    gdim = pool_w.shape[1]
    row = lambda a: a.reshape(1, -1)
    hbm_weights = (w_in_t, w_a, w_b, w_o, pool_w.reshape(d_model, gdim))
    small = (w_gate_up, row(b_gate), row(gn_w), row(pool_b), row(pool_scale), row(ln_w), row(ln_b),
             _tri_matrix())
    operands = (x, *hbm_weights, *small)
    tiles_per_seq = seq // SEQ_TILE
    n_tiles = bsz * tiles_per_seq

    def tile_block(tile):
        return (tile // tiles_per_seq, tile % tiles_per_seq, 0)

    in_specs = [pl.BlockSpec((None, SEQ_TILE, d_model), lambda s: tile_block(jnp.minimum(s, n_tiles - 1)))]
    in_specs += [pl.BlockSpec(memory_space=pl.ANY) for _ in hbm_weights]
    in_specs += [_const_spec(a.shape) for a in small]
    kernel = functools.partial(_block_kernel, d_model=d_model, dk=dk, dv=dv, alpha=alpha,
                               tiles_per_seq=tiles_per_seq, n_tiles=n_tiles)
    state_shape = (GLA_HEADS, dk // GLA_HEADS, dv // GLA_HEADS)
    return pl.pallas_call(
        kernel,
        grid=(n_tiles + 1,),
        in_specs=in_specs,
        out_specs=pl.BlockSpec((None, SEQ_TILE, d_model), lambda s: tile_block(jnp.maximum(s - 1, 0))),
        out_shape=jax.ShapeDtypeStruct(x.shape, x.dtype),
        scratch_shapes=[
            pltpu.VMEM((SEQ_TILE, dk), F32),
            pltpu.VMEM((SEQ_TILE, dk), F32),
            pltpu.VMEM((SEQ_TILE, dk), F32),
            pltpu.VMEM((SEQ_TILE, dv), BF16),
            pltpu.VMEM((SEQ_TILE, dv), F32),
            pltpu.VMEM(state_shape, F32),
            pltpu.VMEM(state_shape, F32),
            pltpu.VMEM(state_shape, F32),
            pltpu.VMEM((POOL_LOOKBACK, d_model), F32),
            pltpu.VMEM((SEQ_TILE, dv), F32),
            pltpu.VMEM((SEQ_TILE, d_model), F32),
            pltpu.VMEM((SEQ_TILE, d_model), F32),
            pltpu.VMEM((SEQ_TILE, d_model), F32),
            pltpu.VMEM((SEQ_TILE, dk), F32),
            pltpu.SMEM((1,), jnp.int32),
            pltpu.VMEM((d_model, GATE_PAD + d_in - rank), BF16),
            pltpu.VMEM(w_a.shape, BF16),
            pltpu.VMEM(w_b.shape, BF16),
            pltpu.VMEM(w_o.shape, BF16),
            pltpu.VMEM((d_model, gdim), BF16),
            pltpu.VMEM((WEIGHT_STAGE_SLOTS, WEIGHT_LOAD_ROWS, d_model), F32),
            pltpu.VMEM((rank, d_model), F32),
            pltpu.VMEM((d_model, gdim), F32),
            pltpu.SemaphoreType.DMA((WEIGHT_STAGE_SLOTS + 2,)),
        ],
        compiler_params=pltpu.CompilerParams(
            dimension_semantics=("arbitrary",),
            vmem_limit_bytes=VMEM_LIMIT_BYTES),
        name="hybrid_gla_pool_layer",
    )(*operands)


def kernel(x, w_in, w_gate_up, b_gate, gn_w, pool_w, pool_b, pool_scale, w_a, w_b, w_o, ln_w, ln_b):
    depth = w_in.shape[0]
    alpha = (2.0 * depth) ** 0.25
    w_in_t = jnp.swapaxes(w_in, 1, 2)
    for l in range(depth):
        x = _layer(x, w_in_t[l], w_gate_up[l], b_gate[l], gn_w[l], pool_w[l], pool_b[l], pool_scale[l],
                   w_a[l], w_b[l], w_o[l], ln_w[l], ln_b[l], alpha=alpha)
    return x
```
